```python
import math
import jax, jax.numpy as jnp
from jax import lax
import numpy as np

D_MODEL = 2048
BATCH = 1
SEQ = 8192
DEPTH = 2
DEC_BATCH = 32
DEC_SEQ = 4
PAST_LEN = 8192
PAGE_SIZE = 128

H_A = 4
HD_A = 128
H_B = 8
HD_B = 128
FORGET_BIAS = 4.0
H_C = 16
HD_C = 128
C_PATTERNS = ((128, 1), (512, 4), (2048, 16))
C_WMAX = 2048
H_X = 4
HD_X = 128
N_MEM = 256
D_FF = -(-(8 * D_MODEL) // (3 * 256)) * 256
QBLK = 128
EPS = 1e-6
N_EVEN = (DEPTH + 1) // 2
N_ODD = DEPTH // 2
EVEN_SIZES = (H_A * 2 * HD_A, H_A * 2 * HD_A, H_A * 2 * HD_A, H_B * HD_B, H_B * HD_B, H_B * HD_B, H_B)
EVEN_IN = sum(EVEN_SIZES)
EVEN_MIX = H_A * 2 * HD_A + H_B * HD_B
ODD_MIX = H_C * HD_C

kernel_name = 'hybrid_diff_fox_dilated_decode_step'


def _rmsnorm(x, g):
    xf = x.astype(jnp.float32)
    y = xf * lax.rsqrt(jnp.mean(xf * xf, axis=-1, keepdims=True) + EPS)
    return (y * g.astype(jnp.float32)).astype(x.dtype)


def _alibi_slopes(n):
    return jnp.asarray(2.0 ** (-8.0 * np.arange(1, n + 1) / n), dtype=jnp.float32)


def _unblock(y):
    nb, b, q = y.shape[:3]
    return jnp.moveaxis(y, 0, 1).reshape(b, nb * q, *y.shape[3:])


def _swiglu(xn, w_gate_up, w_down):
    g, u = jnp.split(xn @ w_gate_up, 2, axis=-1)
    return (jax.nn.silu(g) * u) @ w_down


def _mem_kv(mem, g, w_kv):
    b, m, _ = mem.shape
    k, v = jnp.split(_rmsnorm(mem, g) @ w_kv, 2, axis=-1)
    return k.reshape(b, m, H_X, HD_X), v.reshape(b, m, H_X, HD_X)


def _cross_attn(xn, mem_k, mem_v, w_q, w_o):
    b, t, _ = xn.shape
    q = (xn @ w_q).reshape(b, t, H_X, HD_X)
    s = jnp.einsum('bthd,bmhd->bhtm', q, mem_k).astype(jnp.float32) * HD_X ** -0.5
    p = jax.nn.softmax(s, axis=-1)
    o = jnp.einsum('bhtm,bmhd->bthd', p, mem_v.astype(jnp.float32))
    return o.reshape(b, t, H_X * HD_X).astype(xn.dtype) @ w_o


def _diff_lambda(lq1, lk1, lq2, lk2, lam_init):
    f32 = jnp.float32
    return (jnp.exp(jnp.sum(lq1.astype(f32) * lk1.astype(f32)))
            - jnp.exp(jnp.sum(lq2.astype(f32) * lk2.astype(f32))) + lam_init)


def _diff_attn(q, q_pos, segs, slopes, lam):
    scores = []
    for k, _, k_pos in segs:
        s = jnp.einsum('bqhmd,bkhmd->bhmqk', q, k).astype(jnp.float32) * HD_A ** -0.5
        dist = q_pos[:, None] - k_pos[None, :]
        s = s - slopes[:, None, None, None] * dist.astype(jnp.float32)
        scores.append(jnp.where(dist >= 0, s, -jnp.inf))
    p = jax.nn.softmax(jnp.concatenate(scores, axis=-1), axis=-1)
    p = p[:, :, 0] - lam * p[:, :, 1]
    outs, off = [], 0
    for _, v, k_pos in segs:
        n = k_pos.shape[0]
        outs.append(jnp.einsum('bhqk,bkhe->bqhe', p[..., off:off + n], v.astype(jnp.float32)))
        off += n
    return sum(outs)


def _diff_post(o, g, lam_init):
    y = o * lax.rsqrt(jnp.mean(o * o, axis=-1, keepdims=True) + EPS) * g.astype(jnp.float32) * (1.0 - lam_init)
    return y.reshape(o.shape[0], o.shape[1], -1)


def _forget_attn(q, cq, q_pos, segs):
    cq_t = jnp.swapaxes(cq, 1, 2)[..., :, None]
    scores = []
    for k, _, ck, k_pos in segs:
        s = jnp.einsum('bqhd,bkhd->bhqk', q, k).astype(jnp.float32) * HD_B ** -0.5
        s = s + cq_t - jnp.swapaxes(ck, 1, 2)[..., None, :]
        dist = q_pos[:, None] - k_pos[None, :]
        scores.append(jnp.where(dist >= 0, s, -jnp.inf))
    p = jax.nn.softmax(jnp.concatenate(scores, axis=-1), axis=-1)
    outs, off = [], 0
    for _, v, _, k_pos in segs:
        n = k_pos.shape[0]
        outs.append(jnp.einsum('bhqk,bkhd->bqhd', p[..., off:off + n], v.astype(jnp.float32)))
        off += n
    return sum(outs)


def _even_project(xn, w_in, b_f):
    b, t, _ = xn.shape
    idx = np.cumsum(EVEN_SIZES)[:-1].tolist()
    qa, ka, va, qb, kb, vb, fz = jnp.split(xn @ w_in, idx, axis=-1)
    logf = jax.nn.log_sigmoid(fz.astype(jnp.float32) + b_f.astype(jnp.float32))
    return (qa.reshape(b, t, H_A, 2, HD_A), ka.reshape(b, t, H_A, 2, HD_A), va.reshape(b, t, H_A, 2 * HD_A),
            qb.reshape(b, t, H_B, HD_B), kb.reshape(b, t, H_B, HD_B), vb.reshape(b, t, H_B, HD_B), logf)


def _even_merge(oa, ob, g_sub, lam_init, w_out, dtype):
    b, t = ob.shape[:2]
    z = jnp.concatenate([_diff_post(oa, g_sub, lam_init).astype(dtype),
                         ob.reshape(b, t, H_B * HD_B).astype(dtype)], axis=-1)
    return z @ w_out


def _even_prompt(xn, w_in, b_f, lam, lam_init, g_sub, w_out):
    b, s_len, _ = xn.shape
    qa, ka, va, qb, kb, vb, logf = _even_project(xn, w_in, b_f)
    c = jnp.cumsum(logf, axis=1)
    k_pos = jnp.arange(s_len)
    slopes = _alibi_slopes(H_A)

    def block(i):
        t0 = i * QBLK
        q_pos = t0 + jnp.arange(QBLK)
        sl = lambda a: lax.dynamic_slice_in_dim(a, t0, QBLK, axis=1)
        oa = _diff_attn(sl(qa), q_pos, [(ka, va, k_pos)], slopes, lam)
        ob = _forget_attn(sl(qb), sl(c), q_pos, [(kb, vb, c, k_pos)])
        return oa, ob

    oa, ob = lax.map(block, jnp.arange(s_len // QBLK))
    y = _even_merge(_unblock(oa), _unblock(ob), g_sub, lam_init, w_out, xn.dtype)
    return y, (ka, va, kb, vb, logf)


def _even_sample(xn, pool_a_k, pool_a_v, pool_b_k, pool_b_v, pool_b_logf, page_table,
                 w_in, b_f, lam, lam_init, g_sub, w_out):
    b, t, _ = xn.shape
    qa, ka, va, qb, kb, vb, logf = _even_project(xn, w_in, b_f)
    p_len = page_table.shape[1] * PAGE_SIZE

    def gather(pool):
        g = pool[page_table]
        return g.reshape(b, p_len, *pool.shape[2:])

    lf_past = gather(pool_b_logf).astype(jnp.float32)
    c_all = jnp.cumsum(jnp.concatenate([lf_past, logf], axis=1), axis=1)
    c_past, c_new = c_all[:, :p_len], c_all[:, p_len:]
    q_pos = p_len + jnp.arange(t)
    p_pos = jnp.arange(p_len)
    oa = _diff_attn(qa, q_pos, [(gather(pool_a_k), gather(pool_a_v), p_pos), (ka, va, q_pos)],
                    _alibi_slopes(H_A), lam)
    ob = _forget_attn(qb, c_new, q_pos, [(gather(pool_b_k), gather(pool_b_v), c_past, p_pos),
                                         (kb, vb, c_new, q_pos)])
    y = _even_merge(oa, ob, g_sub, lam_init, w_out, xn.dtype)
    return y, (ka, va, kb, vb, logf)


def _odd_project(xn, w_in):
    b, t, _ = xn.shape
    q, k, v = jnp.split(xn @ w_in, 3, axis=-1)
    return (q.reshape(b, t, H_C, HD_C), k.reshape(b, t, H_C, HD_C), v.reshape(b, t, H_C, HD_C))


def _front_pad(a):
    return jnp.pad(a, ((0, 0), (C_WMAX, 0), (0, 0), (0, 0)))


def _dilated_attn(q, q_idx, k_pad, v_pad, slopes):
    ms, ls, nums = [], [], []
    for w, d in C_PATTERNS:
        steps = jnp.arange(w // d + 1)
        rel = q_idx[:, None] - d * steps[None, :]
        kg = jnp.take(k_pad, rel + C_WMAX, axis=1)
        vg = jnp.take(v_pad, rel + C_WMAX, axis=1)
        s = jnp.einsum('bqhd,bqkhd->bhqk', q, kg).astype(jnp.float32) * HD_C ** -0.5
        s = s - slopes[:, None, None] * (d * steps).astype(jnp.float32)
        s = jnp.where(rel >= 0, s, -jnp.inf)
        m = jnp.max(s, axis=-1)
        e = jnp.exp(s - m[..., None])
        ms.append(m)
        ls.append(jnp.sum(e, axis=-1))
        nums.append(jnp.einsum('bhqk,bqkhd->bqhd', e, vg.astype(jnp.float32)))
    m_all = jnp.max(jnp.stack(ms), axis=0)
    wts = [jnp.exp(m - m_all) for m in ms]
    num = sum(n * jnp.swapaxes(wt, 1, 2)[..., None] for n, wt in zip(nums, wts))
    den = sum(l * wt for l, wt in zip(ls, wts))
    return num / jnp.swapaxes(den, 1, 2)[..., None]


def _odd_prompt(xn, w_in, w_out):
    b, s_len, _ = xn.shape
    q, k, v = _odd_project(xn, w_in)
    k_pad, v_pad = _front_pad(k), _front_pad(v)
    slopes = _alibi_slopes(H_C)

    def block(i):
        t0 = i * QBLK
        q_idx = t0 + jnp.arange(QBLK)
        return _dilated_attn(lax.dynamic_slice_in_dim(q, t0, QBLK, axis=1), q_idx, k_pad, v_pad, slopes)

    o = _unblock(lax.map(block, jnp.arange(s_len // QBLK)))
    y = o.reshape(b, s_len, ODD_MIX).astype(xn.dtype) @ w_out
    keep = min(C_WMAX, s_len)
    return y, (k[:, s_len - keep:], v[:, s_len - keep:])


def _odd_sample(xn, buf_k, buf_v, w_in, w_out):
    b, t, _ = xn.shape
    q, k, v = _odd_project(xn, w_in)
    buf_len = buf_k.shape[1]
    k_seq = jnp.concatenate([buf_k.astype(k.dtype), k], axis=1)
    v_seq = jnp.concatenate([buf_v.astype(v.dtype), v], axis=1)
    q_idx = buf_len + jnp.arange(t)
    o = _dilated_attn(q, q_idx, _front_pad(k_seq), _front_pad(v_seq), _alibi_slopes(H_C))
    y = o.reshape(b, t, ODD_MIX).astype(xn.dtype) @ w_out
    return y, (k_seq[:, -buf_len:], v_seq[:, -buf_len:])


def setup_inputs(seed: int = 0) -> dict:
    key = jax.random.key(seed)
    keys = iter(jax.random.split(key, 40))
    f32 = jnp.float32

    def nrm(shape, scale=1.0):
        return jax.random.normal(next(keys), shape, f32) * scale

    def gain(shape):
        return 1.0 + nrm(shape, 0.01)

    n_pages = PAST_LEN // PAGE_SIZE
    n_used = DEC_BATCH * n_pages
    n_pool = n_used + max(1, n_used // 4)
    win = min(C_WMAX, PAST_LEN)
    x_prompt = nrm((BATCH, SEQ, D_MODEL))
    x_sample = nrm((DEC_BATCH, DEC_SEQ, D_MODEL))
    mem_prompt = nrm((BATCH, N_MEM, D_MODEL))
    cache_a_k = nrm((N_EVEN, n_pool, PAGE_SIZE, H_A, 2, HD_A))
    cache_a_v = nrm((N_EVEN, n_pool, PAGE_SIZE, H_A, 2 * HD_A))
    cache_b_k = nrm((N_EVEN, n_pool, PAGE_SIZE, H_B, HD_B))
    cache_b_v = nrm((N_EVEN, n_pool, PAGE_SIZE, H_B, HD_B))
    cache_b_logf = jax.nn.log_sigmoid(FORGET_BIAS + nrm((N_EVEN, n_pool, PAGE_SIZE, H_B)))
    cache_c_k = nrm((N_ODD, DEC_BATCH, win, H_C, HD_C))
    cache_c_v = nrm((N_ODD, DEC_BATCH, win, H_C, HD_C))
    cache_mem_k = nrm((DEPTH, DEC_BATCH, N_MEM, H_X, HD_X))
    cache_mem_v = nrm((DEPTH, DEC_BATCH, N_MEM, H_X, HD_X))
    page_table = jax.random.permutation(next(keys), n_pool)[:n_used].reshape(DEC_BATCH, n_pages).astype(jnp.int32)
    return {
        'x_prompt': x_prompt,
        'x_sample': x_sample,
        'mem_prompt': mem_prompt,
        'cache_a_k': cache_a_k,
        'cache_a_v': cache_a_v,
        'cache_b_k': cache_b_k,
        'cache_b_v': cache_b_v,
        'cache_b_logf': cache_b_logf,
        'cache_c_k': cache_c_k,
        'cache_c_v': cache_c_v,
        'cache_mem_k': cache_mem_k,
        'cache_mem_v': cache_mem_v,
        'page_table': page_table,
        'norm_mix': gain((DEPTH, D_MODEL)),
        'norm_cross': gain((DEPTH, D_MODEL)),
        'norm_mem': gain((DEPTH, D_MODEL)),
        'norm_ffn': gain((DEPTH, D_MODEL)),
        'norm_final': gain((D_MODEL,)),
        'w_in_even': nrm((N_EVEN, D_MODEL, EVEN_IN), D_MODEL ** -0.5),
        'b_forget': FORGET_BIAS + nrm((N_EVEN, H_B), 0.1),
        'lambda_q1': nrm((N_EVEN, HD_A), 0.1),
        'lambda_k1': nrm((N_EVEN, HD_A), 0.1),
        'lambda_q2': nrm((N_EVEN, HD_A), 0.1),
        'lambda_k2': nrm((N_EVEN, HD_A), 0.1),
        'subln_a': gain((N_EVEN, 2 * HD_A)),
        'w_out_even': nrm((N_EVEN, EVEN_MIX, D_MODEL), EVEN_MIX ** -0.5),
        'w_in_odd': nrm((N_ODD, D_MODEL, 3 * ODD_MIX), D_MODEL ** -0.5),
        'w_out_odd': nrm((N_ODD, ODD_MIX, D_MODEL), ODD_MIX ** -0.5),
        'w_xq': nrm((DEPTH, D_MODEL, H_X * HD_X), D_MODEL ** -0.5),
        'w_xkv': nrm((DEPTH, D_MODEL, 2 * H_X * HD_X), D_MODEL ** -0.5),
        'w_xo': nrm((DEPTH, H_X * HD_X, D_MODEL), (H_X * HD_X) ** -0.5),
        'w_gate_up': nrm((DEPTH, D_MODEL, 2 * D_FF), D_MODEL ** -0.5),
        'w_down': nrm((DEPTH, D_FF, D_MODEL), D_FF ** -0.5),
    }


def reference(x_prompt, x_sample, mem_prompt, cache_a_k, cache_a_v, cache_b_k, cache_b_v, cache_b_logf,
              cache_c_k, cache_c_v, cache_mem_k, cache_mem_v, page_table,
              norm_mix, norm_cross, norm_mem, norm_ffn, norm_final,
              w_in_even, b_forget, lambda_q1, lambda_k1, lambda_q2, lambda_k2, subln_a, w_out_even,
              w_in_odd, w_out_odd, w_xq, w_xkv, w_xo, w_gate_up, w_down):
    xp, xs = x_prompt, x_sample
    ak_p, av_p, bk_p, bv_p, bl_p, ck_p, cv_p, mk_p, mv_p = [], [], [], [], [], [], [], [], []
    ak_s, av_s, bk_s, bv_s, bl_s, ck_s, cv_s = [], [], [], [], [], [], []
    for l in range(DEPTH):
        j = l // 2
        if l % 2 == 0:
            lam_init = 0.8 - 0.6 * math.exp(-0.3 * l)
            lam = _diff_lambda(lambda_q1[j], lambda_k1[j], lambda_q2[j], lambda_k2[j], lam_init)
            yp, (ka, va, kb, vb, lf) = _even_prompt(_rmsnorm(xp, norm_mix[l]), w_in_even[j], b_forget[j],
                                                   lam, lam_init, subln_a[j], w_out_even[j])
            ak_p.append(ka); av_p.append(va); bk_p.append(kb); bv_p.append(vb); bl_p.append(lf)
            ys, (ka, va, kb, vb, lf) = _even_sample(_rmsnorm(xs, norm_mix[l]), cache_a_k[j], cache_a_v[j],
                                                   cache_b_k[j], cache_b_v[j], cache_b_logf[j], page_table,
                                                   w_in_even[j], b_forget[j], lam, lam_init, subln_a[j],
                                                   w_out_even[j])
            ak_s.append(ka); av_s.append(va); bk_s.append(kb); bv_s.append(vb); bl_s.append(lf)
        else:
            yp, (kc, vc) = _odd_prompt(_rmsnorm(xp, norm_mix[l]), w_in_odd[j], w_out_odd[j])
            ck_p.append(kc); cv_p.append(vc)
            ys, (kc, vc) = _odd_sample(_rmsnorm(xs, norm_mix[l]), cache_c_k[j], cache_c_v[j],
                                       w_in_odd[j], w_out_odd[j])
            ck_s.append(kc); cv_s.append(vc)
        xp = xp + yp
        xs = xs + ys
        mk, mv = _mem_kv(mem_prompt, norm_mem[l], w_xkv[l])
        mk_p.append(mk); mv_p.append(mv)
        xp = xp + _cross_attn(_rmsnorm(xp, norm_cross[l]), mk, mv, w_xq[l], w_xo[l])
        xs = xs + _cross_attn(_rmsnorm(xs, norm_cross[l]), cache_mem_k[l], cache_mem_v[l], w_xq[l], w_xo[l])
        xp = xp + _swiglu(_rmsnorm(xp, norm_ffn[l]), w_gate_up[l], w_down[l])
        xs = xs + _swiglu(_rmsnorm(xs, norm_ffn[l]), w_gate_up[l], w_down[l])
    y_prompt = _rmsnorm(xp, norm_final)
    y_sample = _rmsnorm(xs, norm_final)
    st = jnp.stack
    return (y_prompt, y_sample,
            st(ak_p), st(av_p), st(bk_p), st(bv_p), st(bl_p), st(ck_p), st(cv_p), st(mk_p), st(mv_p),
            st(ak_s), st(av_s), st(bk_s), st(bv_s), st(bl_s), st(ck_s), st(cv_s))
```

```python
import functools
import math

import numpy as np
import jax
import jax.numpy as jnp
from jax import lax
from jax.experimental import pallas as pl
from jax.experimental.pallas import tpu as pltpu

F32 = jnp.float32
BF16 = jnp.bfloat16

EPS = 1e-6
PAGE_SIZE = 128
H_A, HD_A = 4, 128
H_B, HD_B = 8, 128
H_C, HD_C = 16, 128
H_X, HD_X = 4, 128
C_PATTERNS = ((128, 1), (512, 4), (2048, 16))
C_WMAX = 2048
N_STEPS = 128
LANES = 128
VMEM_LIMIT = 52 * 1024 * 1024
NEG_INF = float("-inf")


def _params(*sem):
    return pltpu.CompilerParams(dimension_semantics=sem, vmem_limit_bytes=VMEM_LIMIT)


def _tile_lanes(x, reps):
    return x if reps == 1 else jnp.concatenate([x] * reps, axis=1)


def _pick(n, pref):
    for t in pref:
        if n % t == 0:
            return t
    return n


def _rms_kernel(x_ref, g_ref, o_ref):
    x = x_ref[...]
    ms = jnp.mean(x * x, axis=-1, keepdims=True)
    o_ref[...] = ((x * lax.rsqrt(ms + EPS)) * g_ref[...]).astype(o_ref.dtype)


def _rmsnorm(x, g, out_dtype):
    m, d = x.shape
    tm = _pick(m, (512, 256, 128))
    return pl.pallas_call(
        _rms_kernel,
        grid=(m // tm,),
        in_specs=[pl.BlockSpec((tm, d), lambda i: (i, 0)),
                  pl.BlockSpec((1, d), lambda i: (0, 0))],
        out_specs=pl.BlockSpec((tm, d), lambda i: (i, 0)),
        out_shape=jax.ShapeDtypeStruct((m, d), out_dtype),
        compiler_params=_params("parallel"),
        name="rmsnorm",
    )(x, g.reshape(1, d).astype(F32))


def _mm_kernel(*refs, n_pairs, has_res, out_f32, out_bf16):
    acc = None
    for p in range(n_pairs):
        y = jnp.dot(refs[2 * p][...], refs[2 * p + 1][...], preferred_element_type=F32)
        acc = y if acc is None else acc + y
    pos = 2 * n_pairs
    if has_res:
        acc = refs[pos][...] + acc
        pos += 1
    if out_f32:
        refs[pos][...] = acc
        pos += 1
    if out_bf16:
        refs[pos][...] = acc.astype(BF16)


def _matmul(pairs, res=None, out_f32=True, out_bf16=False):
    m = pairs[0][0].shape[0]
    n = pairs[0][1].shape[1]
    tm = _pick(m, (512, 256, 128))
    tn = _pick(n, (1024, 512, 256, 128))
    in_specs, args = [], []
    for a, w in pairs:
        k = a.shape[1]
        in_specs += [pl.BlockSpec((tm, k), lambda i, j: (i, 0)),
                     pl.BlockSpec((k, tn), lambda i, j: (0, j))]
        args += [a, w]
    if res is not None:
        in_specs.append(pl.BlockSpec((tm, tn), lambda i, j: (i, j)))
        args.append(res)
    out_specs, out_shape = [], []
    if out_f32:
        out_specs.append(pl.BlockSpec((tm, tn), lambda i, j: (i, j)))
        out_shape.append(jax.ShapeDtypeStruct((m, n), F32))
    if out_bf16:
        out_specs.append(pl.BlockSpec((tm, tn), lambda i, j: (i, j)))
        out_shape.append(jax.ShapeDtypeStruct((m, n), BF16))
    outs = pl.pallas_call(
        functools.partial(_mm_kernel, n_pairs=len(pairs), has_res=res is not None,
                          out_f32=out_f32, out_bf16=out_bf16),
        grid=(m // tm, n // tn),
        in_specs=in_specs, out_specs=out_specs, out_shape=out_shape,
        compiler_params=_params("parallel", "parallel"),
        name="matmul",
    )(*args)
    return outs[0] if len(outs) == 1 else tuple(outs)


def _ffn_kernel(xn_ref, wg_ref, wu_ref, wd_ref, res_ref, o_ref, acc_ref):
    f = pl.program_id(1)

    @pl.when(f == 0)
    def _():
        acc_ref[...] = jnp.zeros_like(acc_ref)

    xn = xn_ref[...]
    g = jnp.dot(xn, wg_ref[...], preferred_element_type=F32)
    u = jnp.dot(xn, wu_ref[...], preferred_element_type=F32)
    h = (g * jax.nn.sigmoid(g)) * u
    acc_ref[...] += jnp.dot(h.astype(BF16), wd_ref[...], preferred_element_type=F32)

    @pl.when(f == pl.num_programs(1) - 1)
    def _():
        o_ref[...] = res_ref[...] + acc_ref[...]


def _ffn(xn, wg, wu, wd, res):
    m, d = xn.shape
    dff = wg.shape[1]
    tm = _pick(m, (512, 256, 128))
    tf = _pick(dff, (512, 256, 128))
    return pl.pallas_call(
        _ffn_kernel,
        grid=(m // tm, dff // tf),
        in_specs=[pl.BlockSpec((tm, d), lambda i, f: (i, 0)),
                  pl.BlockSpec((d, tf), lambda i, f: (0, f)),
                  pl.BlockSpec((d, tf), lambda i, f: (0, f)),
                  pl.BlockSpec((tf, d), lambda i, f: (f, 0)),
                  pl.BlockSpec((tm, d), lambda i, f: (i, 0))],
        out_specs=pl.BlockSpec((tm, d), lambda i, f: (i, 0)),
        out_shape=jax.ShapeDtypeStruct((m, d), F32),
        scratch_shapes=[pltpu.VMEM((tm, d), F32)],
        compiler_params=_params("parallel", "arbitrary"),
        name="swiglu",
    )(xn, wg, wu, wd, res)


def _log_sigmoid(x):
    return jnp.minimum(x, 0.0) - jnp.log1p(jnp.exp(-jnp.abs(x)))


def _logf_kernel(fz_ref, b_ref, lf_ref, c_ref):
    lf = _log_sigmoid(fz_ref[...] + b_ref[...])
    lf_ref[...] = lf
    n = lf.shape[1]
    lane = lax.broadcasted_iota(jnp.int32, lf.shape, 1)
    c = lf
    shift = 1
    while shift < n:
        c = c + jnp.where(lane >= shift, pltpu.roll(c, shift, axis=1), 0.0)
        shift *= 2
    c_ref[...] = c


def _logf_cumsum(fz_t, b_f):
    h, n = fz_t.shape
    return pl.pallas_call(
        _logf_kernel,
        out_shape=[jax.ShapeDtypeStruct((h, n), F32)] * 2,
        name="logf_cumsum",
    )(fz_t, b_f.reshape(h, 1).astype(F32))


def _fox_kernel(q_ref, k_ref, v_ref, ccol_ref, crow_ref, o_ref, m_s, l_s, acc_s, *, t, scale):
    i = pl.program_id(1)
    q = q_ref[...]
    cq = jnp.broadcast_to(ccol_ref[...], (t, LANES))
    m_s[...] = jnp.full_like(m_s, NEG_INF)
    l_s[...] = jnp.zeros_like(l_s)
    acc_s[...] = jnp.zeros_like(acc_s)
    reps = t // LANES

    def chunk(j, masked):
        start = pl.multiple_of(j * t, t)
        k = k_ref[pl.ds(start, t), :]
        v = v_ref[pl.ds(start, t), :]
        s = lax.dot_general(q, k, (((1,), (1,)), ((), ())), preferred_element_type=F32) * scale
        s = s + (_tile_lanes(cq, reps) - crow_ref[j])
        if masked:
            row = lax.broadcasted_iota(jnp.int32, (t, t), 0)
            col = lax.broadcasted_iota(jnp.int32, (t, t), 1)
            s = jnp.where(row >= col, s, NEG_INF)
        m_prev = m_s[...]
        m_new = jnp.maximum(m_prev, jnp.max(s, axis=1, keepdims=True))
        alpha = jnp.exp(m_prev - m_new)
        p = jnp.exp(s - _tile_lanes(m_new, reps))
        l_s[...] = alpha * l_s[...] + jnp.sum(p, axis=1, keepdims=True)
        acc_s[...] = alpha * acc_s[...] + jnp.dot(p.astype(BF16), v, preferred_element_type=F32)
        m_s[...] = m_new

    def body(j, c):
        chunk(j, False)
        return c

    lax.fori_loop(0, i, body, 0)
    chunk(i, True)
    o_ref[...] = (acc_s[...] / l_s[...]).astype(o_ref.dtype)


def _fox_attention(qb, kb, vb, c_t):
    s_len = qb.shape[0]
    t = _pick(s_len, (512, 256, 128))
    nc = s_len // t
    c_col = c_t.reshape(H_B, s_len, 1)
    c_row = c_t.reshape(H_B, nc, 1, t)
    return pl.pallas_call(
        functools.partial(_fox_kernel, t=t, scale=HD_B ** -0.5),
        grid=(H_B, nc),
        in_specs=[pl.BlockSpec((t, HD_B), lambda h, i: (i, h)),
                  pl.BlockSpec((s_len, HD_B), lambda h, i: (0, h)),
                  pl.BlockSpec((s_len, HD_B), lambda h, i: (0, h)),
                  pl.BlockSpec((None, t, 1), lambda h, i: (h, i, 0)),
                  pl.BlockSpec((None, nc, 1, t), lambda h, i: (h, 0, 0, 0))],
        out_specs=pl.BlockSpec((t, HD_B), lambda h, i: (i, h)),
        out_shape=jax.ShapeDtypeStruct((s_len, H_B * HD_B), BF16),
        scratch_shapes=[pltpu.VMEM((t, LANES), F32), pltpu.VMEM((t, LANES), F32),
                        pltpu.VMEM((t, HD_B), F32)],
        compiler_params=_params("parallel", "parallel"),
        name="forget_attention",
    )(qb, kb, vb, c_col, c_row)


def _diff_lambda_vec(lq1_ref, lk1_ref, lq2_ref, lk2_ref, lam_init):
    a = jnp.sum(lq1_ref[...] * lk1_ref[...], axis=1, keepdims=True)
    b = jnp.sum(lq2_ref[...] * lk2_ref[...], axis=1, keepdims=True)
    return jnp.exp(a) - jnp.exp(b) + lam_init


def _diff_kernel(q_ref, k_ref, v_ref, slope_ref, lq1_ref, lk1_ref, lq2_ref, lk2_ref, g_ref,
                 o_ref, m1_s, l1_s, a1_s, m2_s, l2_s, a2_s, *, t, scale, lam_init):
    i = pl.program_id(1)
    q1 = q_ref[:, :HD_A]
    q2 = q_ref[:, HD_A:]
    slope = slope_ref[...]
    row_pos = (i * t + lax.broadcasted_iota(jnp.int32, (t, LANES), 0)).astype(F32)
    row_term = slope * row_pos
    for m_s, l_s, a_s in ((m1_s, l1_s, a1_s), (m2_s, l2_s, a2_s)):
        m_s[...] = jnp.full_like(m_s, NEG_INF)
        l_s[...] = jnp.zeros_like(l_s)
        a_s[...] = jnp.zeros_like(a_s)
    reps = t // LANES

    def chunk(j, masked):
        start = pl.multiple_of(j * t, t)
        k = k_ref[pl.ds(start, t), :]
        v = v_ref[pl.ds(start, t), :]
        col_pos = (j * t + lax.broadcasted_iota(jnp.int32, (1, t), 1)).astype(F32)
        bias = _tile_lanes(slope, reps) * col_pos - _tile_lanes(row_term, reps)
        if masked:
            row = lax.broadcasted_iota(jnp.int32, (t, t), 0)
            col = lax.broadcasted_iota(jnp.int32, (t, t), 1)
            bias = jnp.where(row >= col, bias, NEG_INF)
        for qm, km, m_s, l_s, a_s in ((q1, k[:, :HD_A], m1_s, l1_s, a1_s),
                                      (q2, k[:, HD_A:], m2_s, l2_s, a2_s)):
            s = lax.dot_general(qm, km, (((1,), (1,)), ((), ())),
                                preferred_element_type=F32) * scale + bias
            m_prev = m_s[...]
            m_new = jnp.maximum(m_prev, jnp.max(s, axis=1, keepdims=True))
            alpha = jnp.exp(m_prev - m_new)
            p = jnp.exp(s - _tile_lanes(m_new, reps))
            l_s[...] = alpha * l_s[...] + jnp.sum(p, axis=1, keepdims=True)
            a_s[...] = (_tile_lanes(alpha, 2) * a_s[...]
                        + jnp.dot(p.astype(BF16), v, preferred_element_type=F32))
            m_s[...] = m_new

    def body(j, c):
        chunk(j, False)
        return c

    lax.fori_loop(0, i, body, 0)
    chunk(i, True)
    lam = _diff_lambda_vec(lq1_ref, lk1_ref, lq2_ref, lk2_ref, lam_init)
    o = (a1_s[...] / _tile_lanes(l1_s[...], 2)
         - lam * (a2_s[...] / _tile_lanes(l2_s[...], 2)))
    y = o * lax.rsqrt(jnp.mean(o * o, axis=1, keepdims=True) + EPS) * g_ref[...] * (1.0 - lam_init)
    o_ref[...] = y.astype(o_ref.dtype)


def _alibi_slopes(n):
    return jnp.asarray(2.0 ** (-8.0 * np.arange(1, n + 1) / n), dtype=F32)


def _diff_attention(qa, ka, va, lam_params, g_sub, lam_init):
    s_len = qa.shape[0]
    t = _pick(s_len, (512, 256, 128))
    nc = s_len // t
    w = 2 * HD_A
    slopes = jnp.broadcast_to(_alibi_slopes(H_A)[:, None, None], (H_A, 1, LANES))
    vec = pl.BlockSpec((1, HD_A), lambda h, i: (0, 0))
    return pl.pallas_call(
        functools.partial(_diff_kernel, t=t, scale=HD_A ** -0.5, lam_init=lam_init),
        grid=(H_A, nc),
        in_specs=[pl.BlockSpec((t, w), lambda h, i: (i, h)),
                  pl.BlockSpec((s_len, w), lambda h, i: (0, h)),
                  pl.BlockSpec((s_len, w), lambda h, i: (0, h)),
                  pl.BlockSpec((None, 1, LANES), lambda h, i: (h, 0, 0)),
                  vec, vec, vec, vec,
                  pl.BlockSpec((1, w), lambda h, i: (0, 0))],
        out_specs=pl.BlockSpec((t, w), lambda h, i: (i, h)),
        out_shape=jax.ShapeDtypeStruct((s_len, H_A * w), BF16),
        scratch_shapes=[pltpu.VMEM((t, LANES), F32), pltpu.VMEM((t, LANES), F32),
                        pltpu.VMEM((t, w), F32),
                        pltpu.VMEM((t, LANES), F32), pltpu.VMEM((t, LANES), F32),
                        pltpu.VMEM((t, w), F32)],
        compiler_params=_params("parallel", "parallel"),
        name="diff_attention",
    )(qa, ka, va, slopes, *lam_params, g_sub.reshape(1, w).astype(F32))


def _rows_bcast(x, n_t, n_h):
    return jnp.concatenate(
        [jnp.broadcast_to(x[t:t + 1, :], (n_h, x.shape[1])) for t in range(n_t)], axis=0)


def _even_sample_kernel(pt_ref, qa_ref, qb_ref, lfn_ref, kan_ref, van_ref, kbn_ref, vbn_ref,
                        lq1_ref, lk1_ref, lq2_ref, lk2_ref, g_ref, *rest,
                        n_pg, n_t, p_len, scale, lam_init):
    ka_refs = rest[0 * n_pg:1 * n_pg]
    va_refs = rest[1 * n_pg:2 * n_pg]
    kb_refs = rest[2 * n_pg:3 * n_pg]
    vb_refs = rest[3 * n_pg:4 * n_pg]
    lf_refs = rest[4 * n_pg:5 * n_pg]
    oa_ref, ob_ref = rest[5 * n_pg:5 * n_pg + 2]
    qam_s, qbm_s, ma_s, la_s, acca_s, mb_s, lb_s, accb_s, carry_s = rest[5 * n_pg + 2:]
    j = pl.program_id(1)
    n_groups = pl.num_programs(1)
    n_r = n_t * 8
    width = 8 * HD_B

    row = lax.broadcasted_iota(jnp.int32, (n_r, 1), 0)
    row_h = row & 7
    row_t = (row >> 3).astype(F32)
    slope = jnp.exp2(-(8.0 / H_A) * ((row_h >> 1) + 1).astype(F32))

    @pl.when(j == 0)
    def _():
        col_blk = lax.broadcasted_iota(jnp.int32, (n_r, width), 1) >> 7
        own = col_blk == (lax.broadcasted_iota(jnp.int32, (n_r, width), 0) & 7)
        qam_s[...] = jnp.where(own, _rows_bcast(qa_ref[...], n_t, 8), 0.0).astype(BF16)
        qbm_s[...] = jnp.where(own, _rows_bcast(qb_ref[...], n_t, 8), 0.0).astype(BF16)
        for m_s, l_s, a_s in ((ma_s, la_s, acca_s), (mb_s, lb_s, accb_s)):
            m_s[...] = jnp.full_like(m_s, NEG_INF)
            l_s[...] = jnp.zeros_like(l_s)
            a_s[...] = jnp.zeros_like(a_s)
        carry_s[...] = jnp.zeros_like(carry_s)

    lfn = lfn_ref[...]
    parts = [lfn[0:8]]
    for t in range(1, n_t):
        parts.append(parts[-1] + lfn[t * 8:(t + 1) * 8])
    cn = jnp.concatenate(parts, axis=0)

    def update(s, vs, m_s, l_s, a_s):
        m_prev = m_s[...]
        m_new = jnp.maximum(m_prev, jnp.max(s, axis=1, keepdims=True))
        alpha = jnp.exp(m_prev - m_new)
        p = jnp.exp(s - m_new)
        l_s[...] = alpha * l_s[...] + jnp.sum(p, axis=1, keepdims=True)
        pb = p.astype(BF16)
        pv = None
        for g, v in enumerate(vs):
            y = jnp.dot(pb[:, g * PAGE_SIZE:(g + 1) * PAGE_SIZE], v, preferred_element_type=F32)
            pv = y if pv is None else pv + y
        a_s[...] = alpha * a_s[...] + pv
        m_s[...] = m_new

    def scores(qm_s, ks):
        qm = qm_s[...]
        return jnp.concatenate(
            [lax.dot_general(qm, k, (((1,), (1,)), ((), ())), preferred_element_type=F32)
             for k in ks], axis=1) * scale

    group = n_groups - 1 - j
    lane = lax.broadcasted_iota(jnp.int32, (1, n_pg * PAGE_SIZE), 1)
    k_pos = (group * (n_pg * PAGE_SIZE) + lane).astype(F32)
    q_pos = p_len + row_t
    sa = scores(qam_s, [r[...].astype(BF16) for r in ka_refs]) - slope * (q_pos - k_pos)
    update(sa, [r[...].astype(BF16) for r in va_refs], ma_s, la_s, acca_s)

    lane8 = lax.broadcasted_iota(jnp.int32, (8, PAGE_SIZE), 1)
    tail = carry_s[...]
    sufs = [None] * n_pg
    for g in range(n_pg - 1, -1, -1):
        lf = lf_refs[g][...]
        inc = lf
        shift = 1
        while shift < PAGE_SIZE:
            inc = inc + jnp.where(lane8 < PAGE_SIZE - shift,
                                  pltpu.roll(inc, PAGE_SIZE - shift, axis=1), 0.0)
            shift *= 2
        sufs[g] = (inc - lf) + tail
        tail = tail + inc[:, 0:1]
    carry_s[...] = tail
    suf = jnp.concatenate(sufs, axis=1)
    bias_b = jnp.concatenate([suf] * n_t, axis=0) + cn
    sb = scores(qbm_s, [r[...].astype(BF16) for r in kb_refs]) + bias_b
    update(sb, [r[...].astype(BF16) for r in vb_refs], mb_s, lb_s, accb_s)

    @pl.when(j == n_groups - 1)
    def _():
        u = lax.broadcasted_iota(jnp.int32, (1, PAGE_SIZE), 1)
        u_f = u.astype(F32)
        valid = u_f <= row_t
        sa_n = scores(qam_s, [kan_ref[...]]) - slope * (row_t - u_f)
        update(jnp.where(valid, sa_n, NEG_INF), [van_ref[...]], ma_s, la_s, acca_s)
        cn_keys = jnp.zeros((n_r, PAGE_SIZE), F32)
        for t in range(n_t):
            col = jnp.concatenate([cn[t * 8:(t + 1) * 8]] * n_t, axis=0)
            cn_keys = jnp.where(u == t, col, cn_keys)
        sb_n = scores(qbm_s, [kbn_ref[...]]) + (cn - cn_keys)
        update(jnp.where(valid, sb_n, NEG_INF), [vbn_ref[...]], mb_s, lb_s, accb_s)

        lam = _diff_lambda_vec(lq1_ref, lk1_ref, lq2_ref, lk2_ref, lam_init)
        r2 = lax.broadcasted_iota(jnp.int32, (8, width), 0)
        c2 = lax.broadcasted_iota(jnp.int32, (8, width), 1)
        head_a = c2 >> 8
        coef_a = jnp.where(r2 == 2 * head_a, 1.0, jnp.where(r2 == 2 * head_a + 1, -lam, 0.0))
        coef_b = jnp.where(r2 == (c2 >> 7), 1.0, 0.0)
        fa = acca_s[...] / la_s[...]
        fb = accb_s[...] / lb_s[...]
        oa = jnp.concatenate(
            [jnp.sum(fa[t * 8:(t + 1) * 8] * coef_a, axis=0, keepdims=True) for t in range(n_t)],
            axis=0)
        ob = jnp.concatenate(
            [jnp.sum(fb[t * 8:(t + 1) * 8] * coef_b, axis=0, keepdims=True) for t in range(n_t)],
            axis=0)
        w = 2 * HD_A
        g = g_ref[...]
        segs = []
        for h in range(H_A):
            o = oa[:, h * w:(h + 1) * w]
            segs.append(o * lax.rsqrt(jnp.mean(o * o, axis=1, keepdims=True) + EPS)
                        * g * (1.0 - lam_init))
        oa_ref[...] = jnp.concatenate(segs, axis=1)
        ob_ref[...] = ob


def _even_sample_attention(qa, qb, lfn_rows, kan, van, kbn, vbn, pools, page_table,
                           lam_params, g_sub, lam_init):
    pool_ak, pool_av, pool_bk, pool_bv, pool_lf_t = pools
    bsz, n_t, width = qa.shape
    n_pages = page_table.shape[1]
    n_pg = _pick(n_pages, (4, 2, 1))
    n_groups = n_pages // n_pg
    n_r = n_t * 8

    def page_map(g):
        def index(b, j, pt):
            return (pt[b * n_pages + (n_groups - 1 - j) * n_pg + g], 0, 0)
        return index

    per_b3 = lambda b, j, pt: (b, 0, 0)
    const2 = lambda b, j, pt: (0, 0)
    vec = pl.BlockSpec((1, HD_A), const2)
    in_specs = [pl.BlockSpec((None, n_t, width), per_b3),
                pl.BlockSpec((None, n_t, width), per_b3),
                pl.BlockSpec((None, n_r, 1), per_b3)]
    in_specs += [pl.BlockSpec((None, PAGE_SIZE, width), per_b3)] * 4
    in_specs += [vec, vec, vec, vec, pl.BlockSpec((1, 2 * HD_A), const2)]
    args = [qa, qb, lfn_rows, kan, van, kbn, vbn, *lam_params,
            g_sub.reshape(1, 2 * HD_A).astype(F32)]
    for pool in (pool_ak, pool_av, pool_bk, pool_bv):
        for g in range(n_pg):
            in_specs.append(pl.BlockSpec((None, PAGE_SIZE, width), page_map(g)))
            args.append(pool)
    for g in range(n_pg):
        in_specs.append(pl.BlockSpec((None, 8, PAGE_SIZE), page_map(g)))
        args.append(pool_lf_t)
    out_spec = pl.BlockSpec((None, n_t, width), per_b3)
    grid_spec = pltpu.PrefetchScalarGridSpec(
        num_scalar_prefetch=1,
        grid=(bsz, n_groups),
        in_specs=in_specs,
        out_specs=[out_spec, out_spec],
        scratch_shapes=[pltpu.VMEM((n_r, width), BF16), pltpu.VMEM((n_r, width), BF16),
                        pltpu.VMEM((n_r, 1), F32), pltpu.VMEM((n_r, 1), F32),
                        pltpu.VMEM((n_r, width), F32),
                        pltpu.VMEM((n_r, 1), F32), pltpu.VMEM((n_r, 1), F32),
                        pltpu.VMEM((n_r, width), F32),
                        pltpu.VMEM((8, PAGE_SIZE), F32)])
    return pl.pallas_call(
        functools.partial(_even_sample_kernel, n_pg=n_pg, n_t=n_t,
                          p_len=float(n_pages * PAGE_SIZE), scale=HD_A ** -0.5,
                          lam_init=lam_init),
        grid_spec=grid_spec,
        out_shape=[jax.ShapeDtypeStruct((bsz, n_t, width), F32)] * 2,
        compiler_params=_params("parallel", "arbitrary"),
        name="even_sample_attention",
    )(page_table.reshape(-1), *args)


def _dil_kernel(q_ref, kc_ref, kp_ref, vc_ref, vp_ref, slope_ref, o_ref, num_s, m_s, l_s,
                *, group, scale):
    g = pl.program_id(0)
    has_prev = g > 0
    slope = slope_ref[...]
    row = lax.broadcasted_iota(jnp.int32, (N_STEPS, N_STEPS), 0)
    col = lax.broadcasted_iota(jnp.int32, (N_STEPS, N_STEPS), 1)
    steps_cur = (row - col).astype(F32)
    steps_prev = steps_cur + float(N_STEPS)
    dn = (((1,), (1,)), ((), ()))

    for pi, (_, d) in enumerate(C_PATTERNS):
        bias_cur = jnp.where(row >= col, -(slope * float(d)) * steps_cur, NEG_INF)
        bias_prev0 = jnp.where(col >= row, -(slope * float(d)) * steps_prev, NEG_INF)
        n_blk = group // (N_STEPS * d)
        for r in range(d):
            for b in range(n_blk):
                start = r + d * N_STEPS * b
                sl = pl.ds(start, N_STEPS, stride=d) if d > 1 else pl.ds(start, N_STEPS)
                q = q_ref[sl, :].astype(BF16)
                k_c = kc_ref[sl, :].astype(BF16)
                v_c = vc_ref[sl, :].astype(BF16)
                if b > 0:
                    pstart = start - d * N_STEPS
                    psl = pl.ds(pstart, N_STEPS, stride=d) if d > 1 else pl.ds(pstart, N_STEPS)
                    k_p = kc_ref[psl, :].astype(BF16)
                    v_p = vc_ref[psl, :].astype(BF16)
                    bias_prev = bias_prev0
                else:
                    pstart = r + d * N_STEPS * (n_blk - 1)
                    psl = pl.ds(pstart, N_STEPS, stride=d) if d > 1 else pl.ds(pstart, N_STEPS)
                    k_p = kp_ref[psl, :].astype(BF16)
                    v_p = vp_ref[psl, :].astype(BF16)
                    bias_prev = jnp.where(has_prev, bias_prev0, NEG_INF)
                s_c = lax.dot_general(q, k_c, dn, preferred_element_type=F32) * scale + bias_cur
                s_p = lax.dot_general(q, k_p, dn, preferred_element_type=F32) * scale + bias_prev
                m = jnp.maximum(jnp.max(s_c, axis=1, keepdims=True),
                                jnp.max(s_p, axis=1, keepdims=True))
                e_c = jnp.exp(s_c - m)
                e_p = jnp.exp(s_p - m)
                l = jnp.sum(e_c, axis=1, keepdims=True) + jnp.sum(e_p, axis=1, keepdims=True)
                num = (jnp.dot(e_c.astype(BF16), v_c, preferred_element_type=F32)
                       + jnp.dot(e_p.astype(BF16), v_p, preferred_element_type=F32))
                num_s[pi, sl, :] = num
                m_s[pi, sl, :] = jnp.broadcast_to(m, (N_STEPS, LANES))
                l_s[pi, sl, :] = jnp.broadcast_to(l, (N_STEPS, LANES))

    def combine(c, carry):
        rows = pl.ds(pl.multiple_of(c * N_STEPS, N_STEPS), N_STEPS)
        ms = [m_s[pi, rows, :] for pi in range(len(C_PATTERNS))]
        m_all = functools.reduce(jnp.maximum, ms)
        num = jnp.zeros((N_STEPS, HD_C), F32)
        den = jnp.zeros((N_STEPS, LANES), F32)
        for pi in range(len(C_PATTERNS)):
            wt = jnp.exp(ms[pi] - m_all)
            num = num + num_s[pi, rows, :] * wt
            den = den + l_s[pi, rows, :] * wt
        o_ref[rows, :] = (num / den).astype(o_ref.dtype)
        return carry

    lax.fori_loop(0, group // N_STEPS, combine, 0)


def _dilated_attention(q, k, v):
    s_len = q.shape[0]
    group = N_STEPS * C_PATTERNS[-1][1]
    assert s_len % group == 0
    n_g = s_len // group
    slopes = jnp.broadcast_to(_alibi_slopes(H_C)[:, None, None], (H_C, 1, LANES))
    cur = pl.BlockSpec((group, HD_C), lambda g, h: (g, h))
    prev = pl.BlockSpec((group, HD_C), lambda g, h: (jnp.maximum(g - 1, 0), h))
    n_pat = len(C_PATTERNS)
    return pl.pallas_call(
        functools.partial(_dil_kernel, group=group, scale=HD_C ** -0.5),
        grid=(n_g, H_C),
        in_specs=[cur, cur, prev, cur, prev,
                  pl.BlockSpec((None, 1, LANES), lambda g, h: (h, 0, 0))],
        out_specs=pl.BlockSpec((group, HD_C), lambda g, h: (g, h)),
        out_shape=jax.ShapeDtypeStruct((s_len, H_C * HD_C), BF16),
        scratch_shapes=[pltpu.VMEM((n_pat, group, HD_C), F32),
                        pltpu.VMEM((n_pat, group, LANES), F32),
                        pltpu.VMEM((n_pat, group, LANES), F32)],
        compiler_params=_params("parallel", "parallel"),
        name="dilated_attention",
    )(q, k, k, v, v, slopes)


def _dil_sample_kernel(q_ref, kc_ref, vc_ref, kn_ref, vn_ref, slope_ref, o_ref,
                       *, n_t, n_q, buf_len, heads, scale):
    row = lax.broadcasted_iota(jnp.int32, (n_q, N_STEPS), 0)
    lane = lax.broadcasted_iota(jnp.int32, (n_q, N_STEPS), 1)
    row_col = lax.broadcasted_iota(jnp.int32, (n_q, 1), 0)
    dn = (((1,), (1,)), ((), ()))
    for hh in range(heads):
        cols = slice(hh * HD_C, (hh + 1) * HD_C)
        slope = slope_ref[hh]
        q = q_ref[:, cols]
        qb = q.astype(BF16)
        kn = kn_ref[:, cols]
        vn = vn_ref[:, cols]
        stats = []
        for _, d in C_PATTERNS:
            if d == 1:
                sl = pl.ds(buf_len - N_STEPS, N_STEPS)
                s = lax.dot_general(qb, kc_ref[sl, cols].astype(BF16), dn,
                                    preferred_element_type=F32)
                steps = (N_STEPS + row - lane).astype(F32)
                s = jnp.where(lane >= row, s * scale - slope * steps, NEG_INF)
            else:
                s = jnp.zeros((n_q, N_STEPS), F32)
                for t in range(n_t):
                    sl = pl.ds(buf_len + t - N_STEPS * d, N_STEPS, stride=d)
                    st = lax.dot_general(qb, kc_ref[sl, cols].astype(BF16), dn,
                                         preferred_element_type=F32)
                    s = jnp.where(row == t, st, s)
                steps = (N_STEPS - lane).astype(F32)
                s = s * scale - (slope * float(d)) * steps
            s_new = []
            for u in range(n_t):
                su = jnp.sum(q * kn[u:u + 1, :], axis=1, keepdims=True) * scale
                delta = row_col - u
                ok = (delta >= 0) & ((delta & (d - 1)) == 0) & (delta <= N_STEPS * d)
                s_new.append(jnp.where(ok, su - slope[:, 0:1] * delta.astype(F32), NEG_INF))
            m = jnp.max(s, axis=1, keepdims=True)
            for su in s_new:
                m = jnp.maximum(m, su)
            e = jnp.exp(s - m)
            l = jnp.sum(e, axis=1, keepdims=True)
            eb = e.astype(BF16)
            if d == 1:
                num = jnp.dot(eb, vc_ref[pl.ds(buf_len - N_STEPS, N_STEPS), cols].astype(BF16),
                              preferred_element_type=F32)
            else:
                num = jnp.zeros((n_q, HD_C), F32)
                for t in range(n_t):
                    sl = pl.ds(buf_len + t - N_STEPS * d, N_STEPS, stride=d)
                    nt = jnp.dot(eb, vc_ref[sl, cols].astype(BF16), preferred_element_type=F32)
                    num = jnp.where(row == t, nt, num)
            for u, su in enumerate(s_new):
                eu = jnp.exp(su - m)
                l = l + eu
                num = num + eu * vn[u:u + 1, :]
            stats.append((m, l, num))
        m_all = functools.reduce(jnp.maximum, [st[0] for st in stats])
        num = jnp.zeros((n_q, HD_C), F32)
        den = jnp.zeros((n_q, 1), F32)
        for m, l, nm in stats:
            wt = jnp.exp(m - m_all)
            num = num + nm * wt
            den = den + l * wt
        o_ref[:, cols] = num / den


def _dilated_sample_attention(q_pad, cache_k, cache_v, k_new, v_new, n_t):
    bsz, n_q, width = q_pad.shape
    buf_len = cache_k.shape[1]
    assert buf_len >= N_STEPS * C_PATTERNS[-1][1]
    heads = 1
    w = heads * HD_C
    slopes = jnp.broadcast_to(_alibi_slopes(H_C)[:, None, None], (H_C, 1, LANES))
    return pl.pallas_call(
        functools.partial(_dil_sample_kernel, n_t=n_t, n_q=n_q, buf_len=buf_len, heads=heads,
                          scale=HD_C ** -0.5),
        grid=(bsz, H_C // heads),
        in_specs=[pl.BlockSpec((None, n_q, w), lambda b, c: (b, 0, c)),
                  pl.BlockSpec((None, buf_len, w), lambda b, c: (b, 0, c)),
                  pl.BlockSpec((None, buf_len, w), lambda b, c: (b, 0, c)),
                  pl.BlockSpec((None, n_t, w), lambda b, c: (b, 0, c)),
                  pl.BlockSpec((None, n_t, w), lambda b, c: (b, 0, c)),
                  pl.BlockSpec((heads, 1, LANES), lambda b, c: (c, 0, 0))],
        out_specs=pl.BlockSpec((None, n_q, w), lambda b, c: (b, 0, c)),
        out_shape=jax.ShapeDtypeStruct((bsz, n_q, width), F32),
        compiler_params=_params("parallel", "parallel"),
        name="dilated_sample_attention",
    )(q_pad, cache_k, cache_v, k_new, v_new, slopes)


def _shift_kernel(ck_ref, cv_ref, kn_ref, vn_ref, ok_ref, ov_ref, sem, *, bsz, keep, new):
    copies = []
    for b in range(bsz):
        for s, (c_ref, n_ref, o_ref) in enumerate(((ck_ref, kn_ref, ok_ref),
                                                   (cv_ref, vn_ref, ov_ref))):
            copies.append(pltpu.make_async_copy(
                c_ref.at[b, pl.ds(new, keep), :], o_ref.at[b, pl.ds(0, keep), :], sem.at[2 * s]))
            copies.append(pltpu.make_async_copy(
                n_ref.at[b], o_ref.at[b, pl.ds(keep, new), :], sem.at[2 * s + 1]))
    for c in copies:
        c.start()
    for c in copies:
        c.wait()


def _shift_window(cache_k, cache_v, k_new, v_new):
    bsz, buf_len, n_h, hd = cache_k.shape
    n_t = k_new.shape[1]
    assert n_t < buf_len and (n_t * n_h) % 8 == 0
    rows = buf_len * n_h
    new = n_t * n_h
    any_spec = pl.BlockSpec(memory_space=pl.ANY)
    ok, ov = pl.pallas_call(
        functools.partial(_shift_kernel, bsz=bsz, keep=rows - new, new=new),
        in_specs=[any_spec] * 4,
        out_specs=[any_spec] * 2,
        out_shape=[jax.ShapeDtypeStruct((bsz, rows, hd), F32)] * 2,
        scratch_shapes=[pltpu.SemaphoreType.DMA((4,))],
        name="window_shift",
    )(cache_k.reshape(bsz, rows, hd), cache_v.reshape(bsz, rows, hd),
      k_new.reshape(bsz, new, hd), v_new.reshape(bsz, new, hd))
    return ok.reshape(cache_k.shape), ov.reshape(cache_v.shape)


def _cross_kernel(q_ref, k_ref, v_ref, o_ref, *, scale):
    dn = (((1,), (1,)), ((), ()))
    for h in range(H_X):
        cols = slice(h * HD_X, (h + 1) * HD_X)
        s = lax.dot_general(q_ref[:, cols], k_ref[:, cols].astype(BF16), dn,
                            preferred_element_type=F32) * scale
        m = jnp.max(s, axis=1, keepdims=True)
        e = jnp.exp(s - m)
        l = jnp.sum(e, axis=1, keepdims=True)
        o = jnp.dot(e.astype(BF16), v_ref[:, cols].astype(BF16), preferred_element_type=F32)
        o_ref[:, cols] = (o / l).astype(o_ref.dtype)


def _cross_attention(q, mem_k, mem_v):
    bsz, n_q, w = q.shape
    n_m = mem_k.shape[1]
    tq = _pick(n_q, (512, 256, 128))
    return pl.pallas_call(
        functools.partial(_cross_kernel, scale=HD_X ** -0.5),
        grid=(bsz, n_q // tq),
        in_specs=[pl.BlockSpec((None, tq, w), lambda b, i: (b, i, 0)),
                  pl.BlockSpec((None, n_m, w), lambda b, i: (b, 0, 0)),
                  pl.BlockSpec((None, n_m, w), lambda b, i: (b, 0, 0))],
        out_specs=pl.BlockSpec((None, tq, w), lambda b, i: (b, i, 0)),
        out_shape=jax.ShapeDtypeStruct((bsz, n_q, w), BF16),
        compiler_params=_params("parallel", "parallel"),
        name="cross_attention",
    )(q, mem_k, mem_v)


SAMPLE_ROWS = 16


def _pad_rows(x, rows):
    return jnp.pad(x, ((0, 0), (0, rows - x.shape[1]), (0, 0)))


def kernel(x_prompt, x_sample, mem_prompt, cache_a_k, cache_a_v, cache_b_k, cache_b_v, cache_b_logf, cache_c_k, cache_c_v, cache_mem_k, cache_mem_v, page_table, norm_mix, norm_cross, norm_mem, norm_ffn, norm_final, w_in_even, b_forget, lambda_q1, lambda_k1, lambda_q2, lambda_k2, subln_a, w_out_even, w_in_odd, w_out_odd, w_xq, w_xkv, w_xo, w_gate_up, w_down):
    n_b, s_len, d_model = x_prompt.shape
    d_b, d_t, _ = x_sample.shape
    assert n_b == 1
    depth = norm_mix.shape[0]
    n_mem = mem_prompt.shape[1]
    xp = x_prompt.reshape(s_len, d_model)
    xs = x_sample.reshape(d_b * d_t, d_model)
    mem = mem_prompt.reshape(n_mem, d_model)
    wa = H_A * 2 * HD_A
    wb = H_B * HD_B
    wc = H_C * HD_C
    wx = H_X * HD_X

    ak_p, av_p, bk_p, bv_p, bl_p, ck_p, cv_p, mk_p, mv_p = [], [], [], [], [], [], [], [], []
    ak_s, av_s, bk_s, bv_s, bl_s, ck_s, cv_s = [], [], [], [], [], [], []

    for l in range(depth):
        j = l // 2
        xnp = _rmsnorm(xp, norm_mix[l], BF16)
        xns = _rmsnorm(xs, norm_mix[l], BF16)
        if l % 2 == 0:
            lam_init = 0.8 - 0.6 * math.exp(-0.3 * l)
            w_in = w_in_even[j]
            w_cols = [w_in[:, o:o + wa].astype(BF16) for o in range(0, 6 * wa, wa)]
            w_fz = jnp.pad(w_in[:, 6 * wa:], ((0, 0), (0, LANES - H_B))).astype(BF16)
            lam_params = [p[j].reshape(1, HD_A).astype(F32)
                          for p in (lambda_q1, lambda_k1, lambda_q2, lambda_k2)]
            w_out = w_out_even[j].astype(BF16)

            qa = _matmul([(xnp, w_cols[0])], out_f32=False, out_bf16=True)
            ka, ka16 = _matmul([(xnp, w_cols[1])], out_bf16=True)
            va, va16 = _matmul([(xnp, w_cols[2])], out_bf16=True)
            qb = _matmul([(xnp, w_cols[3])], out_f32=False, out_bf16=True)
            kb, kb16 = _matmul([(xnp, w_cols[4])], out_bf16=True)
            vb, vb16 = _matmul([(xnp, w_cols[5])], out_bf16=True)
            fz = _matmul([(xnp, w_fz)])[:, :H_B]
            lf_t, c_t = _logf_cumsum(fz.T, b_forget[j])
            za = _diff_attention(qa, ka16, va16, lam_params, subln_a[j], lam_init)
            zb = _fox_attention(qb, kb16, vb16, c_t)
            xp = _matmul([(za, w_out[:wa]), (zb, w_out[wa:])], res=xp)
            ak_p.append(ka.reshape(1, s_len, H_A, 2, HD_A))
            av_p.append(va.reshape(1, s_len, H_A, 2 * HD_A))
            bk_p.append(kb.reshape(1, s_len, H_B, HD_B))
            bv_p.append(vb.reshape(1, s_len, H_B, HD_B))
            bl_p.append(lf_t.T.reshape(1, s_len, H_B))

            qa_s = _matmul([(xns, w_cols[0])])
            ka_s, ka_s16 = _matmul([(xns, w_cols[1])], out_bf16=True)
            va_s, va_s16 = _matmul([(xns, w_cols[2])], out_bf16=True)
            qb_s = _matmul([(xns, w_cols[3])])
            kb_s, kb_s16 = _matmul([(xns, w_cols[4])], out_bf16=True)
            vb_s, vb_s16 = _matmul([(xns, w_cols[5])], out_bf16=True)
            fz_s = _matmul([(xns, w_fz)])[:, :H_B]
            lf_s_t, _ = _logf_cumsum(fz_s.T, b_forget[j])
            lf_s = lf_s_t.T
            n_pool = cache_a_k.shape[1]
            pools = (cache_a_k[j].reshape(n_pool, PAGE_SIZE, wa),
                     cache_a_v[j].reshape(n_pool, PAGE_SIZE, wa),
                     cache_b_k[j].reshape(n_pool, PAGE_SIZE, wb),
                     cache_b_v[j].reshape(n_pool, PAGE_SIZE, wb),
                     jnp.swapaxes(cache_b_logf[j], 1, 2))
            new_kv = [_pad_rows(a.reshape(d_b, d_t, wa), PAGE_SIZE)
                      for a in (ka_s16, va_s16, kb_s16, vb_s16)]
            oa_s, ob_s = _even_sample_attention(
                qa_s.reshape(d_b, d_t, wa), qb_s.reshape(d_b, d_t, wb),
                lf_s.reshape(d_b, d_t * H_B, 1), *new_kv, pools, page_table,
                lam_params, subln_a[j], lam_init)
            xs = _matmul([(oa_s.reshape(d_b * d_t, wa).astype(BF16), w_out[:wa]),
                          (ob_s.reshape(d_b * d_t, wb).astype(BF16), w_out[wa:])], res=xs)
            ak_s.append(ka_s.reshape(d_b, d_t, H_A, 2, HD_A))
            av_s.append(va_s.reshape(d_b, d_t, H_A, 2 * HD_A))
            bk_s.append(kb_s.reshape(d_b, d_t, H_B, HD_B))
            bv_s.append(vb_s.reshape(d_b, d_t, H_B, HD_B))
            bl_s.append(lf_s.reshape(d_b, d_t, H_B))
        else:
            w_in = w_in_odd[j]
            w_cols = [w_in[:, o:o + wc].astype(BF16) for o in range(0, 3 * wc, wc)]
            w_out = w_out_odd[j].astype(BF16)

            q = _matmul([(xnp, w_cols[0])])
            k = _matmul([(xnp, w_cols[1])])
            v = _matmul([(xnp, w_cols[2])])
            o = _dilated_attention(q, k, v)
            xp = _matmul([(o, w_out)], res=xp)
            keep = min(C_WMAX, s_len)
            ck_p.append(k[s_len - keep:].reshape(1, keep, H_C, HD_C))
            cv_p.append(v[s_len - keep:].reshape(1, keep, H_C, HD_C))

            q_s = _matmul([(xns, w_cols[0])]).reshape(d_b, d_t, wc)
            k_s = _matmul([(xns, w_cols[1])]).reshape(d_b, d_t, wc)
            v_s = _matmul([(xns, w_cols[2])]).reshape(d_b, d_t, wc)
            buf_len = cache_c_k.shape[2]
            o_s = _dilated_sample_attention(
                _pad_rows(q_s, SAMPLE_ROWS), cache_c_k[j].reshape(d_b, buf_len, wc),
                cache_c_v[j].reshape(d_b, buf_len, wc), k_s, v_s, d_t)
            xs = _matmul([(o_s[:, :d_t].reshape(d_b * d_t, wc).astype(BF16), w_out)], res=xs)
            new_k, new_v = _shift_window(cache_c_k[j], cache_c_v[j], k_s, v_s)
            ck_s.append(new_k)
            cv_s.append(new_v)

        w_q = w_xq[l].astype(BF16)
        w_kv = w_xkv[l].astype(BF16)
        w_o = w_xo[l].astype(BF16)
        memn = _rmsnorm(mem, norm_mem[l], BF16)
        mk = _matmul([(memn, w_kv[:, :wx])])
        mv = _matmul([(memn, w_kv[:, wx:])])
        mk_p.append(mk.reshape(1, n_mem, H_X, HD_X))
        mv_p.append(mv.reshape(1, n_mem, H_X, HD_X))
        qx = _matmul([(_rmsnorm(xp, norm_cross[l], BF16), w_q)], out_f32=False, out_bf16=True)
        ox = _cross_attention(qx.reshape(1, s_len, wx), mk.reshape(1, n_mem, wx),
                              mv.reshape(1, n_mem, wx))
        xp = _matmul([(ox.reshape(s_len, wx), w_o)], res=xp)
        qx_s = _matmul([(_rmsnorm(xs, norm_cross[l], BF16), w_q)], out_f32=False, out_bf16=True)
        ox_s = _cross_attention(_pad_rows(qx_s.reshape(d_b, d_t, wx), SAMPLE_ROWS),
                                cache_mem_k[l].reshape(d_b, n_mem, wx),
                                cache_mem_v[l].reshape(d_b, n_mem, wx))
        xs = _matmul([(ox_s[:, :d_t].reshape(d_b * d_t, wx), w_o)], res=xs)

        d_ff = w_down.shape[1]
        w_g = w_gate_up[l][:, :d_ff].astype(BF16)
        w_u = w_gate_up[l][:, d_ff:].astype(BF16)
        w_d = w_down[l].astype(BF16)
        xp = _ffn(_rmsnorm(xp, norm_ffn[l], BF16), w_g, w_u, w_d, xp)
        xs = _ffn(_rmsnorm(xs, norm_ffn[l], BF16), w_g, w_u, w_d, xs)

    y_prompt = _rmsnorm(xp, norm_final, F32).reshape(1, s_len, d_model)
    y_sample = _rmsnorm(xs, norm_final, F32).reshape(d_b, d_t, d_model)
    st = jnp.stack
    return (y_prompt, y_sample,
            st(ak_p), st(av_p), st(bk_p), st(bv_p), st(bl_p), st(ck_p), st(cv_p), st(mk_p), st(mv_p),
            st(ak_s), st(av_s), st(bk_s), st(bv_s), st(bl_s), st(ck_s), st(cv_s))
```

```python
import functools
import math

import numpy as np
import jax
import jax.numpy as jnp
from jax import lax
from jax.experimental import pallas as pl
from jax.experimental.pallas import tpu as pltpu

F32 = jnp.float32
BF16 = jnp.bfloat16

EPS = 1e-6
PAGE_SIZE = 128
H_A, HD_A = 4, 128
H_B, HD_B = 8, 128
H_C, HD_C = 16, 128
H_X, HD_X = 4, 128
C_PATTERNS = ((128, 1), (512, 4), (2048, 16))
C_WMAX = 2048
LANES = 128
MXU_TILE = 256
VMEM_LIMIT = 52 * 1024 * 1024
NEG_INF = float("-inf")


def _params(*sem):
    return pltpu.CompilerParams(dimension_semantics=sem, vmem_limit_bytes=VMEM_LIMIT)


def _tile_lanes(x, reps):
    return x if reps == 1 else jnp.concatenate([x] * reps, axis=1)


def _pick(n, pref):
    for t in pref:
        if n % t == 0:
            return t
    return n


def _rms_kernel(x_ref, g_ref, o_ref):
    x = x_ref[...]
    ms = jnp.mean(x * x, axis=-1, keepdims=True)
    o_ref[...] = ((x * lax.rsqrt(ms + EPS)) * g_ref[...]).astype(o_ref.dtype)


def _rmsnorm(x, g, out_dtype):
    m, d = x.shape
    tm = _pick(m, (512, 256, 128))
    return pl.pallas_call(
        _rms_kernel,
        grid=(m // tm,),
        in_specs=[pl.BlockSpec((tm, d), lambda i: (i, 0)),
                  pl.BlockSpec((1, d), lambda i: (0, 0))],
        out_specs=pl.BlockSpec((tm, d), lambda i: (i, 0)),
        out_shape=jax.ShapeDtypeStruct((m, d), out_dtype),
        compiler_params=_params("parallel"),
        name="rmsnorm",
    )(x, g.reshape(1, d).astype(F32))


def _mm_kernel(*refs, n_pairs, has_res, out_f32, out_bf16):
    acc = None
    for p in range(n_pairs):
        y = jnp.dot(refs[2 * p][...], refs[2 * p + 1][...], preferred_element_type=F32)
        acc = y if acc is None else acc + y
    pos = 2 * n_pairs
    if has_res:
        acc = refs[pos][...] + acc
        pos += 1
    if out_f32:
        refs[pos][...] = acc
        pos += 1
    if out_bf16:
        refs[pos][...] = acc.astype(BF16)


def _matmul(pairs, res=None, out_f32=True, out_bf16=False):
    m = pairs[0][0].shape[0]
    n = pairs[0][1].shape[1]
    tm = _pick(m, (512, 256, 128))
    tn = _pick(n, (1024, 512, 256, 128))
    in_specs, args = [], []
    for a, w in pairs:
        k = a.shape[1]
        in_specs += [pl.BlockSpec((tm, k), lambda i, j: (i, 0)),
                     pl.BlockSpec((k, tn), lambda i, j: (0, j))]
        args += [a, w]
    if res is not None:
        in_specs.append(pl.BlockSpec((tm, tn), lambda i, j: (i, j)))
        args.append(res)
    out_specs, out_shape = [], []
    if out_f32:
        out_specs.append(pl.BlockSpec((tm, tn), lambda i, j: (i, j)))
        out_shape.append(jax.ShapeDtypeStruct((m, n), F32))
    if out_bf16:
        out_specs.append(pl.BlockSpec((tm, tn), lambda i, j: (i, j)))
        out_shape.append(jax.ShapeDtypeStruct((m, n), BF16))
    outs = pl.pallas_call(
        functools.partial(_mm_kernel, n_pairs=len(pairs), has_res=res is not None,
                          out_f32=out_f32, out_bf16=out_bf16),
        grid=(m // tm, n // tn),
        in_specs=in_specs, out_specs=out_specs, out_shape=out_shape,
        compiler_params=_params("parallel", "parallel"),
        name="matmul",
    )(*args)
    return outs[0] if len(outs) == 1 else tuple(outs)


def _ffn_kernel(xn_ref, wg_ref, wu_ref, wd_ref, res_ref, o_ref, acc_ref):
    f = pl.program_id(1)

    @pl.when(f == 0)
    def _():
        acc_ref[...] = jnp.zeros_like(acc_ref)

    xn = xn_ref[...]
    g = jnp.dot(xn, wg_ref[...], preferred_element_type=F32)
    u = jnp.dot(xn, wu_ref[...], preferred_element_type=F32)
    h = (g * jax.nn.sigmoid(g)) * u
    acc_ref[...] += jnp.dot(h.astype(BF16), wd_ref[...], preferred_element_type=F32)

    @pl.when(f == pl.num_programs(1) - 1)
    def _():
        o_ref[...] = res_ref[...] + acc_ref[...]


def _ffn(xn, wg, wu, wd, res):
    m, d = xn.shape
    dff = wg.shape[1]
    tm = _pick(m, (512, 256, 128))
    tf = _pick(dff, (512, 256, 128))
    return pl.pallas_call(
        _ffn_kernel,
        grid=(m // tm, dff // tf),
        in_specs=[pl.BlockSpec((tm, d), lambda i, f: (i, 0)),
                  pl.BlockSpec((d, tf), lambda i, f: (0, f)),
                  pl.BlockSpec((d, tf), lambda i, f: (0, f)),
                  pl.BlockSpec((tf, d), lambda i, f: (f, 0)),
                  pl.BlockSpec((tm, d), lambda i, f: (i, 0))],
        out_specs=pl.BlockSpec((tm, d), lambda i, f: (i, 0)),
        out_shape=jax.ShapeDtypeStruct((m, d), F32),
        scratch_shapes=[pltpu.VMEM((tm, d), F32)],
        compiler_params=_params("parallel", "arbitrary"),
        name="swiglu",
    )(xn, wg, wu, wd, res)


def _log_sigmoid(x):
    return jnp.minimum(x, 0.0) - jnp.log1p(jnp.exp(-jnp.abs(x)))


def _logf_kernel(fz_ref, b_ref, lf_ref, c_ref):
    lf = _log_sigmoid(fz_ref[...] + b_ref[...])
    lf_ref[...] = lf
    n = lf.shape[1]
    lane = lax.broadcasted_iota(jnp.int32, lf.shape, 1)
    c = lf
    shift = 1
    while shift < n:
        c = c + jnp.where(lane >= shift, pltpu.roll(c, shift, axis=1), 0.0)
        shift *= 2
    c_ref[...] = c


def _logf_cumsum(fz_t, b_f):
    h, n = fz_t.shape
    return pl.pallas_call(
        _logf_kernel,
        out_shape=[jax.ShapeDtypeStruct((h, n), F32)] * 2,
        name="logf_cumsum",
    )(fz_t, b_f.reshape(h, 1).astype(F32))


def _fox_kernel(q_ref, k_ref, v_ref, ccol_ref, crow_ref, o_ref, m_s, l_s, acc_s, *, t, scale):
    i = pl.program_id(1)
    q = q_ref[...]
    cq = jnp.broadcast_to(ccol_ref[...], (t, LANES))
    m_s[...] = jnp.full_like(m_s, NEG_INF)
    l_s[...] = jnp.zeros_like(l_s)
    acc_s[...] = jnp.zeros_like(acc_s)
    reps = t // LANES

    def chunk(j, masked):
        start = pl.multiple_of(j * t, t)
        k = k_ref[pl.ds(start, t), :]
        v = v_ref[pl.ds(start, t), :]
        s = lax.dot_general(q, k, (((1,), (1,)), ((), ())), preferred_element_type=F32) * scale
        s = s + (_tile_lanes(cq, reps) - crow_ref[j])
        if masked:
            row = lax.broadcasted_iota(jnp.int32, (t, t), 0)
            col = lax.broadcasted_iota(jnp.int32, (t, t), 1)
            s = jnp.where(row >= col, s, NEG_INF)
        m_prev = m_s[...]
        m_new = jnp.maximum(m_prev, jnp.max(s, axis=1, keepdims=True))
        alpha = jnp.exp(m_prev - m_new)
        p = jnp.exp(s - _tile_lanes(m_new, reps))
        l_s[...] = alpha * l_s[...] + jnp.sum(p, axis=1, keepdims=True)
        acc_s[...] = alpha * acc_s[...] + jnp.dot(p.astype(BF16), v, preferred_element_type=F32)
        m_s[...] = m_new

    def body(j, c):
        chunk(j, False)
        return c

    lax.fori_loop(0, i, body, 0)
    chunk(i, True)
    o_ref[...] = (acc_s[...] / l_s[...]).astype(o_ref.dtype)


def _fox_attention(qb, kb, vb, c_t):
    s_len = qb.shape[0]
    t = _pick(s_len, (512, 256, 128))
    nc = s_len // t
    c_col = c_t.reshape(H_B, s_len, 1)
    c_row = c_t.reshape(H_B, nc, 1, t)
    return pl.pallas_call(
        functools.partial(_fox_kernel, t=t, scale=HD_B ** -0.5),
        grid=(H_B, nc),
        in_specs=[pl.BlockSpec((t, HD_B), lambda h, i: (i, h)),
                  pl.BlockSpec((s_len, HD_B), lambda h, i: (0, h)),
                  pl.BlockSpec((s_len, HD_B), lambda h, i: (0, h)),
                  pl.BlockSpec((None, t, 1), lambda h, i: (h, i, 0)),
                  pl.BlockSpec((None, nc, 1, t), lambda h, i: (h, 0, 0, 0))],
        out_specs=pl.BlockSpec((t, HD_B), lambda h, i: (i, h)),
        out_shape=jax.ShapeDtypeStruct((s_len, H_B * HD_B), BF16),
        scratch_shapes=[pltpu.VMEM((t, LANES), F32), pltpu.VMEM((t, LANES), F32),
                        pltpu.VMEM((t, HD_B), F32)],
        compiler_params=_params("parallel", "parallel"),
        name="forget_attention",
    )(qb, kb, vb, c_col, c_row)


def _diff_lambda_vec(lq1_ref, lk1_ref, lq2_ref, lk2_ref, lam_init):
    a = jnp.sum(lq1_ref[...] * lk1_ref[...], axis=1, keepdims=True)
    b = jnp.sum(lq2_ref[...] * lk2_ref[...], axis=1, keepdims=True)
    return jnp.exp(a) - jnp.exp(b) + lam_init


def _diff_kernel(q_ref, k_ref, v_ref, slope_ref, lq1_ref, lk1_ref, lq2_ref, lk2_ref, g_ref,
                 o_ref, m1_s, l1_s, a1_s, m2_s, l2_s, a2_s, *, t, scale, lam_init):
    i = pl.program_id(1)
    q1 = q_ref[:, :HD_A]
    q2 = q_ref[:, HD_A:]
    slope = slope_ref[...]
    row_pos = (i * t + lax.broadcasted_iota(jnp.int32, (t, LANES), 0)).astype(F32)
    row_term = slope * row_pos
    for m_s, l_s, a_s in ((m1_s, l1_s, a1_s), (m2_s, l2_s, a2_s)):
        m_s[...] = jnp.full_like(m_s, NEG_INF)
        l_s[...] = jnp.zeros_like(l_s)
        a_s[...] = jnp.zeros_like(a_s)
    reps = t // LANES

    def chunk(j, masked):
        start = pl.multiple_of(j * t, t)
        k = k_ref[pl.ds(start, t), :]
        v = v_ref[pl.ds(start, t), :]
        col_pos = (j * t + lax.broadcasted_iota(jnp.int32, (1, t), 1)).astype(F32)
        bias = _tile_lanes(slope, reps) * col_pos - _tile_lanes(row_term, reps)
        if masked:
            row = lax.broadcasted_iota(jnp.int32, (t, t), 0)
            col = lax.broadcasted_iota(jnp.int32, (t, t), 1)
            bias = jnp.where(row >= col, bias, NEG_INF)
        for qm, km, m_s, l_s, a_s in ((q1, k[:, :HD_A], m1_s, l1_s, a1_s),
                                      (q2, k[:, HD_A:], m2_s, l2_s, a2_s)):
            s = lax.dot_general(qm, km, (((1,), (1,)), ((), ())),
                                preferred_element_type=F32) * scale + bias
            m_prev = m_s[...]
            m_new = jnp.maximum(m_prev, jnp.max(s, axis=1, keepdims=True))
            alpha = jnp.exp(m_prev - m_new)
            p = jnp.exp(s - _tile_lanes(m_new, reps))
            l_s[...] = alpha * l_s[...] + jnp.sum(p, axis=1, keepdims=True)
            a_s[...] = (_tile_lanes(alpha, 2) * a_s[...]
                        + jnp.dot(p.astype(BF16), v, preferred_element_type=F32))
            m_s[...] = m_new

    def body(j, c):
        chunk(j, False)
        return c

    lax.fori_loop(0, i, body, 0)
    chunk(i, True)
    lam = _diff_lambda_vec(lq1_ref, lk1_ref, lq2_ref, lk2_ref, lam_init)
    o = (a1_s[...] / _tile_lanes(l1_s[...], 2)
         - lam * (a2_s[...] / _tile_lanes(l2_s[...], 2)))
    y = o * lax.rsqrt(jnp.mean(o * o, axis=1, keepdims=True) + EPS) * g_ref[...] * (1.0 - lam_init)
    o_ref[...] = y.astype(o_ref.dtype)


def _alibi_slopes(n):
    return jnp.asarray(2.0 ** (-8.0 * np.arange(1, n + 1) / n), dtype=F32)


def _diff_attention(qa, ka, va, lam_params, g_sub, lam_init):
    s_len = qa.shape[0]
    t = _pick(s_len, (512, 256, 128))
    nc = s_len // t
    w = 2 * HD_A
    slopes = jnp.broadcast_to(_alibi_slopes(H_A)[:, None, None], (H_A, 1, LANES))
    vec = pl.BlockSpec((1, HD_A), lambda h, i: (0, 0))
    return pl.pallas_call(
        functools.partial(_diff_kernel, t=t, scale=HD_A ** -0.5, lam_init=lam_init),
        grid=(H_A, nc),
        in_specs=[pl.BlockSpec((t, w), lambda h, i: (i, h)),
                  pl.BlockSpec((s_len, w), lambda h, i: (0, h)),
                  pl.BlockSpec((s_len, w), lambda h, i: (0, h)),
                  pl.BlockSpec((None, 1, LANES), lambda h, i: (h, 0, 0)),
                  vec, vec, vec, vec,
                  pl.BlockSpec((1, w), lambda h, i: (0, 0))],
        out_specs=pl.BlockSpec((t, w), lambda h, i: (i, h)),
        out_shape=jax.ShapeDtypeStruct((s_len, H_A * w), BF16),
        scratch_shapes=[pltpu.VMEM((t, LANES), F32), pltpu.VMEM((t, LANES), F32),
                        pltpu.VMEM((t, w), F32),
                        pltpu.VMEM((t, LANES), F32), pltpu.VMEM((t, LANES), F32),
                        pltpu.VMEM((t, w), F32)],
        compiler_params=_params("parallel", "parallel"),
        name="diff_attention",
    )(qa, ka, va, slopes, *lam_params, g_sub.reshape(1, w).astype(F32))


PAGE_ROWS = PAGE_SIZE * 8


def _even_sample_tables(n_t, n_new_cols):
    slopes = 2.0 ** (-8.0 * np.arange(1, H_A + 1) / H_A)
    ninf = -np.inf
    col = np.arange(PAGE_ROWS)
    p_col, j_col = col >> 3, col & 7
    ra = np.arange(2 * n_t * H_A)
    m_r, t_r, h_r = ra // (n_t * H_A), (ra // H_A) % n_t, ra % H_A
    j_r = 2 * h_r + m_r
    sl_r = slopes[h_r]
    ta = np.where(j_col[None, :] == j_r[:, None], sl_r[:, None] * p_col[None, :], ninf)
    rb = np.arange(n_t * H_B)
    tb_t, tb_h = rb // H_B, rb % H_B
    tb = np.where(j_col[None, :] == tb_h[:, None], 0.0, ninf)
    ncol = np.arange(n_new_cols)
    u_col, jn_col = ncol >> 3, ncol & 7
    ok_a = (jn_col[None, :] == j_r[:, None]) & (u_col[None, :] <= t_r[:, None])
    tan = np.where(ok_a, -sl_r[:, None] * (t_r[:, None] - u_col[None, :]), ninf)
    ok_b = (jn_col[None, :] == tb_h[:, None]) & (u_col[None, :] <= tb_t[:, None])
    tbn = np.where(ok_b, 0.0, ninf)
    rowa = np.stack([sl_r, t_r.astype(np.float64)], axis=1)
    lane = np.arange(MXU_TILE) & 7
    me = np.stack([(lane[None, :] == (4 * e + h_r)[:, None]).astype(np.float64) for e in range(2)])
    gi = np.arange(MXU_TILE) >> 3
    gt = (gi[:, None] == gi[None, :]).astype(np.float64)
    f = lambda a: jnp.asarray(a, F32)
    return f(ta), f(tb), f(tan), f(tbn), f(rowa), f(me), jnp.asarray(gt, BF16)


def _even_sample_kernel(pt_ref, qa_ref, qb_ref, lfr_ref, lfl_ref, kan_ref, van_ref, kbn_ref, vbn_ref,
                        ta_ref, tb_ref, tan_ref, tbn_ref, rowa_ref, me_ref, gt_ref,
                        lq1_ref, lk1_ref, lq2_ref, lk2_ref, gsub_ref, *rest,
                        n_pg, n_t, p_len, scale, lam_init):
    ka_refs = rest[0 * n_pg:1 * n_pg]
    va_refs = rest[1 * n_pg:2 * n_pg]
    kb_refs = rest[2 * n_pg:3 * n_pg]
    vb_refs = rest[3 * n_pg:4 * n_pg]
    lf_refs = rest[4 * n_pg:5 * n_pg]
    oa_ref, ob_ref = rest[5 * n_pg:5 * n_pg + 2]
    ma_s, la_s, acca_s, mb_s, lb_s, accb_s, carry_s = rest[5 * n_pg + 2:]
    j = pl.program_id(1)
    n_groups = pl.num_programs(1)
    n_ra = 2 * n_t * H_A
    dn = (((1,), (1,)), ((), ()))
    qa = qa_ref[...]
    qb = qb_ref[...]
    slope = rowa_ref[:, 0:1]
    t_row = rowa_ref[:, 1:2]
    gt = gt_ref[...]

    @pl.when(j == 0)
    def _():
        for m_s, l_s, a_s in ((ma_s, la_s, acca_s), (mb_s, lb_s, accb_s)):
            m_s[...] = jnp.full_like(m_s, NEG_INF)
            l_s[...] = jnp.zeros_like(l_s)
            a_s[...] = jnp.zeros_like(a_s)
        carry_s[...] = jnp.zeros_like(carry_s)

    def softmax_step(s, m_s, l_s):
        m_prev = m_s[...]
        m_new = jnp.maximum(m_prev, jnp.max(s, axis=1, keepdims=True))
        alpha = jnp.exp(m_prev - m_new)
        p = jnp.exp(s - m_new)
        l_s[...] = alpha * l_s[...] + jnp.sum(p, axis=1, keepdims=True)
        m_s[...] = m_new
        return p, alpha

    def spread(p):
        n_tiles = p.shape[1] // MXU_TILE
        pb = p.astype(BF16)
        stacked = jnp.concatenate(
            [pb[:, c * MXU_TILE:(c + 1) * MXU_TILE] for c in range(n_tiles)], axis=0)
        rep = jnp.dot(stacked, gt, preferred_element_type=F32)
        halves = []
        for e in range(2):
            me = me_ref[e]
            halves.append(jnp.concatenate(
                [rep[c * n_ra:(c + 1) * n_ra] * me for c in range(n_tiles)], axis=1))
        return jnp.concatenate(halves, axis=0).astype(BF16)

    def attend_a(s, vs):
        p, alpha = softmax_step(s, ma_s, la_s)
        p2 = spread(p)
        pv = None
        for g, v in enumerate(vs):
            rows = v.shape[0]
            y = jnp.dot(p2[:, g * rows:(g + 1) * rows], v, preferred_element_type=F32)
            pv = y if pv is None else pv + y
        acca_s[...] = jnp.concatenate([alpha, alpha], axis=0) * acca_s[...] + pv

    def attend_b(s, vs):
        p, alpha = softmax_step(s, mb_s, lb_s)
        pb = p.astype(BF16)
        pv = None
        for g, v in enumerate(vs):
            rows = v.shape[0]
            y = jnp.dot(pb[:, g * rows:(g + 1) * rows], v, preferred_element_type=F32)
            pv = y if pv is None else pv + y
        accb_s[...] = alpha * accb_s[...] + pv

    def scores(q, ks):
        return jnp.concatenate(
            [lax.dot_general(q, k, dn, preferred_element_type=F32) for k in ks], axis=1) * scale

    lfr = lfr_ref[...]
    parts = [lfr[0:H_B]]
    for t in range(1, n_t):
        parts.append(parts[-1] + lfr[t * H_B:(t + 1) * H_B])
    cn_col = jnp.concatenate(parts, axis=0)

    group = n_groups - 1 - j
    ta = ta_ref[...]
    bias_a = jnp.concatenate(
        [ta + slope * ((group * n_pg + g).astype(F32) * float(PAGE_SIZE) - (p_len + t_row))
         for g in range(n_pg)], axis=1)
    attend_a(scores(qa, [r[...].astype(BF16) for r in ka_refs]) + bias_a,
             [r[...].astype(BF16) for r in va_refs])

    lane = lax.broadcasted_iota(jnp.int32, (H_B, PAGE_ROWS), 1)
    tail = carry_s[...]
    sufs = [None] * n_pg
    for g in range(n_pg - 1, -1, -1):
        lf = lf_refs[g][...]
        inc = lf
        shift = 8
        while shift < PAGE_ROWS:
            inc = inc + jnp.where(lane < PAGE_ROWS - shift,
                                  pltpu.roll(inc, PAGE_ROWS - shift, axis=1), 0.0)
            shift *= 2
        sufs[g] = (inc - lf) + tail
        tail = tail + inc[:, 0:1]
    carry_s[...] = tail
    suf = jnp.concatenate(sufs, axis=1)
    bias_b = (jnp.concatenate([suf] * n_t, axis=0) + cn_col
              + jnp.concatenate([tb_ref[...]] * n_pg, axis=1))
    attend_b(scores(qb, [r[...].astype(BF16) for r in kb_refs]) + bias_b,
             [r[...].astype(BF16) for r in vb_refs])

    @pl.when(j == n_groups - 1)
    def _():
        attend_a(scores(qa, [kan_ref[...]]) + tan_ref[...], [van_ref[...]])
        lfl = lfl_ref[...]
        ln = lax.broadcasted_iota(jnp.int32, lfl.shape, 1)
        cn_lane = lfl
        shift = H_B
        while shift < n_t * H_B:
            cn_lane = cn_lane + jnp.where(ln >= shift, pltpu.roll(cn_lane, shift, axis=1), 0.0)
            shift *= 2
        n_new = tbn_ref.shape[1]
        cn_keys = _tile_lanes(cn_lane, n_new // LANES)
        attend_b(scores(qb, [kbn_ref[...]]) + (tbn_ref[...] + (cn_col - cn_keys)), [vbn_ref[...]])

        lam = _diff_lambda_vec(lq1_ref, lk1_ref, lq2_ref, lk2_ref, lam_init)
        la = la_s[...]
        fa = acca_s[...] / jnp.concatenate([la, la], axis=0)
        half = n_ra // 2
        o = [fa[e * n_ra:e * n_ra + half] - lam * fa[e * n_ra + half:(e + 1) * n_ra]
             for e in range(2)]
        ms = (jnp.sum(o[0] * o[0], axis=1, keepdims=True)
              + jnp.sum(o[1] * o[1], axis=1, keepdims=True)) / (2.0 * HD_A)
        inv = lax.rsqrt(ms + EPS)
        for e in range(2):
            oa_ref[e] = o[e] * inv * gsub_ref[:, e * HD_A:(e + 1) * HD_A] * (1.0 - lam_init)
        ob_ref[...] = accb_s[...] / lb_s[...]


def _even_sample_attention(qa2, qb2, lf_rows, lf_lanes, new_kv, pools, page_table, pool_off,
                           lam_params, g_sub, lam_init):
    pool_ak, pool_av, pool_bk, pool_bv, pool_lf = pools
    bsz, n_ra, hd = qa2.shape
    n_rb = qb2.shape[1]
    n_t = n_rb // H_B
    n_new = new_kv[0].shape[1]
    n_pages = page_table.shape[1]
    n_pg = _pick(n_pages, (4, 2, 1))
    n_groups = n_pages // n_pg
    tables = _even_sample_tables(n_t, n_new)

    def page_map(g):
        def index(b, j, pt):
            return (pool_off + pt[b * n_pages + (n_groups - 1 - j) * n_pg + g], 0, 0)
        return index

    per_b = lambda b, j, pt: (b, 0, 0)
    const2 = lambda b, j, pt: (0, 0)
    const3 = lambda b, j, pt: (0, 0, 0)
    vec = pl.BlockSpec((1, HD_A), const2)
    in_specs = [pl.BlockSpec((None, n_ra, hd), per_b),
                pl.BlockSpec((None, n_rb, hd), per_b),
                pl.BlockSpec((None, n_rb, 1), per_b),
                pl.BlockSpec((None, 1, LANES), per_b)]
    in_specs += [pl.BlockSpec((None, n_new, hd), per_b)] * 4
    in_specs += [pl.BlockSpec(tables[0].shape, const2), pl.BlockSpec(tables[1].shape, const2),
                 pl.BlockSpec(tables[2].shape, const2), pl.BlockSpec(tables[3].shape, const2),
                 pl.BlockSpec(tables[4].shape, const2), pl.BlockSpec(tables[5].shape, const3),
                 pl.BlockSpec(tables[6].shape, const2)]
    in_specs += [vec, vec, vec, vec, pl.BlockSpec((1, 2 * HD_A), const2)]
    args = [qa2, qb2, lf_rows, lf_lanes, *new_kv, *tables, *lam_params,
            g_sub.reshape(1, 2 * HD_A).astype(F32)]
    for pool in (pool_ak, pool_av, pool_bk, pool_bv):
        for g in range(n_pg):
            in_specs.append(pl.BlockSpec((None, PAGE_ROWS, hd), page_map(g)))
            args.append(pool)
    for g in range(n_pg):
        in_specs.append(pl.BlockSpec((None, H_B, PAGE_ROWS), page_map(g)))
        args.append(pool_lf)
    grid_spec = pltpu.PrefetchScalarGridSpec(
        num_scalar_prefetch=1,
        grid=(bsz, n_groups),
        in_specs=in_specs,
        out_specs=[pl.BlockSpec((None, 2, n_ra // 2, hd), lambda b, j, pt: (b, 0, 0, 0)),
                   pl.BlockSpec((None, n_rb, hd), per_b)],
        scratch_shapes=[pltpu.VMEM((n_ra, 1), F32), pltpu.VMEM((n_ra, 1), F32),
                        pltpu.VMEM((2 * n_ra, hd), F32),
                        pltpu.VMEM((n_rb, 1), F32), pltpu.VMEM((n_rb, 1), F32),
                        pltpu.VMEM((n_rb, hd), F32),
                        pltpu.VMEM((H_B, 1), F32)])
    return pl.pallas_call(
        functools.partial(_even_sample_kernel, n_pg=n_pg, n_t=n_t,
                          p_len=float(n_pages * PAGE_SIZE), scale=HD_A ** -0.5,
                          lam_init=lam_init),
        grid_spec=grid_spec,
        out_shape=[jax.ShapeDtypeStruct((bsz, 2, n_ra // 2, hd), F32),
                   jax.ShapeDtypeStruct((bsz, n_rb, hd), F32)],
        compiler_params=_params("parallel", "arbitrary"),
        name="even_sample_attention",
    )(page_table.reshape(-1), *args)


def _pattern_count(dist):
    return sum(((dist >= 0) & (dist <= w) & (dist % d == 0)).astype(jnp.int32)
               for w, d in C_PATTERNS)


def _count_bias(dist, slopes, same_head=True):
    cnt = _pattern_count(dist)
    logc = jnp.where((cnt > 0) & same_head, jnp.log(jnp.maximum(cnt, 1).astype(F32)), NEG_INF)
    return logc - slopes * dist.astype(F32)


def _dilated_bias(t, n_off):
    idx = jnp.arange(t, dtype=jnp.int32)
    dist = (idx[None, :, None] - idx[None, None, :]
            + t * jnp.arange(n_off, dtype=jnp.int32)[:, None, None])
    return _count_bias(dist[None], _alibi_slopes(H_C)[:, None, None, None])


def _dil_kernel(q_ref, k_ref, v_ref, bias_ref, o_ref, m_s, l_s, acc_s, *, t, n_off, scale):
    i = pl.program_id(1)
    q = q_ref[...]
    m_s[...] = jnp.full_like(m_s, NEG_INF)
    l_s[...] = jnp.zeros_like(l_s)
    acc_s[...] = jnp.zeros_like(acc_s)
    reps = t // LANES
    dn = (((1,), (1,)), ((), ()))
    for off in range(n_off):
        @pl.when(i >= off)
        def _(off=off):
            start = pl.multiple_of((i - off) * t, t)
            k = k_ref[pl.ds(start, t), :]
            v = v_ref[pl.ds(start, t), :]
            s = lax.dot_general(q, k, dn, preferred_element_type=F32) * scale + bias_ref[off]
            m_prev = m_s[...]
            m_new = jnp.maximum(m_prev, jnp.max(s, axis=1, keepdims=True))
            alpha = jnp.exp(m_prev - m_new)
            p = jnp.exp(s - _tile_lanes(m_new, reps))
            l_s[...] = alpha * l_s[...] + jnp.sum(p, axis=1, keepdims=True)
            acc_s[...] = alpha * acc_s[...] + jnp.dot(p.astype(BF16), v,
                                                      preferred_element_type=F32)
            m_s[...] = m_new
    o_ref[...] = (acc_s[...] / l_s[...]).astype(o_ref.dtype)


def _dilated_attention(q, k, v):
    s_len = q.shape[0]
    t = _pick(s_len, (512, 256, 128))
    n_off = min(C_WMAX // t + 1, s_len // t)
    bias = _dilated_bias(t, n_off)
    return pl.pallas_call(
        functools.partial(_dil_kernel, t=t, n_off=n_off, scale=HD_C ** -0.5),
        grid=(H_C, s_len // t),
        in_specs=[pl.BlockSpec((t, HD_C), lambda h, i: (i, h)),
                  pl.BlockSpec((s_len, HD_C), lambda h, i: (0, h)),
                  pl.BlockSpec((s_len, HD_C), lambda h, i: (0, h)),
                  pl.BlockSpec((None, n_off, t, t), lambda h, i: (h, 0, 0, 0))],
        out_specs=pl.BlockSpec((t, HD_C), lambda h, i: (i, h)),
        out_shape=jax.ShapeDtypeStruct((s_len, H_C * HD_C), BF16),
        scratch_shapes=[pltpu.VMEM((t, LANES), F32), pltpu.VMEM((t, LANES), F32),
                        pltpu.VMEM((t, HD_C), F32)],
        compiler_params=_params("parallel", "parallel"),
        name="dilated_attention",
    )(q, k, v, bias)


def _c_sample_bias(buf_len, n_t, n_new_cols):
    slopes = _alibi_slopes(H_C)[None, :, None, None]
    t = jnp.arange(n_t, dtype=jnp.int32)[:, None, None, None]
    heads = jnp.arange(H_C, dtype=jnp.int32)
    same = heads[None, :, None, None] == heads[None, None, None, :]

    def table(pos):
        dist = buf_len + t - pos[None, None, :, None]
        return _count_bias(dist, slopes, same).reshape(n_t * H_C, pos.shape[0] * H_C)

    past = table(jnp.arange(buf_len, dtype=jnp.int32))
    new = table(buf_len + jnp.arange(n_t, dtype=jnp.int32))
    new = jnp.pad(new, ((0, 0), (0, n_new_cols - new.shape[1])), constant_values=NEG_INF)
    return past, new


def _c_sample_kernel(q_ref, kc_ref, kx_ref, kn_ref, vc_ref, vx_ref, vn_ref, bias_ref, biasn_ref,
                     ok_ref, ov_ref, o_ref, m_s, l_s, acc_s, *, rows, new, scale):
    c = pl.program_id(1)
    last = pl.num_programs(1) - 1
    dn = (((1,), (1,)), ((), ()))
    q = q_ref[...]

    @pl.when(c == 0)
    def _():
        m_s[...] = jnp.full_like(m_s, NEG_INF)
        l_s[...] = jnp.zeros_like(l_s)
        acc_s[...] = jnp.zeros_like(acc_s)

    for src, nxt, fresh, dst in ((kc_ref, kx_ref, kn_ref, ok_ref), (vc_ref, vx_ref, vn_ref, ov_ref)):
        dst[pl.ds(0, rows - new), :] = src[pl.ds(new, rows - new), :]

        @pl.when(c < last)
        def _(nxt=nxt, dst=dst):
            dst[pl.ds(rows - new, new), :] = nxt[...]

        @pl.when(c == last)
        def _(fresh=fresh, dst=dst):
            dst[pl.ds(rows - new, new), :] = fresh[...]

    def attend(s, v):
        m_prev = m_s[...]
        m_new = jnp.maximum(m_prev, jnp.max(s, axis=1, keepdims=True))
        alpha = jnp.exp(m_prev - m_new)
        p = jnp.exp(s - m_new)
        l_s[...] = alpha * l_s[...] + jnp.sum(p, axis=1, keepdims=True)
        acc_s[...] = alpha * acc_s[...] + jnp.dot(p.astype(BF16), v, preferred_element_type=F32)
        m_s[...] = m_new

    s = lax.dot_general(q, kc_ref[...].astype(BF16), dn, preferred_element_type=F32)
    attend(s * scale + bias_ref[...], vc_ref[...].astype(BF16))

    @pl.when(c == last)
    def _():
        pad = jnp.zeros((biasn_ref.shape[1] - new, kn_ref.shape[1]), F32)
        kn = jnp.concatenate([kn_ref[...], pad], axis=0).astype(BF16)
        vn = jnp.concatenate([vn_ref[...], pad], axis=0).astype(BF16)
        s_n = lax.dot_general(q, kn, dn, preferred_element_type=F32)
        attend(s_n * scale + biasn_ref[...], vn)
        o_ref[...] = acc_s[...] / l_s[...]


def _c_sample(q2, cache_k, cache_v, k_new, v_new, batch_off, n_t):
    bsz, n_r, hd = q2.shape
    total = cache_k.shape[1]
    new = k_new.shape[1]
    buf_len = total // H_C
    rows = _pick(total, (4096, 2048, 1024))
    assert rows % new == 0 and total % rows == 0
    n_chunks = total // rows
    n_new_cols = max(LANES, new)
    bias, bias_n = _c_sample_bias(buf_len, n_t, n_new_cols)
    blocks_per_chunk = rows // new
    n_small = total // new

    cur = pl.BlockSpec((None, rows, hd), lambda b, c: (batch_off + b, c, 0))
    nxt = pl.BlockSpec((None, new, hd),
                       lambda b, c: (batch_off + b, jnp.minimum((c + 1) * blocks_per_chunk,
                                                                n_small - 1), 0))
    per_b = pl.BlockSpec((None, new, hd), lambda b, c: (b, 0, 0))
    out_buf = pl.BlockSpec((None, rows, hd), lambda b, c: (b, c, 0))
    return pl.pallas_call(
        functools.partial(_c_sample_kernel, rows=rows, new=new, scale=HD_C ** -0.5),
        grid=(bsz, n_chunks),
        in_specs=[pl.BlockSpec((None, n_r, hd), lambda b, c: (b, 0, 0)),
                  cur, nxt, per_b, cur, nxt, per_b,
                  pl.BlockSpec((n_r, rows), lambda b, c: (0, c)),
                  pl.BlockSpec((n_r, n_new_cols), lambda b, c: (0, 0))],
        out_specs=[out_buf, out_buf, pl.BlockSpec((None, n_r, hd), lambda b, c: (b, 0, 0))],
        out_shape=[jax.ShapeDtypeStruct((bsz, total, hd), F32),
                   jax.ShapeDtypeStruct((bsz, total, hd), F32),
                   jax.ShapeDtypeStruct((bsz, n_r, hd), F32)],
        scratch_shapes=[pltpu.VMEM((n_r, 1), F32), pltpu.VMEM((n_r, 1), F32),
                        pltpu.VMEM((n_r, hd), F32)],
        compiler_params=_params("parallel", "arbitrary"),
        name="window_sample_attention",
    )(q2, cache_k, cache_k, k_new, cache_v, cache_v, v_new, bias, bias_n)


def _cross_kernel(q_ref, k_ref, v_ref, o_ref, *, scale):
    dn = (((1,), (1,)), ((), ()))
    for h in range(H_X):
        cols = slice(h * HD_X, (h + 1) * HD_X)
        s = lax.dot_general(q_ref[:, cols], k_ref[:, cols].astype(BF16), dn,
                            preferred_element_type=F32) * scale
        m = jnp.max(s, axis=1, keepdims=True)
        e = jnp.exp(s - m)
        l = jnp.sum(e, axis=1, keepdims=True)
        o = jnp.dot(e.astype(BF16), v_ref[:, cols].astype(BF16), preferred_element_type=F32)
        o_ref[:, cols] = (o / l).astype(o_ref.dtype)


def _cross_attention(q, mem_k, mem_v):
    bsz, n_q, w = q.shape
    n_m = mem_k.shape[1]
    tq = _pick(n_q, (512, 256, 128))
    return pl.pallas_call(
        functools.partial(_cross_kernel, scale=HD_X ** -0.5),
        grid=(bsz, n_q // tq),
        in_specs=[pl.BlockSpec((None, tq, w), lambda b, i: (b, i, 0)),
                  pl.BlockSpec((None, n_m, w), lambda b, i: (b, 0, 0)),
                  pl.BlockSpec((None, n_m, w), lambda b, i: (b, 0, 0))],
        out_specs=pl.BlockSpec((None, tq, w), lambda b, i: (b, i, 0)),
        out_shape=jax.ShapeDtypeStruct((bsz, n_q, w), BF16),
        compiler_params=_params("parallel", "parallel"),
        name="cross_attention",
    )(q, mem_k, mem_v)


SAMPLE_ROWS = 16
NEW_KEY_ROWS = MXU_TILE


def _pad_rows(x, rows):
    return jnp.pad(x, ((0, 0), (0, rows - x.shape[1]), (0, 0)))


def kernel(x_prompt, x_sample, mem_prompt, cache_a_k, cache_a_v, cache_b_k, cache_b_v, cache_b_logf, cache_c_k, cache_c_v, cache_mem_k, cache_mem_v, page_table, norm_mix, norm_cross, norm_mem, norm_ffn, norm_final, w_in_even, b_forget, lambda_q1, lambda_k1, lambda_q2, lambda_k2, subln_a, w_out_even, w_in_odd, w_out_odd, w_xq, w_xkv, w_xo, w_gate_up, w_down):
    n_b, s_len, d_model = x_prompt.shape
    d_b, d_t, _ = x_sample.shape
    assert n_b == 1
    depth = norm_mix.shape[0]
    n_mem = mem_prompt.shape[1]
    xp = x_prompt.reshape(s_len, d_model)
    xs = x_sample.reshape(d_b * d_t, d_model)
    mem = mem_prompt.reshape(n_mem, d_model)
    wa = H_A * 2 * HD_A
    wb = H_B * HD_B
    wc = H_C * HD_C
    wx = H_X * HD_X

    n_even, n_pool = cache_a_k.shape[:2]
    pool_ak = cache_a_k.reshape(n_even * n_pool, PAGE_ROWS, HD_A)
    pool_av = (cache_a_v.reshape(n_even, n_pool, PAGE_SIZE, H_A, 2, HD_A)
               .transpose(0, 1, 2, 4, 3, 5).reshape(n_even * n_pool, PAGE_ROWS, HD_A))
    pool_bk = cache_b_k.reshape(n_even * n_pool, PAGE_ROWS, HD_B)
    pool_bv = cache_b_v.reshape(n_even * n_pool, PAGE_ROWS, HD_B)
    pool_lf = jnp.repeat(jnp.swapaxes(cache_b_logf, 2, 3), H_B, axis=3).reshape(
        n_even * n_pool, H_B, PAGE_ROWS)
    n_odd, _, buf_len = cache_c_k.shape[:3]
    win_k = cache_c_k.reshape(n_odd * d_b, buf_len * H_C, HD_C)
    win_v = cache_c_v.reshape(n_odd * d_b, buf_len * H_C, HD_C)

    ak_p, av_p, bk_p, bv_p, bl_p, ck_p, cv_p, mk_p, mv_p = [], [], [], [], [], [], [], [], []
    ak_s, av_s, bk_s, bv_s, bl_s, ck_s, cv_s = [], [], [], [], [], [], []

    for l in range(depth):
        j = l // 2
        xnp = _rmsnorm(xp, norm_mix[l], BF16)
        xns = _rmsnorm(xs, norm_mix[l], BF16)
        if l % 2 == 0:
            lam_init = 0.8 - 0.6 * math.exp(-0.3 * l)
            w_in = w_in_even[j]
            w_cols = [w_in[:, o:o + wa].astype(BF16) for o in range(0, 6 * wa, wa)]
            w_fz = jnp.pad(w_in[:, 6 * wa:], ((0, 0), (0, LANES - H_B))).astype(BF16)
            lam_params = [p[j].reshape(1, HD_A).astype(F32)
                          for p in (lambda_q1, lambda_k1, lambda_q2, lambda_k2)]
            w_out = w_out_even[j].astype(BF16)

            qa = _matmul([(xnp, w_cols[0])], out_f32=False, out_bf16=True)
            ka, ka16 = _matmul([(xnp, w_cols[1])], out_bf16=True)
            va, va16 = _matmul([(xnp, w_cols[2])], out_bf16=True)
            qb = _matmul([(xnp, w_cols[3])], out_f32=False, out_bf16=True)
            kb, kb16 = _matmul([(xnp, w_cols[4])], out_bf16=True)
            vb, vb16 = _matmul([(xnp, w_cols[5])], out_bf16=True)
            fz = _matmul([(xnp, w_fz)])[:, :H_B]
            lf_t, c_t = _logf_cumsum(fz.T, b_forget[j])
            za = _diff_attention(qa, ka16, va16, lam_params, subln_a[j], lam_init)
            zb = _fox_attention(qb, kb16, vb16, c_t)
            xp = _matmul([(za, w_out[:wa]), (zb, w_out[wa:])], res=xp)
            ak_p.append(ka.reshape(1, s_len, H_A, 2, HD_A))
            av_p.append(va.reshape(1, s_len, H_A, 2 * HD_A))
            bk_p.append(kb.reshape(1, s_len, H_B, HD_B))
            bv_p.append(vb.reshape(1, s_len, H_B, HD_B))
            bl_p.append(lf_t.T.reshape(1, s_len, H_B))

            qa_s = _matmul([(xns, w_cols[0])], out_f32=False, out_bf16=True)
            ka_s, ka_s16 = _matmul([(xns, w_cols[1])], out_bf16=True)
            va_s, va_s16 = _matmul([(xns, w_cols[2])], out_bf16=True)
            qb_s = _matmul([(xns, w_cols[3])], out_f32=False, out_bf16=True)
            kb_s, kb_s16 = _matmul([(xns, w_cols[4])], out_bf16=True)
            vb_s, vb_s16 = _matmul([(xns, w_cols[5])], out_bf16=True)
            fz_s = _matmul([(xns, w_fz)])[:, :H_B]
            lf_s_t, _ = _logf_cumsum(fz_s.T, b_forget[j])
            lf_s = lf_s_t.T
            qa2 = (qa_s.reshape(d_b, d_t, H_A, 2, HD_A).transpose(0, 3, 1, 2, 4)
                   .reshape(d_b, 2 * d_t * H_A, HD_A))
            qb2 = qb_s.reshape(d_b, d_t * H_B, HD_B)
            va_rows = (va_s16.reshape(d_b, d_t, H_A, 2, HD_A).transpose(0, 1, 3, 2, 4)
                       .reshape(d_b, d_t * 8, HD_A))
            new_kv = [_pad_rows(a, NEW_KEY_ROWS)
                      for a in (ka_s16.reshape(d_b, d_t * 8, HD_A), va_rows,
                                kb_s16.reshape(d_b, d_t * H_B, HD_B),
                                vb_s16.reshape(d_b, d_t * H_B, HD_B))]
            lf_rows = lf_s.reshape(d_b, d_t * H_B, 1)
            lf_lanes = _pad_rows(lf_rows, LANES).reshape(d_b, 1, LANES)
            oa_s, ob_s = _even_sample_attention(
                qa2, qb2, lf_rows, lf_lanes, new_kv,
                (pool_ak, pool_av, pool_bk, pool_bv, pool_lf), page_table, j * n_pool,
                lam_params, subln_a[j], lam_init)
            za_s = (oa_s.reshape(d_b, 2, d_t, H_A, HD_A).transpose(0, 2, 3, 1, 4)
                    .reshape(d_b * d_t, wa).astype(BF16))
            zb_s = ob_s.reshape(d_b * d_t, wb).astype(BF16)
            xs = _matmul([(za_s, w_out[:wa]), (zb_s, w_out[wa:])], res=xs)
            ak_s.append(ka_s.reshape(d_b, d_t, H_A, 2, HD_A))
            av_s.append(va_s.reshape(d_b, d_t, H_A, 2 * HD_A))
            bk_s.append(kb_s.reshape(d_b, d_t, H_B, HD_B))
            bv_s.append(vb_s.reshape(d_b, d_t, H_B, HD_B))
            bl_s.append(lf_s.reshape(d_b, d_t, H_B))
        else:
            w_in = w_in_odd[j]
            w_cols = [w_in[:, o:o + wc].astype(BF16) for o in range(0, 3 * wc, wc)]
            w_out = w_out_odd[j].astype(BF16)

            q16 = _matmul([(xnp, w_cols[0])], out_f32=False, out_bf16=True)
            k, k16 = _matmul([(xnp, w_cols[1])], out_bf16=True)
            v, v16 = _matmul([(xnp, w_cols[2])], out_bf16=True)
            o = _dilated_attention(q16, k16, v16)
            xp = _matmul([(o, w_out)], res=xp)
            keep = min(C_WMAX, s_len)
            ck_p.append(k[s_len - keep:].reshape(1, keep, H_C, HD_C))
            cv_p.append(v[s_len - keep:].reshape(1, keep, H_C, HD_C))

            q_s = _matmul([(xns, w_cols[0])], out_f32=False, out_bf16=True)
            k_s = _matmul([(xns, w_cols[1])])
            v_s = _matmul([(xns, w_cols[2])])
            new_k, new_v, o_s = _c_sample(
                q_s.reshape(d_b, d_t * H_C, HD_C), win_k, win_v,
                k_s.reshape(d_b, d_t * H_C, HD_C), v_s.reshape(d_b, d_t * H_C, HD_C),
                j * d_b, d_t)
            xs = _matmul([(o_s.reshape(d_b * d_t, wc).astype(BF16), w_out)], res=xs)
            ck_s.append(new_k.reshape(d_b, buf_len, H_C, HD_C))
            cv_s.append(new_v.reshape(d_b, buf_len, H_C, HD_C))

        w_q = w_xq[l].astype(BF16)
        w_kv = w_xkv[l].astype(BF16)
        w_o = w_xo[l].astype(BF16)
        memn = _rmsnorm(mem, norm_mem[l], BF16)
        mk = _matmul([(memn, w_kv[:, :wx])])
        mv = _matmul([(memn, w_kv[:, wx:])])
        mk_p.append(mk.reshape(1, n_mem, H_X, HD_X))
        mv_p.append(mv.reshape(1, n_mem, H_X, HD_X))
        qx = _matmul([(_rmsnorm(xp, norm_cross[l], BF16), w_q)], out_f32=False, out_bf16=True)
        ox = _cross_attention(qx.reshape(1, s_len, wx), mk.reshape(1, n_mem, wx),
                              mv.reshape(1, n_mem, wx))
        xp = _matmul([(ox.reshape(s_len, wx), w_o)], res=xp)
        qx_s = _matmul([(_rmsnorm(xs, norm_cross[l], BF16), w_q)], out_f32=False, out_bf16=True)
        ox_s = _cross_attention(_pad_rows(qx_s.reshape(d_b, d_t, wx), SAMPLE_ROWS),
                                cache_mem_k[l].reshape(d_b, n_mem, wx),
                                cache_mem_v[l].reshape(d_b, n_mem, wx))
        xs = _matmul([(ox_s[:, :d_t].reshape(d_b * d_t, wx), w_o)], res=xs)

        d_ff = w_down.shape[1]
        w_g = w_gate_up[l][:, :d_ff].astype(BF16)
        w_u = w_gate_up[l][:, d_ff:].astype(BF16)
        w_d = w_down[l].astype(BF16)
        xp = _ffn(_rmsnorm(xp, norm_ffn[l], BF16), w_g, w_u, w_d, xp)
        xs = _ffn(_rmsnorm(xs, norm_ffn[l], BF16), w_g, w_u, w_d, xs)

    y_prompt = _rmsnorm(xp, norm_final, F32).reshape(1, s_len, d_model)
    y_sample = _rmsnorm(xs, norm_final, F32).reshape(d_b, d_t, d_model)
    st = jnp.stack
    return (y_prompt, y_sample,
            st(ak_p), st(av_p), st(bk_p), st(bv_p), st(bl_p), st(ck_p), st(cv_p), st(mk_p), st(mv_p),
            st(ak_s), st(av_s), st(bk_s), st(bv_s), st(bl_s), st(ck_s), st(cv_s))
```

```python
import functools
import math

import numpy as np
import jax
import jax.numpy as jnp
from jax import lax
from jax.experimental import pallas as pl
from jax.experimental.pallas import tpu as pltpu

F32 = jnp.float32
BF16 = jnp.bfloat16

EPS = 1e-6
PAGE_SIZE = 128
H_A, HD_A = 4, 128
H_B, HD_B = 8, 128
H_C, HD_C = 16, 128
H_X, HD_X = 4, 128
C_PATTERNS = ((128, 1), (512, 4), (2048, 16))
C_WMAX = 2048
LANES = 128
MXU_TILE = 256
VMEM_LIMIT = 52 * 1024 * 1024
NEG_INF = float("-inf")
LOG2E = 1.4426950408889634


def _params(*sem):
    return pltpu.CompilerParams(dimension_semantics=sem, vmem_limit_bytes=VMEM_LIMIT)


def _tile_lanes(x, reps):
    return x if reps == 1 else jnp.concatenate([x] * reps, axis=1)


def _pick(n, pref):
    for t in pref:
        if n % t == 0:
            return t
    return n


def _rms_kernel(x_ref, g_ref, o_ref):
    x = x_ref[...]
    ms = jnp.mean(x * x, axis=-1, keepdims=True)
    o_ref[...] = ((x * lax.rsqrt(ms + EPS)) * g_ref[...]).astype(o_ref.dtype)


def _rmsnorm(x, g, out_dtype):
    m, d = x.shape
    tm = _pick(m, (512, 256, 128))
    return pl.pallas_call(
        _rms_kernel,
        grid=(m // tm,),
        in_specs=[pl.BlockSpec((tm, d), lambda i: (i, 0)),
                  pl.BlockSpec((1, d), lambda i: (0, 0))],
        out_specs=pl.BlockSpec((tm, d), lambda i: (i, 0)),
        out_shape=jax.ShapeDtypeStruct((m, d), out_dtype),
        compiler_params=_params("parallel"),
        name="rmsnorm",
    )(x, g.reshape(1, d).astype(F32))


def _mm_kernel(*refs, n_pairs, has_res, out_f32, out_bf16, bf16_scale):
    acc = None
    for p in range(n_pairs):
        y = jnp.dot(refs[2 * p][...], refs[2 * p + 1][...], preferred_element_type=F32)
        acc = y if acc is None else acc + y
    pos = 2 * n_pairs
    if has_res:
        acc = refs[pos][...] + acc
        pos += 1
    if out_f32:
        refs[pos][...] = acc
        pos += 1
    if out_bf16:
        refs[pos][...] = (acc if bf16_scale is None else acc * bf16_scale).astype(BF16)


def _matmul(pairs, res=None, out_f32=True, out_bf16=False, bf16_scale=None):
    m = pairs[0][0].shape[0]
    n = pairs[0][1].shape[1]
    tm = _pick(m, (1024, 512, 256, 128))
    tn = _pick(n, (1024, 512, 256, 128))
    in_specs, args = [], []
    for a, w in pairs:
        k = a.shape[1]
        in_specs += [pl.BlockSpec((tm, k), lambda i, j: (i, 0)),
                     pl.BlockSpec((k, tn), lambda i, j: (0, j))]
        args += [a, w]
    if res is not None:
        in_specs.append(pl.BlockSpec((tm, tn), lambda i, j: (i, j)))
        args.append(res)
    out_specs, out_shape = [], []
    if out_f32:
        out_specs.append(pl.BlockSpec((tm, tn), lambda i, j: (i, j)))
        out_shape.append(jax.ShapeDtypeStruct((m, n), F32))
    if out_bf16:
        out_specs.append(pl.BlockSpec((tm, tn), lambda i, j: (i, j)))
        out_shape.append(jax.ShapeDtypeStruct((m, n), BF16))
    outs = pl.pallas_call(
        functools.partial(_mm_kernel, n_pairs=len(pairs), has_res=res is not None,
                          out_f32=out_f32, out_bf16=out_bf16, bf16_scale=bf16_scale),
        grid=(m // tm, n // tn),
        in_specs=in_specs, out_specs=out_specs, out_shape=out_shape,
        compiler_params=_params("parallel", "parallel"),
        name="matmul",
    )(*args)
    return outs[0] if len(outs) == 1 else tuple(outs)


def _ffn_kernel(xn_ref, wg_ref, wu_ref, wd_ref, res_ref, o_ref, acc_ref):
    f = pl.program_id(1)

    @pl.when(f == 0)
    def _():
        acc_ref[...] = jnp.zeros_like(acc_ref)

    xn = xn_ref[...]
    g = jnp.dot(xn, wg_ref[...], preferred_element_type=F32)
    u = jnp.dot(xn, wu_ref[...], preferred_element_type=F32)
    h = (g * jax.nn.sigmoid(g)) * u
    acc_ref[...] += jnp.dot(h.astype(BF16), wd_ref[...], preferred_element_type=F32)

    @pl.when(f == pl.num_programs(1) - 1)
    def _():
        o_ref[...] = res_ref[...] + acc_ref[...]


def _ffn(xn, wg, wu, wd, res):
    m, d = xn.shape
    dff = wg.shape[1]
    tm = _pick(m, (512, 256, 128))
    tf = _pick(dff, (512, 256, 128))
    return pl.pallas_call(
        _ffn_kernel,
        grid=(m // tm, dff // tf),
        in_specs=[pl.BlockSpec((tm, d), lambda i, f: (i, 0)),
                  pl.BlockSpec((d, tf), lambda i, f: (0, f)),
                  pl.BlockSpec((d, tf), lambda i, f: (0, f)),
                  pl.BlockSpec((tf, d), lambda i, f: (f, 0)),
                  pl.BlockSpec((tm, d), lambda i, f: (i, 0))],
        out_specs=pl.BlockSpec((tm, d), lambda i, f: (i, 0)),
        out_shape=jax.ShapeDtypeStruct((m, d), F32),
        scratch_shapes=[pltpu.VMEM((tm, d), F32)],
        compiler_params=_params("parallel", "arbitrary"),
        name="swiglu",
    )(xn, wg, wu, wd, res)


def _log_sigmoid(x):
    return jnp.minimum(x, 0.0) - jnp.log1p(jnp.exp(-jnp.abs(x)))


def _logf_kernel(fz_ref, b_ref, lf_ref, c2_ref):
    lf = _log_sigmoid(fz_ref[...] + b_ref[...])
    lf_ref[...] = lf
    n = lf.shape[1]
    lane = lax.broadcasted_iota(jnp.int32, lf.shape, 1)
    c = lf
    shift = 1
    while shift < n:
        c = c + jnp.where(lane >= shift, pltpu.roll(c, shift, axis=1), 0.0)
        shift *= 2
    c2_ref[...] = c * LOG2E


def _logf_cumsum(fz_t, b_f):
    h, n = fz_t.shape
    return pl.pallas_call(
        _logf_kernel,
        out_shape=[jax.ShapeDtypeStruct((h, n), F32)] * 2,
        name="logf_cumsum",
    )(fz_t, b_f.reshape(h, 1).astype(F32))


def _online_update(u, v, m_s, l_s, acc_s, row_shift):
    reps = u.shape[1] // LANES
    m_prev = m_s[...]
    m_new = jnp.maximum(m_prev, jnp.max(u, axis=1, keepdims=True) + row_shift)
    alpha = jnp.exp2(m_prev - m_new)
    p = jnp.exp2(u - _tile_lanes(m_new - row_shift, reps))
    l_s[...] = alpha * l_s[...] + jnp.sum(p, axis=1, keepdims=True)
    acc_s[...] = (_tile_lanes(alpha, acc_s.shape[1] // LANES) * acc_s[...]
                  + jnp.dot(p.astype(BF16), v, preferred_element_type=F32))
    m_s[...] = m_new


def _causal_mask(u):
    row = lax.broadcasted_iota(jnp.int32, u.shape, 0)
    col = lax.broadcasted_iota(jnp.int32, u.shape, 1)
    return jnp.where(row >= col, u, NEG_INF)


def _causal_chunks(i, scores, update):
    def pair(jj, c):
        j0 = 2 * jj
        u0, u1 = scores(j0, False), scores(j0 + 1, False)
        update(u0, j0)
        update(u1, j0 + 1)
        return c

    lax.fori_loop(0, lax.div(i, 2), pair, 0)
    odd = lax.rem(i, 2) == 1

    @pl.when(odd)
    def _():
        u0, u1 = scores(i - 1, False), scores(i, True)
        update(u0, i - 1)
        update(u1, i)

    @pl.when(jnp.logical_not(odd))
    def _():
        update(scores(i, True), i)


def _fox_kernel(q_ref, k_ref, v_ref, ccol_ref, crow_ref, o_ref, m_s, l_s, acc_s, *, t):
    i = pl.program_id(1)
    q = q_ref[...]
    cq = jnp.broadcast_to(ccol_ref[...], (t, LANES))
    m_s[...] = jnp.full_like(m_s, NEG_INF)
    l_s[...] = jnp.zeros_like(l_s)
    acc_s[...] = jnp.zeros_like(acc_s)
    dn = (((1,), (1,)), ((), ()))

    def scores(j, masked):
        k = k_ref[pl.ds(pl.multiple_of(j * t, t), t), :]
        u = lax.dot_general(q, k, dn, preferred_element_type=F32) - crow_ref[j]
        return _causal_mask(u) if masked else u

    def update(u, j):
        v = v_ref[pl.ds(pl.multiple_of(j * t, t), t), :]
        _online_update(u, v, m_s, l_s, acc_s, cq)

    _causal_chunks(i, scores, update)
    o_ref[...] = (acc_s[...] / l_s[...]).astype(o_ref.dtype)


def _fox_attention(qb, kb, vb, c_t):
    s_len = qb.shape[0]
    t = _pick(s_len, (512, 256, 128))
    nc = s_len // t
    c_col = c_t.reshape(H_B, s_len, 1)
    c_row = c_t.reshape(H_B, nc, 1, t)
    return pl.pallas_call(
        functools.partial(_fox_kernel, t=t),
        grid=(H_B, nc),
        in_specs=[pl.BlockSpec((t, HD_B), lambda h, i: (i, h)),
                  pl.BlockSpec((s_len, HD_B), lambda h, i: (0, h)),
                  pl.BlockSpec((s_len, HD_B), lambda h, i: (0, h)),
                  pl.BlockSpec((None, t, 1), lambda h, i: (h, i, 0)),
                  pl.BlockSpec((None, nc, 1, t), lambda h, i: (h, 0, 0, 0))],
        out_specs=pl.BlockSpec((t, HD_B), lambda h, i: (i, h)),
        out_shape=jax.ShapeDtypeStruct((s_len, H_B * HD_B), BF16),
        scratch_shapes=[pltpu.VMEM((t, LANES), F32), pltpu.VMEM((t, LANES), F32),
                        pltpu.VMEM((t, HD_B), F32)],
        compiler_params=_params("parallel", "parallel"),
        name="forget_attention",
    )(qb, kb, vb, c_col, c_row)


def _diff_lambda_vec(lq1_ref, lk1_ref, lq2_ref, lk2_ref, lam_init):
    a = jnp.sum(lq1_ref[...] * lk1_ref[...], axis=1, keepdims=True)
    b = jnp.sum(lq2_ref[...] * lk2_ref[...], axis=1, keepdims=True)
    return jnp.exp(a) - jnp.exp(b) + lam_init


def _diff_kernel(q_ref, k_ref, v_ref, slope_ref, lq1_ref, lk1_ref, lq2_ref, lk2_ref, g_ref,
                 o_ref, m1_s, l1_s, a1_s, m2_s, l2_s, a2_s, *, t, lam_init):
    i = pl.program_id(1)
    q1 = q_ref[:, :HD_A]
    q2 = q_ref[:, HD_A:]
    slope = slope_ref[...]
    row_pos = (i * t + lax.broadcasted_iota(jnp.int32, (t, LANES), 0)).astype(F32)
    row_shift = -(slope * row_pos)
    for m_s, l_s, a_s in ((m1_s, l1_s, a1_s), (m2_s, l2_s, a2_s)):
        m_s[...] = jnp.full_like(m_s, NEG_INF)
        l_s[...] = jnp.zeros_like(l_s)
        a_s[...] = jnp.zeros_like(a_s)
    reps = t // LANES
    dn = (((1,), (1,)), ((), ()))

    def scores(j, masked):
        k = k_ref[pl.ds(pl.multiple_of(j * t, t), t), :]
        col_pos = (j * t + lax.broadcasted_iota(jnp.int32, (1, t), 1)).astype(F32)
        col_term = _tile_lanes(slope, reps) * col_pos
        us = []
        for qm, km in ((q1, k[:, :HD_A]), (q2, k[:, HD_A:])):
            u = lax.dot_general(qm, km, dn, preferred_element_type=F32) + col_term
            us.append(_causal_mask(u) if masked else u)
        return us

    def update(us, j):
        v = v_ref[pl.ds(pl.multiple_of(j * t, t), t), :]
        _online_update(us[0], v, m1_s, l1_s, a1_s, row_shift)
        _online_update(us[1], v, m2_s, l2_s, a2_s, row_shift)

    _causal_chunks(i, scores, update)
    lam = _diff_lambda_vec(lq1_ref, lk1_ref, lq2_ref, lk2_ref, lam_init)
    o = (a1_s[...] / _tile_lanes(l1_s[...], 2)
         - lam * (a2_s[...] / _tile_lanes(l2_s[...], 2)))
    y = o * lax.rsqrt(jnp.mean(o * o, axis=1, keepdims=True) + EPS) * g_ref[...] * (1.0 - lam_init)
    o_ref[...] = y.astype(o_ref.dtype)


def _alibi_slopes(n):
    return jnp.asarray(2.0 ** (-8.0 * np.arange(1, n + 1) / n), dtype=F32)


def _diff_attention(qa, ka, va, lam_params, g_sub, lam_init):
    s_len = qa.shape[0]
    t = _pick(s_len, (512, 256, 128))
    nc = s_len // t
    w = 2 * HD_A
    slopes = jnp.broadcast_to((_alibi_slopes(H_A) * LOG2E)[:, None, None], (H_A, 1, LANES))
    vec = pl.BlockSpec((1, HD_A), lambda h, i: (0, 0))
    return pl.pallas_call(
        functools.partial(_diff_kernel, t=t, lam_init=lam_init),
        grid=(H_A, nc),
        in_specs=[pl.BlockSpec((t, w), lambda h, i: (i, h)),
                  pl.BlockSpec((s_len, w), lambda h, i: (0, h)),
                  pl.BlockSpec((s_len, w), lambda h, i: (0, h)),
                  pl.BlockSpec((None, 1, LANES), lambda h, i: (h, 0, 0)),
                  vec, vec, vec, vec,
                  pl.BlockSpec((1, w), lambda h, i: (0, 0))],
        out_specs=pl.BlockSpec((t, w), lambda h, i: (i, h)),
        out_shape=jax.ShapeDtypeStruct((s_len, H_A * w), BF16),
        scratch_shapes=[pltpu.VMEM((t, LANES), F32), pltpu.VMEM((t, LANES), F32),
                        pltpu.VMEM((t, w), F32),
                        pltpu.VMEM((t, LANES), F32), pltpu.VMEM((t, LANES), F32),
                        pltpu.VMEM((t, w), F32)],
        compiler_params=_params("parallel", "parallel"),
        name="diff_attention",
    )(qa, ka, va, slopes, *lam_params, g_sub.reshape(1, w).astype(F32))


PAGE_ROWS = PAGE_SIZE * 8


def _even_sample_tables(n_t, n_new_cols):
    slopes = 2.0 ** (-8.0 * np.arange(1, H_A + 1) / H_A)
    ninf = -np.inf
    col = np.arange(PAGE_ROWS)
    p_col, j_col = col >> 3, col & 7
    ra = np.arange(2 * n_t * H_A)
    m_r, t_r, h_r = ra // (n_t * H_A), (ra // H_A) % n_t, ra % H_A
    j_r = 2 * h_r + m_r
    sl_r = slopes[h_r]
    ta = np.where(j_col[None, :] == j_r[:, None], sl_r[:, None] * p_col[None, :], ninf)
    rb = np.arange(n_t * H_B)
    tb_t, tb_h = rb // H_B, rb % H_B
    tb = np.where(j_col[None, :] == tb_h[:, None], 0.0, ninf)
    ncol = np.arange(n_new_cols)
    u_col, jn_col = ncol >> 3, ncol & 7
    ok_a = (jn_col[None, :] == j_r[:, None]) & (u_col[None, :] <= t_r[:, None])
    tan = np.where(ok_a, -sl_r[:, None] * (t_r[:, None] - u_col[None, :]), ninf)
    ok_b = (jn_col[None, :] == tb_h[:, None]) & (u_col[None, :] <= tb_t[:, None])
    tbn = np.where(ok_b, 0.0, ninf)
    rowa = np.stack([sl_r, t_r.astype(np.float64)], axis=1)
    lane = np.arange(MXU_TILE) & 7
    me = np.stack([(lane[None, :] == (4 * e + h_r)[:, None]).astype(np.float64) for e in range(2)])
    gi = np.arange(MXU_TILE) >> 3
    gt = (gi[:, None] == gi[None, :]).astype(np.float64)
    f = lambda a: jnp.asarray(a, F32)
    return f(ta), f(tb), f(tan), f(tbn), f(rowa), f(me), jnp.asarray(gt, BF16)


def _even_sample_kernel(pt_ref, qa_ref, qb_ref, lfr_ref, lfl_ref, kan_ref, van_ref, kbn_ref, vbn_ref,
                        ta_ref, tb_ref, tan_ref, tbn_ref, rowa_ref, me_ref, gt_ref,
                        lq1_ref, lk1_ref, lq2_ref, lk2_ref, gsub_ref, *rest,
                        n_pg, n_t, p_len, scale, lam_init):
    ka_refs = rest[0 * n_pg:1 * n_pg]
    va_refs = rest[1 * n_pg:2 * n_pg]
    kb_refs = rest[2 * n_pg:3 * n_pg]
    vb_refs = rest[3 * n_pg:4 * n_pg]
    lf_refs = rest[4 * n_pg:5 * n_pg]
    oa_ref, ob_ref = rest[5 * n_pg:5 * n_pg + 2]
    ma_s, la_s, acca_s, mb_s, lb_s, accb_s, carry_s = rest[5 * n_pg + 2:]
    j = pl.program_id(1)
    n_groups = pl.num_programs(1)
    n_ra = 2 * n_t * H_A
    dn = (((1,), (1,)), ((), ()))
    qa = qa_ref[...]
    qb = qb_ref[...]
    slope = rowa_ref[:, 0:1]
    t_row = rowa_ref[:, 1:2]
    gt = gt_ref[...]

    @pl.when(j == 0)
    def _():
        for m_s, l_s, a_s in ((ma_s, la_s, acca_s), (mb_s, lb_s, accb_s)):
            m_s[...] = jnp.full_like(m_s, NEG_INF)
            l_s[...] = jnp.zeros_like(l_s)
            a_s[...] = jnp.zeros_like(a_s)
        carry_s[...] = jnp.zeros_like(carry_s)

    def softmax_step(s, m_s, l_s):
        m_prev = m_s[...]
        m_new = jnp.maximum(m_prev, jnp.max(s, axis=1, keepdims=True))
        alpha = jnp.exp(m_prev - m_new)
        p = jnp.exp(s - m_new)
        l_s[...] = alpha * l_s[...] + jnp.sum(p, axis=1, keepdims=True)
        m_s[...] = m_new
        return p, alpha

    def spread(p):
        n_tiles = p.shape[1] // MXU_TILE
        pb = p.astype(BF16)
        stacked = jnp.concatenate(
            [pb[:, c * MXU_TILE:(c + 1) * MXU_TILE] for c in range(n_tiles)], axis=0)
        rep = jnp.dot(stacked, gt, preferred_element_type=F32)
        halves = []
        for e in range(2):
            me = me_ref[e]
            halves.append(jnp.concatenate(
                [rep[c * n_ra:(c + 1) * n_ra] * me for c in range(n_tiles)], axis=1))
        return jnp.concatenate(halves, axis=0).astype(BF16)

    def attend_a(s, vs):
        p, alpha = softmax_step(s, ma_s, la_s)
        p2 = spread(p)
        pv = None
        for g, v in enumerate(vs):
            rows = v.shape[0]
            y = jnp.dot(p2[:, g * rows:(g + 1) * rows], v, preferred_element_type=F32)
            pv = y if pv is None else pv + y
        acca_s[...] = jnp.concatenate([alpha, alpha], axis=0) * acca_s[...] + pv

    def attend_b(s, vs):
        p, alpha = softmax_step(s, mb_s, lb_s)
        pb = p.astype(BF16)
        pv = None
        for g, v in enumerate(vs):
            rows = v.shape[0]
            y = jnp.dot(pb[:, g * rows:(g + 1) * rows], v, preferred_element_type=F32)
            pv = y if pv is None else pv + y
        accb_s[...] = alpha * accb_s[...] + pv

    def scores(q, ks):
        return jnp.concatenate(
            [lax.dot_general(q, k, dn, preferred_element_type=F32) for k in ks], axis=1) * scale

    lfr = lfr_ref[...]
    parts = [lfr[0:H_B]]
    for t in range(1, n_t):
        parts.append(parts[-1] + lfr[t * H_B:(t + 1) * H_B])
    cn_col = jnp.concatenate(parts, axis=0)

    group = n_groups - 1 - j
    ta = ta_ref[...]
    bias_a = jnp.concatenate(
        [ta + slope * ((group * n_pg + g).astype(F32) * float(PAGE_SIZE) - (p_len + t_row))
         for g in range(n_pg)], axis=1)
    attend_a(scores(qa, [r[...].astype(BF16) for r in ka_refs]) + bias_a,
             [r[...].astype(BF16) for r in va_refs])

    lane = lax.broadcasted_iota(jnp.int32, (H_B, PAGE_ROWS), 1)
    tail = carry_s[...]
    sufs = [None] * n_pg
    for g in range(n_pg - 1, -1, -1):
        lf = lf_refs[g][...]
        inc = lf
        shift = 8
        while shift < PAGE_ROWS:
            inc = inc + jnp.where(lane < PAGE_ROWS - shift,
                                  pltpu.roll(inc, PAGE_ROWS - shift, axis=1), 0.0)
            shift *= 2
        sufs[g] = (inc - lf) + tail
        tail = tail + inc[:, 0:1]
    carry_s[...] = tail
    suf = jnp.concatenate(sufs, axis=1)
    bias_b = (jnp.concatenate([suf] * n_t, axis=0) + cn_col
              + jnp.concatenate([tb_ref[...]] * n_pg, axis=1))
    attend_b(scores(qb, [r[...].astype(BF16) for r in kb_refs]) + bias_b,
             [r[...].astype(BF16) for r in vb_refs])

    @pl.when(j == n_groups - 1)
    def _():
        attend_a(scores(qa, [kan_ref[...]]) + tan_ref[...], [van_ref[...]])
        lfl = lfl_ref[...]
        ln = lax.broadcasted_iota(jnp.int32, lfl.shape, 1)
        cn_lane = lfl
        shift = H_B
        while shift < n_t * H_B:
            cn_lane = cn_lane + jnp.where(ln >= shift, pltpu.roll(cn_lane, shift, axis=1), 0.0)
            shift *= 2
        n_new = tbn_ref.shape[1]
        cn_keys = _tile_lanes(cn_lane, n_new // LANES)
        attend_b(scores(qb, [kbn_ref[...]]) + (tbn_ref[...] + (cn_col - cn_keys)), [vbn_ref[...]])

        lam = _diff_lambda_vec(lq1_ref, lk1_ref, lq2_ref, lk2_ref, lam_init)
        la = la_s[...]
        fa = acca_s[...] / jnp.concatenate([la, la], axis=0)
        half = n_ra // 2
        o = [fa[e * n_ra:e * n_ra + half] - lam * fa[e * n_ra + half:(e + 1) * n_ra]
             for e in range(2)]
        ms = (jnp.sum(o[0] * o[0], axis=1, keepdims=True)
              + jnp.sum(o[1] * o[1], axis=1, keepdims=True)) / (2.0 * HD_A)
        inv = lax.rsqrt(ms + EPS)
        for e in range(2):
            oa_ref[e] = o[e] * inv * gsub_ref[:, e * HD_A:(e + 1) * HD_A] * (1.0 - lam_init)
        ob_ref[...] = accb_s[...] / lb_s[...]


def _even_sample_attention(qa2, qb2, lf_rows, lf_lanes, new_kv, pools, page_table, pool_off,
                           lam_params, g_sub, lam_init):
    pool_ak, pool_av, pool_bk, pool_bv, pool_lf = pools
    bsz, n_ra, hd = qa2.shape
    n_rb = qb2.shape[1]
    n_t = n_rb // H_B
    n_new = new_kv[0].shape[1]
    n_pages = page_table.shape[1]
    n_pg = _pick(n_pages, (4, 2, 1))
    n_groups = n_pages // n_pg
    tables = _even_sample_tables(n_t, n_new)

    def page_map(g):
        def index(b, j, pt):
            return (pool_off + pt[b * n_pages + (n_groups - 1 - j) * n_pg + g], 0, 0)
        return index

    per_b = lambda b, j, pt: (b, 0, 0)
    const2 = lambda b, j, pt: (0, 0)
    const3 = lambda b, j, pt: (0, 0, 0)
    vec = pl.BlockSpec((1, HD_A), const2)
    in_specs = [pl.BlockSpec((None, n_ra, hd), per_b),
                pl.BlockSpec((None, n_rb, hd), per_b),
                pl.BlockSpec((None, n_rb, 1), per_b),
                pl.BlockSpec((None, 1, LANES), per_b)]
    in_specs += [pl.BlockSpec((None, n_new, hd), per_b)] * 4
    in_specs += [pl.BlockSpec(tables[0].shape, const2), pl.BlockSpec(tables[1].shape, const2),
                 pl.BlockSpec(tables[2].shape, const2), pl.BlockSpec(tables[3].shape, const2),
                 pl.BlockSpec(tables[4].shape, const2), pl.BlockSpec(tables[5].shape, const3),
                 pl.BlockSpec(tables[6].shape, const2)]
    in_specs += [vec, vec, vec, vec, pl.BlockSpec((1, 2 * HD_A), const2)]
    args = [qa2, qb2, lf_rows, lf_lanes, *new_kv, *tables, *lam_params,
            g_sub.reshape(1, 2 * HD_A).astype(F32)]
    for pool in (pool_ak, pool_av, pool_bk, pool_bv):
        for g in range(n_pg):
            in_specs.append(pl.BlockSpec((None, PAGE_ROWS, hd), page_map(g)))
            args.append(pool)
    for g in range(n_pg):
        in_specs.append(pl.BlockSpec((None, H_B, PAGE_ROWS), page_map(g)))
        args.append(pool_lf)
    grid_spec = pltpu.PrefetchScalarGridSpec(
        num_scalar_prefetch=1,
        grid=(bsz, n_groups),
        in_specs=in_specs,
        out_specs=[pl.BlockSpec((None, 2, n_ra // 2, hd), lambda b, j, pt: (b, 0, 0, 0)),
                   pl.BlockSpec((None, n_rb, hd), per_b)],
        scratch_shapes=[pltpu.VMEM((n_ra, 1), F32), pltpu.VMEM((n_ra, 1), F32),
                        pltpu.VMEM((2 * n_ra, hd), F32),
                        pltpu.VMEM((n_rb, 1), F32), pltpu.VMEM((n_rb, 1), F32),
                        pltpu.VMEM((n_rb, hd), F32),
                        pltpu.VMEM((H_B, 1), F32)])
    return pl.pallas_call(
        functools.partial(_even_sample_kernel, n_pg=n_pg, n_t=n_t,
                          p_len=float(n_pages * PAGE_SIZE), scale=HD_A ** -0.5,
                          lam_init=lam_init),
        grid_spec=grid_spec,
        out_shape=[jax.ShapeDtypeStruct((bsz, 2, n_ra // 2, hd), F32),
                   jax.ShapeDtypeStruct((bsz, n_rb, hd), F32)],
        compiler_params=_params("parallel", "arbitrary"),
        name="even_sample_attention",
    )(page_table.reshape(-1), *args)


def _pattern_count(dist):
    return sum(((dist >= 0) & (dist <= w) & (dist % d == 0)).astype(jnp.int32)
               for w, d in C_PATTERNS)


def _count_bias(dist, slopes, same_head=True):
    cnt = _pattern_count(dist)
    logc = jnp.where((cnt > 0) & same_head, jnp.log(jnp.maximum(cnt, 1).astype(F32)), NEG_INF)
    return logc - slopes * dist.astype(F32)


def _dilated_bias(t, n_off):
    idx = jnp.arange(t, dtype=jnp.int32)
    dist = (idx[None, :, None] - idx[None, None, :]
            + t * jnp.arange(n_off, dtype=jnp.int32)[:, None, None])
    return _count_bias(dist[None], _alibi_slopes(H_C)[:, None, None, None]) * LOG2E


def _dil_kernel(q_ref, k_ref, v_ref, bias_ref, o_ref, m_s, l_s, acc_s, *, t, n_off):
    i = pl.program_id(1)
    q = q_ref[...]
    m_s[...] = jnp.full_like(m_s, NEG_INF)
    l_s[...] = jnp.zeros_like(l_s)
    acc_s[...] = jnp.zeros_like(acc_s)
    dn = (((1,), (1,)), ((), ()))
    no_shift = jnp.zeros((t, LANES), F32)

    def scores(off):
        k = k_ref[pl.ds(pl.multiple_of((i - off) * t, t), t), :]
        return lax.dot_general(q, k, dn, preferred_element_type=F32) + bias_ref[off]

    def update(u, off):
        v = v_ref[pl.ds(pl.multiple_of((i - off) * t, t), t), :]
        _online_update(u, v, m_s, l_s, acc_s, no_shift)

    @pl.when(i >= n_off - 1)
    def _():
        us = [scores(off) for off in range(n_off)]
        for off in range(n_off):
            update(us[off], off)

    @pl.when(i < n_off - 1)
    def _():
        for off in range(n_off - 1):
            @pl.when(i >= off)
            def _(off=off):
                update(scores(off), off)

    o_ref[...] = (acc_s[...] / l_s[...]).astype(o_ref.dtype)


def _dilated_attention(q, k, v):
    s_len = q.shape[0]
    t = _pick(s_len, (512, 256, 128))
    n_off = min(C_WMAX // t + 1, s_len // t)
    bias = _dilated_bias(t, n_off)
    return pl.pallas_call(
        functools.partial(_dil_kernel, t=t, n_off=n_off),
        grid=(H_C, s_len // t),
        in_specs=[pl.BlockSpec((t, HD_C), lambda h, i: (i, h)),
                  pl.BlockSpec((s_len, HD_C), lambda h, i: (0, h)),
                  pl.BlockSpec((s_len, HD_C), lambda h, i: (0, h)),
                  pl.BlockSpec((None, n_off, t, t), lambda h, i: (h, 0, 0, 0))],
        out_specs=pl.BlockSpec((t, HD_C), lambda h, i: (i, h)),
        out_shape=jax.ShapeDtypeStruct((s_len, H_C * HD_C), BF16),
        scratch_shapes=[pltpu.VMEM((t, LANES), F32), pltpu.VMEM((t, LANES), F32),
                        pltpu.VMEM((t, HD_C), F32)],
        compiler_params=_params("parallel", "parallel"),
        name="dilated_attention",
    )(q, k, v, bias)


def _c_sample_bias(buf_len, n_t, n_new_cols):
    slopes = _alibi_slopes(H_C)[None, :, None, None]
    t = jnp.arange(n_t, dtype=jnp.int32)[:, None, None, None]
    heads = jnp.arange(H_C, dtype=jnp.int32)
    same = heads[None, :, None, None] == heads[None, None, None, :]

    def table(pos):
        dist = buf_len + t - pos[None, None, :, None]
        return _count_bias(dist, slopes, same).reshape(n_t * H_C, pos.shape[0] * H_C)

    past = table(jnp.arange(buf_len, dtype=jnp.int32))
    new = table(buf_len + jnp.arange(n_t, dtype=jnp.int32))
    new = jnp.pad(new, ((0, 0), (0, n_new_cols - new.shape[1])), constant_values=NEG_INF)
    return past, new


def _c_sample_kernel(q_ref, kc_ref, kx_ref, kn_ref, vc_ref, vx_ref, vn_ref, bias_ref, biasn_ref,
                     ok_ref, ov_ref, o_ref, m_s, l_s, acc_s, *, rows, new, scale):
    c = pl.program_id(1)
    last = pl.num_programs(1) - 1
    dn = (((1,), (1,)), ((), ()))
    q = q_ref[...]

    @pl.when(c == 0)
    def _():
        m_s[...] = jnp.full_like(m_s, NEG_INF)
        l_s[...] = jnp.zeros_like(l_s)
        acc_s[...] = jnp.zeros_like(acc_s)

    for src, nxt, fresh, dst in ((kc_ref, kx_ref, kn_ref, ok_ref), (vc_ref, vx_ref, vn_ref, ov_ref)):
        dst[pl.ds(0, rows - new), :] = src[pl.ds(new, rows - new), :]

        @pl.when(c < last)
        def _(nxt=nxt, dst=dst):
            dst[pl.ds(rows - new, new), :] = nxt[...]

        @pl.when(c == last)
        def _(fresh=fresh, dst=dst):
            dst[pl.ds(rows - new, new), :] = fresh[...]

    def attend(s, v):
        m_prev = m_s[...]
        m_new = jnp.maximum(m_prev, jnp.max(s, axis=1, keepdims=True))
        alpha = jnp.exp(m_prev - m_new)
        p = jnp.exp(s - m_new)
        l_s[...] = alpha * l_s[...] + jnp.sum(p, axis=1, keepdims=True)
        acc_s[...] = alpha * acc_s[...] + jnp.dot(p.astype(BF16), v, preferred_element_type=F32)
        m_s[...] = m_new

    s = lax.dot_general(q, kc_ref[...].astype(BF16), dn, preferred_element_type=F32)
    attend(s * scale + bias_ref[...], vc_ref[...].astype(BF16))

    @pl.when(c == last)
    def _():
        pad = jnp.zeros((biasn_ref.shape[1] - new, kn_ref.shape[1]), F32)
        kn = jnp.concatenate([kn_ref[...], pad], axis=0).astype(BF16)
        vn = jnp.concatenate([vn_ref[...], pad], axis=0).astype(BF16)
        s_n = lax.dot_general(q, kn, dn, preferred_element_type=F32)
        attend(s_n * scale + biasn_ref[...], vn)
        o_ref[...] = acc_s[...] / l_s[...]


def _c_sample(q2, cache_k, cache_v, k_new, v_new, batch_off, n_t):
    bsz, n_r, hd = q2.shape
    total = cache_k.shape[1]
    new = k_new.shape[1]
    buf_len = total // H_C
    rows = _pick(total, (4096, 2048, 1024))
    assert rows % new == 0 and total % rows == 0
    n_chunks = total // rows
    n_new_cols = max(LANES, new)
    bias, bias_n = _c_sample_bias(buf_len, n_t, n_new_cols)
    blocks_per_chunk = rows // new
    n_small = total // new

    cur = pl.BlockSpec((None, rows, hd), lambda b, c: (batch_off + b, c, 0))
    nxt = pl.BlockSpec((None, new, hd),
                       lambda b, c: (batch_off + b, jnp.minimum((c + 1) * blocks_per_chunk,
                                                                n_small - 1), 0))
    per_b = pl.BlockSpec((None, new, hd), lambda b, c: (b, 0, 0))
    out_buf = pl.BlockSpec((None, rows, hd), lambda b, c: (b, c, 0))
    return pl.pallas_call(
        functools.partial(_c_sample_kernel, rows=rows, new=new, scale=HD_C ** -0.5),
        grid=(bsz, n_chunks),
        in_specs=[pl.BlockSpec((None, n_r, hd), lambda b, c: (b, 0, 0)),
                  cur, nxt, per_b, cur, nxt, per_b,
                  pl.BlockSpec((n_r, rows), lambda b, c: (0, c)),
                  pl.BlockSpec((n_r, n_new_cols), lambda b, c: (0, 0))],
        out_specs=[out_buf, out_buf, pl.BlockSpec((None, n_r, hd), lambda b, c: (b, 0, 0))],
        out_shape=[jax.ShapeDtypeStruct((bsz, total, hd), F32),
                   jax.ShapeDtypeStruct((bsz, total, hd), F32),
                   jax.ShapeDtypeStruct((bsz, n_r, hd), F32)],
        scratch_shapes=[pltpu.VMEM((n_r, 1), F32), pltpu.VMEM((n_r, 1), F32),
                        pltpu.VMEM((n_r, hd), F32)],
        compiler_params=_params("parallel", "arbitrary"),
        name="window_sample_attention",
    )(q2, cache_k, cache_k, k_new, cache_v, cache_v, v_new, bias, bias_n)


def _cross_kernel(q_ref, k_ref, v_ref, o_ref, *, scale):
    dn = (((1,), (1,)), ((), ()))
    for h in range(H_X):
        cols = slice(h * HD_X, (h + 1) * HD_X)
        s = lax.dot_general(q_ref[:, cols], k_ref[:, cols].astype(BF16), dn,
                            preferred_element_type=F32) * scale
        m = jnp.max(s, axis=1, keepdims=True)
        e = jnp.exp(s - m)
        l = jnp.sum(e, axis=1, keepdims=True)
        o = jnp.dot(e.astype(BF16), v_ref[:, cols].astype(BF16), preferred_element_type=F32)
        o_ref[:, cols] = (o / l).astype(o_ref.dtype)


def _cross_attention(q, mem_k, mem_v):
    bsz, n_q, w = q.shape
    n_m = mem_k.shape[1]
    tq = _pick(n_q, (512, 256, 128))
    return pl.pallas_call(
        functools.partial(_cross_kernel, scale=HD_X ** -0.5),
        grid=(bsz, n_q // tq),
        in_specs=[pl.BlockSpec((None, tq, w), lambda b, i: (b, i, 0)),
                  pl.BlockSpec((None, n_m, w), lambda b, i: (b, 0, 0)),
                  pl.BlockSpec((None, n_m, w), lambda b, i: (b, 0, 0))],
        out_specs=pl.BlockSpec((None, tq, w), lambda b, i: (b, i, 0)),
        out_shape=jax.ShapeDtypeStruct((bsz, n_q, w), BF16),
        compiler_params=_params("parallel", "parallel"),
        name="cross_attention",
    )(q, mem_k, mem_v)


SAMPLE_ROWS = 16
NEW_KEY_ROWS = MXU_TILE


def _pad_rows(x, rows):
    return jnp.pad(x, ((0, 0), (0, rows - x.shape[1]), (0, 0)))


def kernel(x_prompt, x_sample, mem_prompt, cache_a_k, cache_a_v, cache_b_k, cache_b_v, cache_b_logf, cache_c_k, cache_c_v, cache_mem_k, cache_mem_v, page_table, norm_mix, norm_cross, norm_mem, norm_ffn, norm_final, w_in_even, b_forget, lambda_q1, lambda_k1, lambda_q2, lambda_k2, subln_a, w_out_even, w_in_odd, w_out_odd, w_xq, w_xkv, w_xo, w_gate_up, w_down):
    n_b, s_len, d_model = x_prompt.shape
    d_b, d_t, _ = x_sample.shape
    assert n_b == 1
    depth = norm_mix.shape[0]
    n_mem = mem_prompt.shape[1]
    xp = x_prompt.reshape(s_len, d_model)
    xs = x_sample.reshape(d_b * d_t, d_model)
    mem = mem_prompt.reshape(n_mem, d_model)
    wa = H_A * 2 * HD_A
    wb = H_B * HD_B
    wc = H_C * HD_C
    wx = H_X * HD_X

    n_even, n_pool = cache_a_k.shape[:2]
    pool_ak = cache_a_k.reshape(n_even * n_pool, PAGE_ROWS, HD_A)
    pool_av = (cache_a_v.reshape(n_even, n_pool, PAGE_SIZE, H_A, 2, HD_A)
               .transpose(0, 1, 2, 4, 3, 5).reshape(n_even * n_pool, PAGE_ROWS, HD_A))
    pool_bk = cache_b_k.reshape(n_even * n_pool, PAGE_ROWS, HD_B)
    pool_bv = cache_b_v.reshape(n_even * n_pool, PAGE_ROWS, HD_B)
    pool_lf = jnp.repeat(jnp.swapaxes(cache_b_logf, 2, 3), H_B, axis=3).reshape(
        n_even * n_pool, H_B, PAGE_ROWS)
    n_odd, _, buf_len = cache_c_k.shape[:3]
    win_k = cache_c_k.reshape(n_odd * d_b, buf_len * H_C, HD_C)
    win_v = cache_c_v.reshape(n_odd * d_b, buf_len * H_C, HD_C)

    ak_p, av_p, bk_p, bv_p, bl_p, ck_p, cv_p, mk_p, mv_p = [], [], [], [], [], [], [], [], []
    ak_s, av_s, bk_s, bv_s, bl_s, ck_s, cv_s = [], [], [], [], [], [], []

    for l in range(depth):
        j = l // 2
        xnp = _rmsnorm(xp, norm_mix[l], BF16)
        xns = _rmsnorm(xs, norm_mix[l], BF16)
        if l % 2 == 0:
            lam_init = 0.8 - 0.6 * math.exp(-0.3 * l)
            w_in = w_in_even[j]
            w_cols = [w_in[:, o:o + wa].astype(BF16) for o in range(0, 6 * wa, wa)]
            w_fz = jnp.pad(w_in[:, 6 * wa:], ((0, 0), (0, LANES - H_B))).astype(BF16)
            lam_params = [p[j].reshape(1, HD_A).astype(F32)
                          for p in (lambda_q1, lambda_k1, lambda_q2, lambda_k2)]
            w_out = w_out_even[j].astype(BF16)

            qa = _matmul([(xnp, w_cols[0])], out_f32=False, out_bf16=True,
                         bf16_scale=HD_A ** -0.5 * LOG2E)
            ka, ka16 = _matmul([(xnp, w_cols[1])], out_bf16=True)
            va, va16 = _matmul([(xnp, w_cols[2])], out_bf16=True)
            qb = _matmul([(xnp, w_cols[3])], out_f32=False, out_bf16=True,
                         bf16_scale=HD_B ** -0.5 * LOG2E)
            kb, kb16 = _matmul([(xnp, w_cols[4])], out_bf16=True)
            vb, vb16 = _matmul([(xnp, w_cols[5])], out_bf16=True)
            fz = _matmul([(xnp, w_fz)])[:, :H_B]
            lf_t, c_t = _logf_cumsum(fz.T, b_forget[j])
            za = _diff_attention(qa, ka16, va16, lam_params, subln_a[j], lam_init)
            zb = _fox_attention(qb, kb16, vb16, c_t)
            xp = _matmul([(za, w_out[:wa]), (zb, w_out[wa:])], res=xp)
            ak_p.append(ka.reshape(1, s_len, H_A, 2, HD_A))
            av_p.append(va.reshape(1, s_len, H_A, 2 * HD_A))
            bk_p.append(kb.reshape(1, s_len, H_B, HD_B))
            bv_p.append(vb.reshape(1, s_len, H_B, HD_B))
            bl_p.append(lf_t.T.reshape(1, s_len, H_B))

            qa_s = _matmul([(xns, w_cols[0])], out_f32=False, out_bf16=True)
            ka_s, ka_s16 = _matmul([(xns, w_cols[1])], out_bf16=True)
            va_s, va_s16 = _matmul([(xns, w_cols[2])], out_bf16=True)
            qb_s = _matmul([(xns, w_cols[3])], out_f32=False, out_bf16=True)
            kb_s, kb_s16 = _matmul([(xns, w_cols[4])], out_bf16=True)
            vb_s, vb_s16 = _matmul([(xns, w_cols[5])], out_bf16=True)
            fz_s = _matmul([(xns, w_fz)])[:, :H_B]
            lf_s_t, _ = _logf_cumsum(fz_s.T, b_forget[j])
            lf_s = lf_s_t.T
            qa2 = (qa_s.reshape(d_b, d_t, H_A, 2, HD_A).transpose(0, 3, 1, 2, 4)
                   .reshape(d_b, 2 * d_t * H_A, HD_A))
            qb2 = qb_s.reshape(d_b, d_t * H_B, HD_B)
            va_rows = (va_s16.reshape(d_b, d_t, H_A, 2, HD_A).transpose(0, 1, 3, 2, 4)
                       .reshape(d_b, d_t * 8, HD_A))
            new_kv = [_pad_rows(a, NEW_KEY_ROWS)
                      for a in (ka_s16.reshape(d_b, d_t * 8, HD_A), va_rows,
                                kb_s16.reshape(d_b, d_t * H_B, HD_B),
                                vb_s16.reshape(d_b, d_t * H_B, HD_B))]
            lf_rows = lf_s.reshape(d_b, d_t * H_B, 1)
            lf_lanes = _pad_rows(lf_rows, LANES).reshape(d_b, 1, LANES)
            oa_s, ob_s = _even_sample_attention(
                qa2, qb2, lf_rows, lf_lanes, new_kv,
                (pool_ak, pool_av, pool_bk, pool_bv, pool_lf), page_table, j * n_pool,
                lam_params, subln_a[j], lam_init)
            za_s = (oa_s.reshape(d_b, 2, d_t, H_A, HD_A).transpose(0, 2, 3, 1, 4)
                    .reshape(d_b * d_t, wa).astype(BF16))
            zb_s = ob_s.reshape(d_b * d_t, wb).astype(BF16)
            xs = _matmul([(za_s, w_out[:wa]), (zb_s, w_out[wa:])], res=xs)
            ak_s.append(ka_s.reshape(d_b, d_t, H_A, 2, HD_A))
            av_s.append(va_s.reshape(d_b, d_t, H_A, 2 * HD_A))
            bk_s.append(kb_s.reshape(d_b, d_t, H_B, HD_B))
            bv_s.append(vb_s.reshape(d_b, d_t, H_B, HD_B))
            bl_s.append(lf_s.reshape(d_b, d_t, H_B))
        else:
            w_in = w_in_odd[j]
            w_cols = [w_in[:, o:o + wc].astype(BF16) for o in range(0, 3 * wc, wc)]
            w_out = w_out_odd[j].astype(BF16)

            q16 = _matmul([(xnp, w_cols[0])], out_f32=False, out_bf16=True,
                          bf16_scale=HD_C ** -0.5 * LOG2E)
            k, k16 = _matmul([(xnp, w_cols[1])], out_bf16=True)
            v, v16 = _matmul([(xnp, w_cols[2])], out_bf16=True)
            o = _dilated_attention(q16, k16, v16)
            xp = _matmul([(o, w_out)], res=xp)
            keep = min(C_WMAX, s_len)
            ck_p.append(k[s_len - keep:].reshape(1, keep, H_C, HD_C))
            cv_p.append(v[s_len - keep:].reshape(1, keep, H_C, HD_C))

            q_s = _matmul([(xns, w_cols[0])], out_f32=False, out_bf16=True)
            k_s = _matmul([(xns, w_cols[1])])
            v_s = _matmul([(xns, w_cols[2])])
            new_k, new_v, o_s = _c_sample(
                q_s.reshape(d_b, d_t * H_C, HD_C), win_k, win_v,
                k_s.reshape(d_b, d_t * H_C, HD_C), v_s.reshape(d_b, d_t * H_C, HD_C),
                j * d_b, d_t)
            xs = _matmul([(o_s.reshape(d_b * d_t, wc).astype(BF16), w_out)], res=xs)
            ck_s.append(new_k.reshape(d_b, buf_len, H_C, HD_C))
            cv_s.append(new_v.reshape(d_b, buf_len, H_C, HD_C))

        w_q = w_xq[l].astype(BF16)
        w_kv = w_xkv[l].astype(BF16)
        w_o = w_xo[l].astype(BF16)
        memn = _rmsnorm(mem, norm_mem[l], BF16)
        mk = _matmul([(memn, w_kv[:, :wx])])
        mv = _matmul([(memn, w_kv[:, wx:])])
        mk_p.append(mk.reshape(1, n_mem, H_X, HD_X))
        mv_p.append(mv.reshape(1, n_mem, H_X, HD_X))
        qx = _matmul([(_rmsnorm(xp, norm_cross[l], BF16), w_q)], out_f32=False, out_bf16=True)
        ox = _cross_attention(qx.reshape(1, s_len, wx), mk.reshape(1, n_mem, wx),
                              mv.reshape(1, n_mem, wx))
        xp = _matmul([(ox.reshape(s_len, wx), w_o)], res=xp)
        qx_s = _matmul([(_rmsnorm(xs, norm_cross[l], BF16), w_q)], out_f32=False, out_bf16=True)
        ox_s = _cross_attention(_pad_rows(qx_s.reshape(d_b, d_t, wx), SAMPLE_ROWS),
                                cache_mem_k[l].reshape(d_b, n_mem, wx),
                                cache_mem_v[l].reshape(d_b, n_mem, wx))
        xs = _matmul([(ox_s[:, :d_t].reshape(d_b * d_t, wx), w_o)], res=xs)

        d_ff = w_down.shape[1]
        w_g = w_gate_up[l][:, :d_ff].astype(BF16)
        w_u = w_gate_up[l][:, d_ff:].astype(BF16)
        w_d = w_down[l].astype(BF16)
        xp = _ffn(_rmsnorm(xp, norm_ffn[l], BF16), w_g, w_u, w_d, xp)
        xs = _ffn(_rmsnorm(xs, norm_ffn[l], BF16), w_g, w_u, w_d, xs)

    y_prompt = _rmsnorm(xp, norm_final, F32).reshape(1, s_len, d_model)
    y_sample = _rmsnorm(xs, norm_final, F32).reshape(d_b, d_t, d_model)
    st = jnp.stack
    return (y_prompt, y_sample,
            st(ak_p), st(av_p), st(bk_p), st(bv_p), st(bl_p), st(ck_p), st(cv_p), st(mk_p), st(mv_p),
            st(ak_s), st(av_s), st(bk_s), st(bv_s), st(bl_s), st(ck_s), st(cv_s))
```

```python
import functools
import math

import numpy as np
import jax
import jax.numpy as jnp
from jax import lax
from jax.experimental import pallas as pl
from jax.experimental.pallas import tpu as pltpu

F32 = jnp.float32
BF16 = jnp.bfloat16

EPS = 1e-6
PAGE_SIZE = 128
H_A, HD_A = 4, 128
H_B, HD_B = 8, 128
H_C, HD_C = 16, 128
H_X, HD_X = 4, 128
C_PATTERNS = ((128, 1), (512, 4), (2048, 16))
C_WMAX = 2048
LANES = 128
MXU_TILE = 256
VMEM_LIMIT = 52 * 1024 * 1024
NEG_INF = float("-inf")
LOG2E = 1.4426950408889634


def _params(*sem):
    return pltpu.CompilerParams(dimension_semantics=sem, vmem_limit_bytes=VMEM_LIMIT)


def _tile_lanes(x, reps):
    return x if reps == 1 else jnp.concatenate([x] * reps, axis=1)


def _pick(n, pref):
    for t in pref:
        if n % t == 0:
            return t
    return n


def _rms_kernel(x_ref, g_ref, o_ref):
    x = x_ref[...]
    ms = jnp.mean(x * x, axis=-1, keepdims=True)
    o_ref[...] = ((x * lax.rsqrt(ms + EPS)) * g_ref[...]).astype(o_ref.dtype)


def _rmsnorm(x, g, out_dtype):
    m, d = x.shape
    tm = _pick(m, (512, 256, 128))
    return pl.pallas_call(
        _rms_kernel,
        grid=(m // tm,),
        in_specs=[pl.BlockSpec((tm, d), lambda i: (i, 0)),
                  pl.BlockSpec((1, d), lambda i: (0, 0))],
        out_specs=pl.BlockSpec((tm, d), lambda i: (i, 0)),
        out_shape=jax.ShapeDtypeStruct((m, d), out_dtype),
        compiler_params=_params("parallel"),
        name="rmsnorm",
    )(x, g.reshape(1, d).astype(F32))


def _mm_kernel(*refs, n_pairs):
    acc = refs[2 * n_pairs][...]
    for p in range(n_pairs):
        acc = acc + jnp.dot(refs[2 * p][...], refs[2 * p + 1][...], preferred_element_type=F32)
    refs[2 * n_pairs + 1][...] = acc


def _matmul_res(pairs, res):
    m = res.shape[0]
    n = res.shape[1]
    tm = _pick(m, (1024, 512, 256, 128))
    tn = _pick(n, (1024, 512, 256, 128))
    in_specs, args = [], []
    for a, w3, layer, row_blk in pairs:
        k = a.shape[1]
        in_specs += [pl.BlockSpec((tm, k), lambda i, j: (i, 0)),
                     pl.BlockSpec((None, k, tn),
                                  lambda i, j, layer=layer, row_blk=row_blk: (layer, row_blk, j))]
        args += [a, w3]
    in_specs.append(pl.BlockSpec((tm, tn), lambda i, j: (i, j)))
    return pl.pallas_call(
        functools.partial(_mm_kernel, n_pairs=len(pairs)),
        grid=(m // tm, n // tn),
        in_specs=in_specs,
        out_specs=pl.BlockSpec((tm, tn), lambda i, j: (i, j)),
        out_shape=jax.ShapeDtypeStruct((m, n), F32),
        compiler_params=_params("parallel", "parallel"),
        name="matmul_residual",
    )(*args, res)


def _proj_kernel(*refs, q_groups, q_scale, has_bf16, has_f32, has_gate):
    x_ref, gain_ref, w_ref = refs[:3]
    pos = 3
    wz_ref = ob_ref = of_ref = oz_ref = None
    if has_gate:
        wz_ref = refs[pos]
        pos += 1
    if has_bf16:
        ob_ref = refs[pos]
        pos += 1
    if has_f32:
        of_ref = refs[pos]
        pos += 1
    if has_gate:
        oz_ref = refs[pos]
        pos += 1
    xn_s = refs[pos]
    g = pl.program_id(1)

    @pl.when(g == 0)
    def _():
        x = x_ref[...]
        ms = jnp.mean(x * x, axis=-1, keepdims=True)
        xn_s[...] = ((x * lax.rsqrt(ms + EPS)) * gain_ref[...]).astype(BF16)
        if has_gate:
            oz_ref[...] = jnp.dot(xn_s[...], wz_ref[...], preferred_element_type=F32)

    y = jnp.dot(xn_s[...], w_ref[...], preferred_element_type=F32)
    is_q = None
    for qg in q_groups:
        is_q = (g == qg) if is_q is None else jnp.logical_or(is_q, g == qg)
    if has_bf16:
        ob_ref[...] = (y if is_q is None else y * jnp.where(is_q, q_scale, 1.0)).astype(BF16)
    if has_f32:
        if is_q is None:
            of_ref[...] = y
        else:
            @pl.when(jnp.logical_not(is_q))
            def _():
                of_ref[...] = y


def _norm_proj(x, gain, w3, layer, n_groups, gw, q_groups=(), q_scale=1.0,
               want_bf16=True, want_f32=True, w_gate=None):
    m, d = x.shape
    tm = _pick(m, (1024, 512, 256, 128))
    kv_groups = [g for g in range(n_groups) if g not in q_groups]
    assert not q_groups or max(q_groups) < max(kv_groups)

    def f32_slot(g):
        return sum(jnp.where(g > k, 1, 0) for k in kv_groups)

    in_specs = [pl.BlockSpec((tm, d), lambda i, g: (i, 0)),
                pl.BlockSpec((1, d), lambda i, g: (0, 0)),
                pl.BlockSpec((None, d, gw), lambda i, g: (layer, 0, g))]
    args = [x, gain.reshape(1, d).astype(F32), w3]
    if w_gate is not None:
        in_specs.append(pl.BlockSpec(w_gate.shape, lambda i, g: (0, 0)))
        args.append(w_gate)
    out_specs, out_shape = [], []
    if want_bf16:
        out_specs.append(pl.BlockSpec((None, tm, gw), lambda i, g: (g, i, 0)))
        out_shape.append(jax.ShapeDtypeStruct((n_groups, m, gw), BF16))
    if want_f32:
        out_specs.append(pl.BlockSpec((None, tm, gw), lambda i, g: (f32_slot(g), i, 0)))
        out_shape.append(jax.ShapeDtypeStruct((len(kv_groups), m, gw), F32))
    if w_gate is not None:
        out_specs.append(pl.BlockSpec((tm, w_gate.shape[1]), lambda i, g: (i, 0)))
        out_shape.append(jax.ShapeDtypeStruct((m, w_gate.shape[1]), F32))
    outs = pl.pallas_call(
        functools.partial(_proj_kernel, q_groups=tuple(q_groups), q_scale=q_scale,
                          has_bf16=want_bf16, has_f32=want_f32, has_gate=w_gate is not None),
        grid=(m // tm, n_groups),
        in_specs=in_specs, out_specs=out_specs, out_shape=out_shape,
        scratch_shapes=[pltpu.VMEM((tm, d), BF16)],
        compiler_params=_params("parallel", "arbitrary"),
        name="norm_projection",
    )(*args)
    return outs[0] if len(outs) == 1 else tuple(outs)


def _ffn_kernel(x_ref, gain_ref, wg_ref, wu_ref, wd_ref, o_ref, xn_s, acc_ref):
    f = pl.program_id(1)

    @pl.when(f == 0)
    def _():
        x = x_ref[...]
        ms = jnp.mean(x * x, axis=-1, keepdims=True)
        xn_s[...] = ((x * lax.rsqrt(ms + EPS)) * gain_ref[...]).astype(BF16)
        acc_ref[...] = jnp.zeros_like(acc_ref)

    xn = xn_s[...]
    g = jnp.dot(xn, wg_ref[...], preferred_element_type=F32)
    u = jnp.dot(xn, wu_ref[...], preferred_element_type=F32)
    h = (g * jax.nn.sigmoid(g)) * u
    acc_ref[...] += jnp.dot(h.astype(BF16), wd_ref[...], preferred_element_type=F32)

    @pl.when(f == pl.num_programs(1) - 1)
    def _():
        o_ref[...] = x_ref[...] + acc_ref[...]


def _ffn(x, gain, w_gate_up3, w_down3, layer):
    m, d = x.shape
    dff = w_down3.shape[1]
    tm = _pick(m, (512, 256, 128))
    tf = _pick(dff, (512, 256, 128))
    n_f = dff // tf
    return pl.pallas_call(
        _ffn_kernel,
        grid=(m // tm, n_f),
        in_specs=[pl.BlockSpec((tm, d), lambda i, f: (i, 0)),
                  pl.BlockSpec((1, d), lambda i, f: (0, 0)),
                  pl.BlockSpec((None, d, tf), lambda i, f: (layer, 0, f)),
                  pl.BlockSpec((None, d, tf), lambda i, f: (layer, 0, n_f + f)),
                  pl.BlockSpec((None, tf, d), lambda i, f: (layer, f, 0))],
        out_specs=pl.BlockSpec((tm, d), lambda i, f: (i, 0)),
        out_shape=jax.ShapeDtypeStruct((m, d), F32),
        scratch_shapes=[pltpu.VMEM((tm, d), BF16), pltpu.VMEM((tm, d), F32)],
        compiler_params=_params("parallel", "arbitrary"),
        name="swiglu",
    )(x, gain.reshape(1, d).astype(F32), w_gate_up3, w_gate_up3, w_down3)


def _log_sigmoid(x):
    return jnp.minimum(x, 0.0) - jnp.log1p(jnp.exp(-jnp.abs(x)))


def _logf_kernel(fz_ref, b_ref, lf_ref, c2_ref):
    lf = _log_sigmoid(fz_ref[...] + b_ref[...])
    lf_ref[...] = lf
    n = lf.shape[1]
    lane = lax.broadcasted_iota(jnp.int32, lf.shape, 1)
    c = lf
    shift = 1
    while shift < n:
        c = c + jnp.where(lane >= shift, pltpu.roll(c, shift, axis=1), 0.0)
        shift *= 2
    c2_ref[...] = c * LOG2E


def _logf_cumsum(fz_t, b_f):
    h, n = fz_t.shape
    return pl.pallas_call(
        _logf_kernel,
        out_shape=[jax.ShapeDtypeStruct((h, n), F32)] * 2,
        name="logf_cumsum",
    )(fz_t, b_f.reshape(h, 1).astype(F32))


def _online_update(u, v, m_s, l_s, acc_s, row_shift):
    reps = u.shape[1] // LANES
    m_prev = m_s[...]
    m_new = jnp.maximum(m_prev, jnp.max(u, axis=1, keepdims=True) + row_shift)
    alpha = jnp.exp2(m_prev - m_new)
    p = jnp.exp2(u - _tile_lanes(m_new - row_shift, reps))
    l_s[...] = alpha * l_s[...] + jnp.sum(p, axis=1, keepdims=True)
    acc_s[...] = (_tile_lanes(alpha, acc_s.shape[1] // LANES) * acc_s[...]
                  + jnp.dot(p.astype(BF16), v, preferred_element_type=F32))
    m_s[...] = m_new


def _causal_mask(u):
    row = lax.broadcasted_iota(jnp.int32, u.shape, 0)
    col = lax.broadcasted_iota(jnp.int32, u.shape, 1)
    return jnp.where(row >= col, u, NEG_INF)


def _causal_chunks(i, scores, update):
    def pair(jj, c):
        j0 = 2 * jj
        u0, u1 = scores(j0, False), scores(j0 + 1, False)
        update(u0, j0)
        update(u1, j0 + 1)
        return c

    lax.fori_loop(0, lax.div(i, 2), pair, 0)
    odd = lax.rem(i, 2) == 1

    @pl.when(odd)
    def _():
        u0, u1 = scores(i - 1, False), scores(i, True)
        update(u0, i - 1)
        update(u1, i)

    @pl.when(jnp.logical_not(odd))
    def _():
        update(scores(i, True), i)


def _fox_kernel(q_ref, k_ref, v_ref, ccol_ref, crow_ref, o_ref, m_s, l_s, acc_s, *, t):
    i = pl.program_id(1)
    q = q_ref[...]
    cq = jnp.broadcast_to(ccol_ref[...], (t, LANES))
    m_s[...] = jnp.full_like(m_s, NEG_INF)
    l_s[...] = jnp.zeros_like(l_s)
    acc_s[...] = jnp.zeros_like(acc_s)
    dn = (((1,), (1,)), ((), ()))

    def scores(j, masked):
        k = k_ref[pl.ds(pl.multiple_of(j * t, t), t), :]
        u = lax.dot_general(q, k, dn, preferred_element_type=F32) - crow_ref[j]
        return _causal_mask(u) if masked else u

    def update(u, j):
        v = v_ref[pl.ds(pl.multiple_of(j * t, t), t), :]
        _online_update(u, v, m_s, l_s, acc_s, cq)

    _causal_chunks(i, scores, update)
    o_ref[...] = (acc_s[...] / l_s[...]).astype(o_ref.dtype)


def _fox_attention(st, gq, gk, gv, c_t):
    s_len = st.shape[1]
    t = _pick(s_len, (512, 256, 128))
    nc = s_len // t
    c_col = c_t.reshape(H_B, s_len, 1)
    c_row = c_t.reshape(H_B, nc, 1, t)
    return pl.pallas_call(
        functools.partial(_fox_kernel, t=t),
        grid=(H_B, nc),
        in_specs=[pl.BlockSpec((None, t, HD_B), lambda h, i: (gq, i, h)),
                  pl.BlockSpec((None, s_len, HD_B), lambda h, i: (gk, 0, h)),
                  pl.BlockSpec((None, s_len, HD_B), lambda h, i: (gv, 0, h)),
                  pl.BlockSpec((None, t, 1), lambda h, i: (h, i, 0)),
                  pl.BlockSpec((None, nc, 1, t), lambda h, i: (h, 0, 0, 0))],
        out_specs=pl.BlockSpec((t, HD_B), lambda h, i: (i, h)),
        out_shape=jax.ShapeDtypeStruct((s_len, H_B * HD_B), BF16),
        scratch_shapes=[pltpu.VMEM((t, LANES), F32), pltpu.VMEM((t, LANES), F32),
                        pltpu.VMEM((t, HD_B), F32)],
        compiler_params=_params("parallel", "parallel"),
        name="forget_attention",
    )(st, st, st, c_col, c_row)


def _diff_lambda_vec(lq1_ref, lk1_ref, lq2_ref, lk2_ref, lam_init):
    a = jnp.sum(lq1_ref[...] * lk1_ref[...], axis=1, keepdims=True)
    b = jnp.sum(lq2_ref[...] * lk2_ref[...], axis=1, keepdims=True)
    return jnp.exp(a) - jnp.exp(b) + lam_init


def _diff_kernel(q_ref, k_ref, v_ref, slope_ref, lq1_ref, lk1_ref, lq2_ref, lk2_ref, g_ref,
                 o_ref, m1_s, l1_s, a1_s, m2_s, l2_s, a2_s, *, t, lam_init):
    i = pl.program_id(1)
    q1 = q_ref[:, :HD_A]
    q2 = q_ref[:, HD_A:]
    slope = slope_ref[...]
    row_pos = (i * t + lax.broadcasted_iota(jnp.int32, (t, LANES), 0)).astype(F32)
    row_shift = -(slope * row_pos)
    for m_s, l_s, a_s in ((m1_s, l1_s, a1_s), (m2_s, l2_s, a2_s)):
        m_s[...] = jnp.full_like(m_s, NEG_INF)
        l_s[...] = jnp.zeros_like(l_s)
        a_s[...] = jnp.zeros_like(a_s)
    reps = t // LANES
    dn = (((1,), (1,)), ((), ()))

    def scores(j, masked):
        k = k_ref[pl.ds(pl.multiple_of(j * t, t), t), :]
        col_pos = (j * t + lax.broadcasted_iota(jnp.int32, (1, t), 1)).astype(F32)
        col_term = _tile_lanes(slope, reps) * col_pos
        us = []
        for qm, km in ((q1, k[:, :HD_A]), (q2, k[:, HD_A:])):
            u = lax.dot_general(qm, km, dn, preferred_element_type=F32) + col_term
            us.append(_causal_mask(u) if masked else u)
        return us

    def update(us, j):
        v = v_ref[pl.ds(pl.multiple_of(j * t, t), t), :]
        _online_update(us[0], v, m1_s, l1_s, a1_s, row_shift)
        _online_update(us[1], v, m2_s, l2_s, a2_s, row_shift)

    _causal_chunks(i, scores, update)
    lam = _diff_lambda_vec(lq1_ref, lk1_ref, lq2_ref, lk2_ref, lam_init)
    o = (a1_s[...] / _tile_lanes(l1_s[...], 2)
         - lam * (a2_s[...] / _tile_lanes(l2_s[...], 2)))
    y = o * lax.rsqrt(jnp.mean(o * o, axis=1, keepdims=True) + EPS) * g_ref[...] * (1.0 - lam_init)
    o_ref[...] = y.astype(o_ref.dtype)


def _alibi_slopes(n):
    return jnp.asarray(2.0 ** (-8.0 * np.arange(1, n + 1) / n), dtype=F32)


def _diff_attention(st, gq, gk, gv, lam_params, g_sub, lam_init):
    s_len = st.shape[1]
    t = _pick(s_len, (512, 256, 128))
    nc = s_len // t
    w = 2 * HD_A
    slopes = jnp.broadcast_to((_alibi_slopes(H_A) * LOG2E)[:, None, None], (H_A, 1, LANES))
    vec = pl.BlockSpec((1, HD_A), lambda h, i: (0, 0))
    return pl.pallas_call(
        functools.partial(_diff_kernel, t=t, lam_init=lam_init),
        grid=(H_A, nc),
        in_specs=[pl.BlockSpec((None, t, w), lambda h, i: (gq, i, h)),
                  pl.BlockSpec((None, s_len, w), lambda h, i: (gk, 0, h)),
                  pl.BlockSpec((None, s_len, w), lambda h, i: (gv, 0, h)),
                  pl.BlockSpec((None, 1, LANES), lambda h, i: (h, 0, 0)),
                  vec, vec, vec, vec,
                  pl.BlockSpec((1, w), lambda h, i: (0, 0))],
        out_specs=pl.BlockSpec((t, w), lambda h, i: (i, h)),
        out_shape=jax.ShapeDtypeStruct((s_len, H_A * w), BF16),
        scratch_shapes=[pltpu.VMEM((t, LANES), F32), pltpu.VMEM((t, LANES), F32),
                        pltpu.VMEM((t, w), F32),
                        pltpu.VMEM((t, LANES), F32), pltpu.VMEM((t, LANES), F32),
                        pltpu.VMEM((t, w), F32)],
        compiler_params=_params("parallel", "parallel"),
        name="diff_attention",
    )(st, st, st, slopes, *lam_params, g_sub.reshape(1, w).astype(F32))


PAGE_ROWS = PAGE_SIZE * 8


def _even_sample_tables(n_t, n_new_cols):
    slopes = 2.0 ** (-8.0 * np.arange(1, H_A + 1) / H_A)
    ninf = -np.inf
    col = np.arange(PAGE_ROWS)
    p_col, j_col = col >> 3, col & 7
    ra = np.arange(2 * n_t * H_A)
    m_r, t_r, h_r = ra // (n_t * H_A), (ra // H_A) % n_t, ra % H_A
    j_r = 2 * h_r + m_r
    sl_r = slopes[h_r]
    ta = np.where(j_col[None, :] == j_r[:, None], sl_r[:, None] * p_col[None, :], ninf)
    rb = np.arange(n_t * H_B)
    tb_t, tb_h = rb // H_B, rb % H_B
    tb = np.where(j_col[None, :] == tb_h[:, None], 0.0, ninf)
    ncol = np.arange(n_new_cols)
    u_col, jn_col = ncol >> 3, ncol & 7
    ok_a = (jn_col[None, :] == j_r[:, None]) & (u_col[None, :] <= t_r[:, None])
    tan = np.where(ok_a, -sl_r[:, None] * (t_r[:, None] - u_col[None, :]), ninf)
    ok_b = (jn_col[None, :] == tb_h[:, None]) & (u_col[None, :] <= tb_t[:, None])
    tbn = np.where(ok_b, 0.0, ninf)
    rowa = np.stack([sl_r, t_r.astype(np.float64)], axis=1)
    lane = np.arange(MXU_TILE) & 7
    me = np.stack([(lane[None, :] == (4 * e + h_r)[:, None]).astype(np.float64) for e in range(2)])
    gi = np.arange(MXU_TILE) >> 3
    gt = (gi[:, None] == gi[None, :]).astype(np.float64)
    f = lambda a: jnp.asarray(a, F32)
    return f(ta), f(tb), f(tan), f(tbn), f(rowa), f(me), jnp.asarray(gt, BF16)


def _even_sample_kernel(pt_ref, qa_ref, qb_ref, lfr_ref, lfl_ref, kan_ref, van_ref, kbn_ref, vbn_ref,
                        ta_ref, tb_ref, tan_ref, tbn_ref, rowa_ref, me_ref, gt_ref,
                        lq1_ref, lk1_ref, lq2_ref, lk2_ref, gsub_ref, *rest,
                        n_pg, n_t, p_len, scale, lam_init):
    ka_refs = rest[0 * n_pg:1 * n_pg]
    va_refs = rest[1 * n_pg:2 * n_pg]
    kb_refs = rest[2 * n_pg:3 * n_pg]
    vb_refs = rest[3 * n_pg:4 * n_pg]
    lf_refs = rest[4 * n_pg:5 * n_pg]
    oa_ref, ob_ref = rest[5 * n_pg:5 * n_pg + 2]
    ma_s, la_s, acca_s, mb_s, lb_s, accb_s, carry_s = rest[5 * n_pg + 2:]
    j = pl.program_id(1)
    n_groups = pl.num_programs(1)
    n_ra = 2 * n_t * H_A
    dn = (((1,), (1,)), ((), ()))
    qa = qa_ref[...]
    qb = qb_ref[...]
    slope = rowa_ref[:, 0:1]
    t_row = rowa_ref[:, 1:2]
    gt = gt_ref[...]

    @pl.when(j == 0)
    def _():
        for m_s, l_s, a_s in ((ma_s, la_s, acca_s), (mb_s, lb_s, accb_s)):
            m_s[...] = jnp.full_like(m_s, NEG_INF)
            l_s[...] = jnp.zeros_like(l_s)
            a_s[...] = jnp.zeros_like(a_s)
        carry_s[...] = jnp.zeros_like(carry_s)

    def softmax_step(s, m_s, l_s):
        m_prev = m_s[...]
        m_new = jnp.maximum(m_prev, jnp.max(s, axis=1, keepdims=True))
        alpha = jnp.exp(m_prev - m_new)
        p = jnp.exp(s - m_new)
        l_s[...] = alpha * l_s[...] + jnp.sum(p, axis=1, keepdims=True)
        m_s[...] = m_new
        return p, alpha

    def spread(p):
        n_tiles = p.shape[1] // MXU_TILE
        pb = p.astype(BF16)
        stacked = jnp.concatenate(
            [pb[:, c * MXU_TILE:(c + 1) * MXU_TILE] for c in range(n_tiles)], axis=0)
        rep = jnp.dot(stacked, gt, preferred_element_type=F32)
        halves = []
        for e in range(2):
            me = me_ref[e]
            halves.append(jnp.concatenate(
                [rep[c * n_ra:(c + 1) * n_ra] * me for c in range(n_tiles)], axis=1))
        return jnp.concatenate(halves, axis=0).astype(BF16)

    def attend_a(s, vs):
        p, alpha = softmax_step(s, ma_s, la_s)
        p2 = spread(p)
        pv = None
        for g, v in enumerate(vs):
            rows = v.shape[0]
            y = jnp.dot(p2[:, g * rows:(g + 1) * rows], v, preferred_element_type=F32)
            pv = y if pv is None else pv + y
        acca_s[...] = jnp.concatenate([alpha, alpha], axis=0) * acca_s[...] + pv

    def attend_b(s, vs):
        p, alpha = softmax_step(s, mb_s, lb_s)
        pb = p.astype(BF16)
        pv = None
        for g, v in enumerate(vs):
            rows = v.shape[0]
            y = jnp.dot(pb[:, g * rows:(g + 1) * rows], v, preferred_element_type=F32)
            pv = y if pv is None else pv + y
        accb_s[...] = alpha * accb_s[...] + pv

    def scores(q, ks):
        return jnp.concatenate(
            [lax.dot_general(q, k, dn, preferred_element_type=F32) for k in ks], axis=1) * scale

    lfr = lfr_ref[...]
    parts = [lfr[0:H_B]]
    for t in range(1, n_t):
        parts.append(parts[-1] + lfr[t * H_B:(t + 1) * H_B])
    cn_col = jnp.concatenate(parts, axis=0)

    group = n_groups - 1 - j
    ta = ta_ref[...]
    bias_a = jnp.concatenate(
        [ta + slope * ((group * n_pg + g).astype(F32) * float(PAGE_SIZE) - (p_len + t_row))
         for g in range(n_pg)], axis=1)
    attend_a(scores(qa, [r[...].astype(BF16) for r in ka_refs]) + bias_a,
             [r[...].astype(BF16) for r in va_refs])

    lane = lax.broadcasted_iota(jnp.int32, (H_B, PAGE_ROWS), 1)
    tail = carry_s[...]
    sufs = [None] * n_pg
    for g in range(n_pg - 1, -1, -1):
        lf = lf_refs[g][...]
        inc = lf
        shift = 8
        while shift < PAGE_ROWS:
            inc = inc + jnp.where(lane < PAGE_ROWS - shift,
                                  pltpu.roll(inc, PAGE_ROWS - shift, axis=1), 0.0)
            shift *= 2
        sufs[g] = (inc - lf) + tail
        tail = tail + inc[:, 0:1]
    carry_s[...] = tail
    suf = jnp.concatenate(sufs, axis=1)
    bias_b = (jnp.concatenate([suf] * n_t, axis=0) + cn_col
              + jnp.concatenate([tb_ref[...]] * n_pg, axis=1))
    attend_b(scores(qb, [r[...].astype(BF16) for r in kb_refs]) + bias_b,
             [r[...].astype(BF16) for r in vb_refs])

    @pl.when(j == n_groups - 1)
    def _():
        attend_a(scores(qa, [kan_ref[...]]) + tan_ref[...], [van_ref[...]])
        lfl = lfl_ref[...]
        ln = lax.broadcasted_iota(jnp.int32, lfl.shape, 1)
        cn_lane = lfl
        shift = H_B
        while shift < n_t * H_B:
            cn_lane = cn_lane + jnp.where(ln >= shift, pltpu.roll(cn_lane, shift, axis=1), 0.0)
            shift *= 2
        n_new = tbn_ref.shape[1]
        cn_keys = _tile_lanes(cn_lane, n_new // LANES)
        attend_b(scores(qb, [kbn_ref[...]]) + (tbn_ref[...] + (cn_col - cn_keys)), [vbn_ref[...]])

        lam = _diff_lambda_vec(lq1_ref, lk1_ref, lq2_ref, lk2_ref, lam_init)
        la = la_s[...]
        fa = acca_s[...] / jnp.concatenate([la, la], axis=0)
        half = n_ra // 2
        o = [fa[e * n_ra:e * n_ra + half] - lam * fa[e * n_ra + half:(e + 1) * n_ra]
             for e in range(2)]
        ms = (jnp.sum(o[0] * o[0], axis=1, keepdims=True)
              + jnp.sum(o[1] * o[1], axis=1, keepdims=True)) / (2.0 * HD_A)
        inv = lax.rsqrt(ms + EPS)
        for e in range(2):
            oa_ref[e] = o[e] * inv * gsub_ref[:, e * HD_A:(e + 1) * HD_A] * (1.0 - lam_init)
        ob_ref[...] = accb_s[...] / lb_s[...]


def _even_sample_attention(qa2, qb2, lf_rows, lf_lanes, new_kv, pools, page_table, pool_off,
                           lam_params, g_sub, lam_init):
    pool_ak, pool_av, pool_bk, pool_bv, pool_lf = pools
    bsz, n_ra, hd = qa2.shape
    n_rb = qb2.shape[1]
    n_t = n_rb // H_B
    n_new = new_kv[0].shape[1]
    n_pages = page_table.shape[1]
    n_pg = _pick(n_pages, (8, 4, 2, 1))
    n_groups = n_pages // n_pg
    tables = _even_sample_tables(n_t, n_new)

    def page_map(g):
        def index(b, j, pt):
            return (pool_off + pt[b * n_pages + (n_groups - 1 - j) * n_pg + g], 0, 0)
        return index

    per_b = lambda b, j, pt: (b, 0, 0)
    const2 = lambda b, j, pt: (0, 0)
    const3 = lambda b, j, pt: (0, 0, 0)
    vec = pl.BlockSpec((1, HD_A), const2)
    in_specs = [pl.BlockSpec((None, n_ra, hd), per_b),
                pl.BlockSpec((None, n_rb, hd), per_b),
                pl.BlockSpec((None, n_rb, 1), per_b),
                pl.BlockSpec((None, 1, LANES), per_b)]
    in_specs += [pl.BlockSpec((None, n_new, hd), per_b)] * 4
    in_specs += [pl.BlockSpec(tables[0].shape, const2), pl.BlockSpec(tables[1].shape, const2),
                 pl.BlockSpec(tables[2].shape, const2), pl.BlockSpec(tables[3].shape, const2),
                 pl.BlockSpec(tables[4].shape, const2), pl.BlockSpec(tables[5].shape, const3),
                 pl.BlockSpec(tables[6].shape, const2)]
    in_specs += [vec, vec, vec, vec, pl.BlockSpec((1, 2 * HD_A), const2)]
    args = [qa2, qb2, lf_rows, lf_lanes, *new_kv, *tables, *lam_params,
            g_sub.reshape(1, 2 * HD_A).astype(F32)]
    for pool in (pool_ak, pool_av, pool_bk, pool_bv):
        for g in range(n_pg):
            in_specs.append(pl.BlockSpec((None, PAGE_ROWS, hd), page_map(g)))
            args.append(pool)
    for g in range(n_pg):
        in_specs.append(pl.BlockSpec((None, H_B, PAGE_ROWS), page_map(g)))
        args.append(pool_lf)
    grid_spec = pltpu.PrefetchScalarGridSpec(
        num_scalar_prefetch=1,
        grid=(bsz, n_groups),
        in_specs=in_specs,
        out_specs=[pl.BlockSpec((None, 2, n_ra // 2, hd), lambda b, j, pt: (b, 0, 0, 0)),
                   pl.BlockSpec((None, n_rb, hd), per_b)],
        scratch_shapes=[pltpu.VMEM((n_ra, 1), F32), pltpu.VMEM((n_ra, 1), F32),
                        pltpu.VMEM((2 * n_ra, hd), F32),
                        pltpu.VMEM((n_rb, 1), F32), pltpu.VMEM((n_rb, 1), F32),
                        pltpu.VMEM((n_rb, hd), F32),
                        pltpu.VMEM((H_B, 1), F32)])
    return pl.pallas_call(
        functools.partial(_even_sample_kernel, n_pg=n_pg, n_t=n_t,
                          p_len=float(n_pages * PAGE_SIZE), scale=HD_A ** -0.5,
                          lam_init=lam_init),
        grid_spec=grid_spec,
        out_shape=[jax.ShapeDtypeStruct((bsz, 2, n_ra // 2, hd), F32),
                   jax.ShapeDtypeStruct((bsz, n_rb, hd), F32)],
        compiler_params=_params("parallel", "arbitrary"),
        name="even_sample_attention",
    )(page_table.reshape(-1), *args)


def _pattern_count(dist):
    return sum(((dist >= 0) & (dist <= w) & (dist % d == 0)).astype(jnp.int32)
               for w, d in C_PATTERNS)


def _count_bias(dist, slopes, same_head=True):
    cnt = _pattern_count(dist)
    logc = jnp.where((cnt > 0) & same_head, jnp.log(jnp.maximum(cnt, 1).astype(F32)), NEG_INF)
    return logc - slopes * dist.astype(F32)


def _dilated_bias(t, n_off):
    idx = jnp.arange(t, dtype=jnp.int32)
    dist = (idx[None, :, None] - idx[None, None, :]
            + t * jnp.arange(n_off, dtype=jnp.int32)[:, None, None])
    return _count_bias(dist[None], _alibi_slopes(H_C)[:, None, None, None]) * LOG2E


def _dil_kernel(q_ref, k_ref, v_ref, bias_ref, o_ref, m_s, l_s, acc_s, *, t, n_off):
    i = pl.program_id(1)
    q = q_ref[...]
    m_s[...] = jnp.full_like(m_s, NEG_INF)
    l_s[...] = jnp.zeros_like(l_s)
    acc_s[...] = jnp.zeros_like(acc_s)
    dn = (((1,), (1,)), ((), ()))
    no_shift = jnp.zeros((t, LANES), F32)

    def scores(off):
        k = k_ref[pl.ds(pl.multiple_of((i - off) * t, t), t), :]
        return lax.dot_general(q, k, dn, preferred_element_type=F32) + bias_ref[off]

    def update(u, off):
        v = v_ref[pl.ds(pl.multiple_of((i - off) * t, t), t), :]
        _online_update(u, v, m_s, l_s, acc_s, no_shift)

    @pl.when(i >= n_off - 1)
    def _():
        us = [scores(off) for off in range(n_off)]
        for off in range(n_off):
            update(us[off], off)

    @pl.when(i < n_off - 1)
    def _():
        for off in range(n_off - 1):
            @pl.when(i >= off)
            def _(off=off):
                update(scores(off), off)

    o_ref[...] = (acc_s[...] / l_s[...]).astype(o_ref.dtype)


def _dilated_attention(st):
    s_len = st.shape[1]
    per = st.shape[2] // HD_C
    t = _pick(s_len, (512, 256, 128))
    n_off = min(C_WMAX // t + 1, s_len // t)
    bias = _dilated_bias(t, n_off)
    return pl.pallas_call(
        functools.partial(_dil_kernel, t=t, n_off=n_off),
        grid=(H_C, s_len // t),
        in_specs=[pl.BlockSpec((None, t, HD_C), lambda h, i: (h // per, i, h % per)),
                  pl.BlockSpec((None, s_len, HD_C), lambda h, i: (2 + h // per, 0, h % per)),
                  pl.BlockSpec((None, s_len, HD_C), lambda h, i: (4 + h // per, 0, h % per)),
                  pl.BlockSpec((None, n_off, t, t), lambda h, i: (h, 0, 0, 0))],
        out_specs=pl.BlockSpec((t, HD_C), lambda h, i: (i, h)),
        out_shape=jax.ShapeDtypeStruct((s_len, H_C * HD_C), BF16),
        scratch_shapes=[pltpu.VMEM((t, LANES), F32), pltpu.VMEM((t, LANES), F32),
                        pltpu.VMEM((t, HD_C), F32)],
        compiler_params=_params("parallel", "parallel"),
        name="dilated_attention",
    )(st, st, st, bias)


def _c_sample_bias(buf_len, n_t, n_new_cols):
    row = jnp.arange(n_t * H_C, dtype=jnp.int32)[:, None]
    t_row, h_row = row // H_C, row % H_C
    slopes = jnp.tile(_alibi_slopes(H_C), n_t)[:, None]

    def table(n_pos, pos0):
        col = jnp.arange(n_pos * H_C, dtype=jnp.int32)[None, :]
        dist = buf_len + t_row - (pos0 + col // H_C)
        return _count_bias(dist, slopes, h_row == col % H_C)

    past = table(buf_len, 0)
    new = table(n_t, buf_len)
    new = jnp.pad(new, ((0, 0), (0, n_new_cols - new.shape[1])), constant_values=NEG_INF)
    return past, new


def _c_sample_kernel(q_ref, kc_ref, kx_ref, kn_ref, vc_ref, vx_ref, vn_ref, bias_ref, biasn_ref,
                     ok_ref, ov_ref, o_ref, m_s, l_s, acc_s, *, rows, new, scale):
    c = pl.program_id(1)
    last = pl.num_programs(1) - 1
    dn = (((1,), (1,)), ((), ()))
    q = q_ref[...]

    @pl.when(c == 0)
    def _():
        m_s[...] = jnp.full_like(m_s, NEG_INF)
        l_s[...] = jnp.zeros_like(l_s)
        acc_s[...] = jnp.zeros_like(acc_s)

    for src, nxt, fresh, dst in ((kc_ref, kx_ref, kn_ref, ok_ref), (vc_ref, vx_ref, vn_ref, ov_ref)):
        dst[pl.ds(0, rows - new), :] = src[pl.ds(new, rows - new), :]

        @pl.when(c < last)
        def _(nxt=nxt, dst=dst):
            dst[pl.ds(rows - new, new), :] = nxt[...]

        @pl.when(c == last)
        def _(fresh=fresh, dst=dst):
            dst[pl.ds(rows - new, new), :] = fresh[...]

    def attend(s, v):
        m_prev = m_s[...]
        m_new = jnp.maximum(m_prev, jnp.max(s, axis=1, keepdims=True))
        alpha = jnp.exp(m_prev - m_new)
        p = jnp.exp(s - m_new)
        l_s[...] = alpha * l_s[...] + jnp.sum(p, axis=1, keepdims=True)
        acc_s[...] = alpha * acc_s[...] + jnp.dot(p.astype(BF16), v, preferred_element_type=F32)
        m_s[...] = m_new

    s = lax.dot_general(q, kc_ref[...].astype(BF16), dn, preferred_element_type=F32)
    attend(s * scale + bias_ref[...], vc_ref[...].astype(BF16))

    @pl.when(c == last)
    def _():
        pad = jnp.zeros((biasn_ref.shape[1] - new, kn_ref.shape[1]), F32)
        kn = jnp.concatenate([kn_ref[...], pad], axis=0).astype(BF16)
        vn = jnp.concatenate([vn_ref[...], pad], axis=0).astype(BF16)
        s_n = lax.dot_general(q, kn, dn, preferred_element_type=F32)
        attend(s_n * scale + biasn_ref[...], vn)
        o_ref[...] = acc_s[...] / l_s[...]


def _c_sample(q2, cache_k, cache_v, k_new, v_new, batch_off, n_t):
    bsz, n_r, hd = q2.shape
    total = cache_k.shape[1]
    new = k_new.shape[1]
    buf_len = total // H_C
    rows = _pick(total, (4096, 2048, 1024))
    assert rows % new == 0 and total % rows == 0
    n_chunks = total // rows
    n_new_cols = max(LANES, new)
    bias, bias_n = _c_sample_bias(buf_len, n_t, n_new_cols)
    blocks_per_chunk = rows // new
    n_small = total // new

    cur = pl.BlockSpec((None, rows, hd), lambda b, c: (batch_off + b, c, 0))
    nxt = pl.BlockSpec((None, new, hd),
                       lambda b, c: (batch_off + b, jnp.minimum((c + 1) * blocks_per_chunk,
                                                                n_small - 1), 0))
    per_b = pl.BlockSpec((None, new, hd), lambda b, c: (b, 0, 0))
    out_buf = pl.BlockSpec((None, rows, hd), lambda b, c: (b, c, 0))
    return pl.pallas_call(
        functools.partial(_c_sample_kernel, rows=rows, new=new, scale=HD_C ** -0.5),
        grid=(bsz, n_chunks),
        in_specs=[pl.BlockSpec((None, n_r, hd), lambda b, c: (b, 0, 0)),
                  cur, nxt, per_b, cur, nxt, per_b,
                  pl.BlockSpec((n_r, rows), lambda b, c: (0, c)),
                  pl.BlockSpec((n_r, n_new_cols), lambda b, c: (0, 0))],
        out_specs=[out_buf, out_buf, pl.BlockSpec((None, n_r, hd), lambda b, c: (b, 0, 0))],
        out_shape=[jax.ShapeDtypeStruct((bsz, total, hd), F32),
                   jax.ShapeDtypeStruct((bsz, total, hd), F32),
                   jax.ShapeDtypeStruct((bsz, n_r, hd), F32)],
        scratch_shapes=[pltpu.VMEM((n_r, 1), F32), pltpu.VMEM((n_r, 1), F32),
                        pltpu.VMEM((n_r, hd), F32)],
        compiler_params=_params("parallel", "arbitrary"),
        name="window_sample_attention",
    )(q2, cache_k, cache_k, k_new, cache_v, cache_v, v_new, bias, bias_n)


def _cross_kernel(q_ref, k_ref, v_ref, o_ref, *, scale):
    dn = (((1,), (1,)), ((), ()))
    for h in range(H_X):
        cols = slice(h * HD_X, (h + 1) * HD_X)
        s = lax.dot_general(q_ref[:, cols], k_ref[:, cols].astype(BF16), dn,
                            preferred_element_type=F32) * scale
        m = jnp.max(s, axis=1, keepdims=True)
        e = jnp.exp(s - m)
        l = jnp.sum(e, axis=1, keepdims=True)
        o = jnp.dot(e.astype(BF16), v_ref[:, cols].astype(BF16), preferred_element_type=F32)
        o_ref[:, cols] = (o / l).astype(o_ref.dtype)


def _cross_attention(q, mem_k, mem_v):
    bsz, n_q, w = q.shape
    n_m = mem_k.shape[1]
    tq = _pick(n_q, (512, 256, 128))
    return pl.pallas_call(
        functools.partial(_cross_kernel, scale=HD_X ** -0.5),
        grid=(bsz, n_q // tq),
        in_specs=[pl.BlockSpec((None, tq, w), lambda b, i: (b, i, 0)),
                  pl.BlockSpec((None, n_m, w), lambda b, i: (b, 0, 0)),
                  pl.BlockSpec((None, n_m, w), lambda b, i: (b, 0, 0))],
        out_specs=pl.BlockSpec((None, tq, w), lambda b, i: (b, i, 0)),
        out_shape=jax.ShapeDtypeStruct((bsz, n_q, w), BF16),
        compiler_params=_params("parallel", "parallel"),
        name="cross_attention",
    )(q, mem_k, mem_v)


SAMPLE_ROWS = 16
NEW_KEY_ROWS = MXU_TILE


def _pad_rows(x, rows):
    return jnp.pad(x, ((0, 0), (0, rows - x.shape[1]), (0, 0)))


def kernel(x_prompt, x_sample, mem_prompt, cache_a_k, cache_a_v, cache_b_k, cache_b_v, cache_b_logf, cache_c_k, cache_c_v, cache_mem_k, cache_mem_v, page_table, norm_mix, norm_cross, norm_mem, norm_ffn, norm_final, w_in_even, b_forget, lambda_q1, lambda_k1, lambda_q2, lambda_k2, subln_a, w_out_even, w_in_odd, w_out_odd, w_xq, w_xkv, w_xo, w_gate_up, w_down):
    n_b, s_len, d_model = x_prompt.shape
    d_b, d_t, _ = x_sample.shape
    assert n_b == 1
    depth = norm_mix.shape[0]
    n_mem = mem_prompt.shape[1]
    xp = x_prompt.reshape(s_len, d_model)
    xs = x_sample.reshape(d_b * d_t, d_model)
    mem = mem_prompt.reshape(n_mem, d_model)
    wa = H_A * 2 * HD_A
    wb = H_B * HD_B
    wc = H_C * HD_C
    wx = H_X * HD_X

    n_even, n_pool = cache_a_k.shape[:2]
    pool_ak = cache_a_k.reshape(n_even * n_pool, PAGE_ROWS, HD_A)
    pool_av = (cache_a_v.reshape(n_even, n_pool, PAGE_SIZE, H_A, 2, HD_A)
               .transpose(0, 1, 2, 4, 3, 5).reshape(n_even * n_pool, PAGE_ROWS, HD_A))
    pool_bk = cache_b_k.reshape(n_even * n_pool, PAGE_ROWS, HD_B)
    pool_bv = cache_b_v.reshape(n_even * n_pool, PAGE_ROWS, HD_B)
    pool_lf = jnp.repeat(jnp.swapaxes(cache_b_logf, 2, 3), H_B, axis=3).reshape(
        n_even * n_pool, H_B, PAGE_ROWS)
    n_odd, _, buf_len = cache_c_k.shape[:3]
    win_k = cache_c_k.reshape(n_odd * d_b, buf_len * H_C, HD_C)
    win_v = cache_c_v.reshape(n_odd * d_b, buf_len * H_C, HD_C)

    ak_p, av_p, bk_p, bv_p, bl_p, ck_p, cv_p, mk_p, mv_p = [], [], [], [], [], [], [], [], []
    ak_s, av_s, bk_s, bv_s, bl_s, ck_s, cv_s = [], [], [], [], [], [], []

    w_in_even16, w_out_even16 = w_in_even.astype(BF16), w_out_even.astype(BF16)
    w_in_odd16, w_out_odd16 = w_in_odd.astype(BF16), w_out_odd.astype(BF16)
    w_xq16, w_xkv16, w_xo16 = w_xq.astype(BF16), w_xkv.astype(BF16), w_xo.astype(BF16)
    w_gate_up16, w_down16 = w_gate_up.astype(BF16), w_down.astype(BF16)
    gw = 1024

    for l in range(depth):
        j = l // 2
        if l % 2 == 0:
            lam_init = 0.8 - 0.6 * math.exp(-0.3 * l)
            w_fz = jnp.pad(w_in_even[j][:, 6 * wa:], ((0, 0), (0, LANES - H_B))).astype(BF16)
            lam_params = [p[j].reshape(1, HD_A).astype(F32)
                          for p in (lambda_q1, lambda_k1, lambda_q2, lambda_k2)]
            assert wa == gw and wb == gw and HD_A == HD_B
            proj = functools.partial(_norm_proj, gain=norm_mix[l], w3=w_in_even16, layer=j,
                                     n_groups=6, gw=gw, q_groups=(0, 3), w_gate=w_fz)

            st16, st32, fz = proj(xp, q_scale=HD_A ** -0.5 * LOG2E)
            lf_t, c_t = _logf_cumsum(fz[:, :H_B].T, b_forget[j])
            za = _diff_attention(st16, 0, 1, 2, lam_params, subln_a[j], lam_init)
            zb = _fox_attention(st16, 3, 4, 5, c_t)
            xp = _matmul_res([(za, w_out_even16, j, 0), (zb, w_out_even16, j, 1)], xp)
            ak_p.append(st32[0].reshape(1, s_len, H_A, 2, HD_A))
            av_p.append(st32[1].reshape(1, s_len, H_A, 2 * HD_A))
            bk_p.append(st32[2].reshape(1, s_len, H_B, HD_B))
            bv_p.append(st32[3].reshape(1, s_len, H_B, HD_B))
            bl_p.append(lf_t.T.reshape(1, s_len, H_B))

            ss16, ss32, fz_s = proj(xs)
            qa_s, ka_s16, va_s16, qb_s, kb_s16, vb_s16 = (ss16[g] for g in range(6))
            ka_s, va_s, kb_s, vb_s = (ss32[g] for g in range(4))
            lf_s_t, _ = _logf_cumsum(fz_s[:, :H_B].T, b_forget[j])
            lf_s = lf_s_t.T
            qa2 = (qa_s.reshape(d_b, d_t, H_A, 2, HD_A).transpose(0, 3, 1, 2, 4)
                   .reshape(d_b, 2 * d_t * H_A, HD_A))
            qb2 = qb_s.reshape(d_b, d_t * H_B, HD_B)
            va_rows = (va_s16.reshape(d_b, d_t, H_A, 2, HD_A).transpose(0, 1, 3, 2, 4)
                       .reshape(d_b, d_t * 8, HD_A))
            new_kv = [_pad_rows(a, NEW_KEY_ROWS)
                      for a in (ka_s16.reshape(d_b, d_t * 8, HD_A), va_rows,
                                kb_s16.reshape(d_b, d_t * H_B, HD_B),
                                vb_s16.reshape(d_b, d_t * H_B, HD_B))]
            lf_rows = lf_s.reshape(d_b, d_t * H_B, 1)
            lf_lanes = _pad_rows(lf_rows, LANES).reshape(d_b, 1, LANES)
            oa_s, ob_s = _even_sample_attention(
                qa2, qb2, lf_rows, lf_lanes, new_kv,
                (pool_ak, pool_av, pool_bk, pool_bv, pool_lf), page_table, j * n_pool,
                lam_params, subln_a[j], lam_init)
            za_s = (oa_s.reshape(d_b, 2, d_t, H_A, HD_A).transpose(0, 2, 3, 1, 4)
                    .reshape(d_b * d_t, wa).astype(BF16))
            zb_s = ob_s.reshape(d_b * d_t, wb).astype(BF16)
            xs = _matmul_res([(za_s, w_out_even16, j, 0), (zb_s, w_out_even16, j, 1)], xs)
            ak_s.append(ka_s.reshape(d_b, d_t, H_A, 2, HD_A))
            av_s.append(va_s.reshape(d_b, d_t, H_A, 2 * HD_A))
            bk_s.append(kb_s.reshape(d_b, d_t, H_B, HD_B))
            bv_s.append(vb_s.reshape(d_b, d_t, H_B, HD_B))
            bl_s.append(lf_s.reshape(d_b, d_t, H_B))
        else:
            assert wc == 2 * gw
            proj = functools.partial(_norm_proj, gain=norm_mix[l], w3=w_in_odd16, layer=j,
                                     n_groups=6, gw=gw, q_groups=(0, 1))
            halves = lambda s32, a: jnp.concatenate([s32[a], s32[a + 1]], axis=1)

            st16, st32 = proj(xp, q_scale=HD_C ** -0.5 * LOG2E)
            o = _dilated_attention(st16)
            xp = _matmul_res([(o, w_out_odd16, j, 0)], xp)
            keep = min(C_WMAX, s_len)
            ck_p.append(halves(st32[:, s_len - keep:], 0).reshape(1, keep, H_C, HD_C))
            cv_p.append(halves(st32[:, s_len - keep:], 2).reshape(1, keep, H_C, HD_C))

            ss16, ss32 = proj(xs)
            q_s = jnp.concatenate([ss16[0], ss16[1]], axis=1)
            k_s, v_s = halves(ss32, 0), halves(ss32, 2)
            new_k, new_v, o_s = _c_sample(
                q_s.reshape(d_b, d_t * H_C, HD_C), win_k, win_v,
                k_s.reshape(d_b, d_t * H_C, HD_C), v_s.reshape(d_b, d_t * H_C, HD_C),
                j * d_b, d_t)
            xs = _matmul_res([(o_s.reshape(d_b * d_t, wc).astype(BF16), w_out_odd16, j, 0)], xs)
            ck_s.append(new_k.reshape(d_b, buf_len, H_C, HD_C))
            cv_s.append(new_v.reshape(d_b, buf_len, H_C, HD_C))

        mem_kv = _norm_proj(mem, norm_mem[l], w_xkv16, l, 2, wx, want_bf16=False)
        mk_p.append(mem_kv[0].reshape(1, n_mem, H_X, HD_X))
        mv_p.append(mem_kv[1].reshape(1, n_mem, H_X, HD_X))
        qx = _norm_proj(xp, norm_cross[l], w_xq16, l, 1, wx, want_f32=False)
        ox = _cross_attention(qx, mem_kv[0:1], mem_kv[1:2])
        xp = _matmul_res([(ox.reshape(s_len, wx), w_xo16, l, 0)], xp)
        qx_s = _norm_proj(xs, norm_cross[l], w_xq16, l, 1, wx, want_f32=False)
        ox_s = _cross_attention(_pad_rows(qx_s.reshape(d_b, d_t, wx), SAMPLE_ROWS),
                                cache_mem_k[l].reshape(d_b, n_mem, wx),
                                cache_mem_v[l].reshape(d_b, n_mem, wx))
        xs = _matmul_res([(ox_s[:, :d_t].reshape(d_b * d_t, wx), w_xo16, l, 0)], xs)

        xp = _ffn(xp, norm_ffn[l], w_gate_up16, w_down16, l)
        xs = _ffn(xs, norm_ffn[l], w_gate_up16, w_down16, l)

    y_prompt = _rmsnorm(xp, norm_final, F32).reshape(1, s_len, d_model)
    y_sample = _rmsnorm(xs, norm_final, F32).reshape(d_b, d_t, d_model)
    st = jnp.stack
    return (y_prompt, y_sample,
            st(ak_p), st(av_p), st(bk_p), st(bv_p), st(bl_p), st(ck_p), st(cv_p), st(mk_p), st(mv_p),
            st(ak_s), st(av_s), st(bk_s), st(bv_s), st(bl_s), st(ck_s), st(cv_s))
```

```python
import functools
import math

import numpy as np
import jax
import jax.numpy as jnp
from jax import lax
from jax.experimental import pallas as pl
from jax.experimental.pallas import tpu as pltpu

F32 = jnp.float32
BF16 = jnp.bfloat16

EPS = 1e-6
PAGE_SIZE = 128
H_A, HD_A = 4, 128
H_B, HD_B = 8, 128
H_C, HD_C = 16, 128
H_X, HD_X = 4, 128
C_PATTERNS = ((128, 1), (512, 4), (2048, 16))
C_WMAX = 2048
LANES = 128
MXU_TILE = 256
VMEM_LIMIT = 52 * 1024 * 1024
NEG_INF = float("-inf")
LOG2E = 1.4426950408889634


def _params(*sem):
    return pltpu.CompilerParams(dimension_semantics=sem, vmem_limit_bytes=VMEM_LIMIT)


def _tile_lanes(x, reps):
    return x if reps == 1 else jnp.concatenate([x] * reps, axis=1)


def _pick(n, pref):
    for t in pref:
        if n % t == 0:
            return t
    return n


def _mm_kernel(*refs, n_pairs):
    acc = refs[2 * n_pairs][...]
    for p in range(n_pairs):
        acc = acc + jnp.dot(refs[2 * p][...], refs[2 * p + 1][...], preferred_element_type=F32)
    refs[2 * n_pairs + 1][...] = acc


def _matmul_res(pairs, res):
    m = res.shape[0]
    n = res.shape[1]
    tm = _pick(m, (1024, 512, 256, 128))
    tn = _pick(n, (1024, 512, 256, 128))
    in_specs, args = [], []
    for a, w3, layer, row_blk in pairs:
        k = a.shape[1]
        in_specs += [pl.BlockSpec((tm, k), lambda i, j: (i, 0)),
                     pl.BlockSpec((None, k, tn),
                                  lambda i, j, layer=layer, row_blk=row_blk: (layer, row_blk, j))]
        args += [a, w3]
    in_specs.append(pl.BlockSpec((tm, tn), lambda i, j: (i, j)))
    return pl.pallas_call(
        functools.partial(_mm_kernel, n_pairs=len(pairs)),
        grid=(m // tm, n // tn),
        in_specs=in_specs,
        out_specs=pl.BlockSpec((tm, tn), lambda i, j: (i, j)),
        out_shape=jax.ShapeDtypeStruct((m, n), F32),
        compiler_params=_params("parallel", "parallel"),
        name="matmul_residual",
    )(*args, res)


def _proj_kernel(*refs, q_groups, q_scale, has_bf16, has_f32, has_gate):
    x_ref, gain_ref, w_ref = refs[:3]
    pos = 3
    wz_ref = ob_ref = of_ref = oz_ref = None
    if has_gate:
        wz_ref = refs[pos]
        pos += 1
    if has_bf16:
        ob_ref = refs[pos]
        pos += 1
    if has_f32:
        of_ref = refs[pos]
        pos += 1
    if has_gate:
        oz_ref = refs[pos]
        pos += 1
    xn_s = refs[pos]
    g = pl.program_id(1)

    @pl.when(g == 0)
    def _():
        x = x_ref[...]
        ms = jnp.mean(x * x, axis=-1, keepdims=True)
        xn_s[...] = ((x * lax.rsqrt(ms + EPS)) * gain_ref[...]).astype(BF16)
        if has_gate:
            oz_ref[...] = jnp.dot(xn_s[...], wz_ref[...], preferred_element_type=F32)

    y = jnp.dot(xn_s[...], w_ref[...], preferred_element_type=F32)
    is_q = None
    for qg in q_groups:
        is_q = (g == qg) if is_q is None else jnp.logical_or(is_q, g == qg)
    if has_bf16:
        ob_ref[...] = (y if is_q is None else y * jnp.where(is_q, q_scale, 1.0)).astype(BF16)
    if has_f32:
        if is_q is None:
            of_ref[...] = y
        else:
            @pl.when(jnp.logical_not(is_q))
            def _():
                of_ref[...] = y


def _norm_proj(x, gain, w3, layer, n_groups, gw, q_groups=(), q_scale=1.0,
               want_bf16=True, want_f32=True, w_gate=None):
    m, d = x.shape
    tm = _pick(m, (1024, 512, 256, 128))
    kv_groups = [g for g in range(n_groups) if g not in q_groups]
    assert not q_groups or max(q_groups) < max(kv_groups)

    def f32_slot(g):
        return sum(jnp.where(g > k, 1, 0) for k in kv_groups)

    in_specs = [pl.BlockSpec((tm, d), lambda i, g: (i, 0)),
                pl.BlockSpec((1, d), lambda i, g: (0, 0)),
                pl.BlockSpec((None, d, gw), lambda i, g: (layer, 0, g))]
    args = [x, gain.reshape(1, d).astype(F32), w3]
    if w_gate is not None:
        in_specs.append(pl.BlockSpec(w_gate.shape, lambda i, g: (0, 0)))
        args.append(w_gate)
    out_specs, out_shape = [], []
    if want_bf16:
        out_specs.append(pl.BlockSpec((None, tm, gw), lambda i, g: (g, i, 0)))
        out_shape.append(jax.ShapeDtypeStruct((n_groups, m, gw), BF16))
    if want_f32:
        out_specs.append(pl.BlockSpec((None, tm, gw), lambda i, g: (f32_slot(g), i, 0)))
        out_shape.append(jax.ShapeDtypeStruct((len(kv_groups), m, gw), F32))
    if w_gate is not None:
        out_specs.append(pl.BlockSpec((tm, w_gate.shape[1]), lambda i, g: (i, 0)))
        out_shape.append(jax.ShapeDtypeStruct((m, w_gate.shape[1]), F32))
    outs = pl.pallas_call(
        functools.partial(_proj_kernel, q_groups=tuple(q_groups), q_scale=q_scale,
                          has_bf16=want_bf16, has_f32=want_f32, has_gate=w_gate is not None),
        grid=(m // tm, n_groups),
        in_specs=in_specs, out_specs=out_specs, out_shape=out_shape,
        scratch_shapes=[pltpu.VMEM((tm, d), BF16)],
        compiler_params=_params("parallel", "arbitrary"),
        name="norm_projection",
    )(*args)
    return outs[0] if len(outs) == 1 else tuple(outs)


def _ffn_kernel(x_ref, gain_ref, wg_ref, wu_ref, wd_ref, *rest, final_norm):
    if final_norm:
        fgain_ref, o_ref, xn_s, acc_ref = rest
    else:
        o_ref, xn_s, acc_ref = rest
    f = pl.program_id(1)

    @pl.when(f == 0)
    def _():
        x = x_ref[...]
        ms = jnp.mean(x * x, axis=-1, keepdims=True)
        xn_s[...] = ((x * lax.rsqrt(ms + EPS)) * gain_ref[...]).astype(BF16)
        acc_ref[...] = jnp.zeros_like(acc_ref)

    xn = xn_s[...]
    g = jnp.dot(xn, wg_ref[...], preferred_element_type=F32)
    u = jnp.dot(xn, wu_ref[...], preferred_element_type=F32)
    h = (g * jax.nn.sigmoid(g)) * u
    acc_ref[...] += jnp.dot(h.astype(BF16), wd_ref[...], preferred_element_type=F32)

    @pl.when(f == pl.num_programs(1) - 1)
    def _():
        y = x_ref[...] + acc_ref[...]
        if final_norm:
            ms = jnp.mean(y * y, axis=-1, keepdims=True)
            y = (y * lax.rsqrt(ms + EPS)) * fgain_ref[...]
        o_ref[...] = y


def _ffn(x, gain, w_gate_up3, w_down3, layer, final_gain=None):
    m, d = x.shape
    dff = w_down3.shape[1]
    tm = _pick(m, (512, 256, 128))
    tf = _pick(dff, (512, 256, 128))
    n_f = dff // tf
    vec = pl.BlockSpec((1, d), lambda i, f: (0, 0))
    in_specs = [pl.BlockSpec((tm, d), lambda i, f: (i, 0)),
                vec,
                pl.BlockSpec((None, d, tf), lambda i, f: (layer, 0, f)),
                pl.BlockSpec((None, d, tf), lambda i, f: (layer, 0, n_f + f)),
                pl.BlockSpec((None, tf, d), lambda i, f: (layer, f, 0))]
    args = [x, gain.reshape(1, d).astype(F32), w_gate_up3, w_gate_up3, w_down3]
    if final_gain is not None:
        in_specs.append(vec)
        args.append(final_gain.reshape(1, d).astype(F32))
    return pl.pallas_call(
        functools.partial(_ffn_kernel, final_norm=final_gain is not None),
        grid=(m // tm, n_f),
        in_specs=in_specs,
        out_specs=pl.BlockSpec((tm, d), lambda i, f: (i, 0)),
        out_shape=jax.ShapeDtypeStruct((m, d), F32),
        scratch_shapes=[pltpu.VMEM((tm, d), BF16), pltpu.VMEM((tm, d), F32)],
        compiler_params=_params("parallel", "arbitrary"),
        name="swiglu",
    )(*args)


def _log_sigmoid(x):
    return jnp.minimum(x, 0.0) - jnp.log1p(jnp.exp(-jnp.abs(x)))


def _logf_kernel(fz_ref, b_ref, lf_ref, c2_ref):
    lf = _log_sigmoid(fz_ref[...] + b_ref[...])
    lf_ref[...] = lf
    n = lf.shape[1]
    lane = lax.broadcasted_iota(jnp.int32, lf.shape, 1)
    c = lf
    shift = 1
    while shift < n:
        c = c + jnp.where(lane >= shift, pltpu.roll(c, shift, axis=1), 0.0)
        shift *= 2
    c2_ref[...] = c * LOG2E


def _logf_cumsum(fz_t, b_f):
    h, n = fz_t.shape
    return pl.pallas_call(
        _logf_kernel,
        out_shape=[jax.ShapeDtypeStruct((h, n), F32)] * 2,
        name="logf_cumsum",
    )(fz_t, b_f.reshape(h, 1).astype(F32))


def _online_update(u, v, m_s, l_s, acc_s, row_shift):
    reps = u.shape[1] // LANES
    m_prev = m_s[...]
    m_new = jnp.maximum(m_prev, jnp.max(u, axis=1, keepdims=True) + row_shift)
    alpha = jnp.exp2(m_prev - m_new)
    p = jnp.exp2(u - _tile_lanes(m_new - row_shift, reps))
    l_s[...] = alpha * l_s[...] + jnp.sum(p, axis=1, keepdims=True)
    acc_s[...] = (_tile_lanes(alpha, acc_s.shape[1] // LANES) * acc_s[...]
                  + jnp.dot(p.astype(BF16), v, preferred_element_type=F32))
    m_s[...] = m_new


def _causal_mask(u):
    row = lax.broadcasted_iota(jnp.int32, u.shape, 0)
    col = lax.broadcasted_iota(jnp.int32, u.shape, 1)
    return jnp.where(row >= col, u, NEG_INF)


def _causal_chunks(i, scores, update):
    def pair(jj, c):
        j0 = 2 * jj
        u0, u1 = scores(j0, False), scores(j0 + 1, False)
        update(u0, j0)
        update(u1, j0 + 1)
        return c

    lax.fori_loop(0, lax.div(i, 2), pair, 0)
    odd = lax.rem(i, 2) == 1

    @pl.when(odd)
    def _():
        u0, u1 = scores(i - 1, False), scores(i, True)
        update(u0, i - 1)
        update(u1, i)

    @pl.when(jnp.logical_not(odd))
    def _():
        update(scores(i, True), i)


def _fox_kernel(q_ref, k_ref, v_ref, ccol_ref, crow_ref, o_ref, m_s, l_s, acc_s, *, t):
    i = pl.program_id(1)
    q = q_ref[...]
    cq = jnp.broadcast_to(ccol_ref[...], (t, LANES))
    m_s[...] = jnp.full_like(m_s, NEG_INF)
    l_s[...] = jnp.zeros_like(l_s)
    acc_s[...] = jnp.zeros_like(acc_s)
    dn = (((1,), (1,)), ((), ()))

    def scores(j, masked):
        k = k_ref[pl.ds(pl.multiple_of(j * t, t), t), :]
        u = lax.dot_general(q, k, dn, preferred_element_type=F32) - crow_ref[j]
        return _causal_mask(u) if masked else u

    def update(u, j):
        v = v_ref[pl.ds(pl.multiple_of(j * t, t), t), :]
        _online_update(u, v, m_s, l_s, acc_s, cq)

    _causal_chunks(i, scores, update)
    o_ref[...] = (acc_s[...] / l_s[...]).astype(o_ref.dtype)


def _fox_attention(st, gq, gk, gv, c_t):
    s_len = st.shape[1]
    t = _pick(s_len, (512, 256, 128))
    nc = s_len // t
    c_col = c_t.reshape(H_B, s_len, 1)
    c_row = c_t.reshape(H_B, nc, 1, t)
    return pl.pallas_call(
        functools.partial(_fox_kernel, t=t),
        grid=(H_B, nc),
        in_specs=[pl.BlockSpec((None, t, HD_B), lambda h, i: (gq, i, h)),
                  pl.BlockSpec((None, s_len, HD_B), lambda h, i: (gk, 0, h)),
                  pl.BlockSpec((None, s_len, HD_B), lambda h, i: (gv, 0, h)),
                  pl.BlockSpec((None, t, 1), lambda h, i: (h, i, 0)),
                  pl.BlockSpec((None, nc, 1, t), lambda h, i: (h, 0, 0, 0))],
        out_specs=pl.BlockSpec((t, HD_B), lambda h, i: (i, h)),
        out_shape=jax.ShapeDtypeStruct((s_len, H_B * HD_B), BF16),
        scratch_shapes=[pltpu.VMEM((t, LANES), F32), pltpu.VMEM((t, LANES), F32),
                        pltpu.VMEM((t, HD_B), F32)],
        compiler_params=_params("parallel", "parallel"),
        name="forget_attention",
    )(st, st, st, c_col, c_row)


def _diff_lambda_vec(lq1_ref, lk1_ref, lq2_ref, lk2_ref, lam_init):
    a = jnp.sum(lq1_ref[...] * lk1_ref[...], axis=1, keepdims=True)
    b = jnp.sum(lq2_ref[...] * lk2_ref[...], axis=1, keepdims=True)
    return jnp.exp(a) - jnp.exp(b) + lam_init


def _diff_kernel(q_ref, k_ref, v_ref, slope_ref, lq1_ref, lk1_ref, lq2_ref, lk2_ref, g_ref,
                 o_ref, m1_s, l1_s, a1_s, m2_s, l2_s, a2_s, *, t, lam_init):
    i = pl.program_id(1)
    q1 = q_ref[:, :HD_A]
    q2 = q_ref[:, HD_A:]
    slope = slope_ref[...]
    row_pos = (i * t + lax.broadcasted_iota(jnp.int32, (t, LANES), 0)).astype(F32)
    row_shift = -(slope * row_pos)
    for m_s, l_s, a_s in ((m1_s, l1_s, a1_s), (m2_s, l2_s, a2_s)):
        m_s[...] = jnp.full_like(m_s, NEG_INF)
        l_s[...] = jnp.zeros_like(l_s)
        a_s[...] = jnp.zeros_like(a_s)
    reps = t // LANES
    dn = (((1,), (1,)), ((), ()))

    def scores(j, masked):
        k = k_ref[pl.ds(pl.multiple_of(j * t, t), t), :]
        col_pos = (j * t + lax.broadcasted_iota(jnp.int32, (1, t), 1)).astype(F32)
        col_term = _tile_lanes(slope, reps) * col_pos
        us = []
        for qm, km in ((q1, k[:, :HD_A]), (q2, k[:, HD_A:])):
            u = lax.dot_general(qm, km, dn, preferred_element_type=F32) + col_term
            us.append(_causal_mask(u) if masked else u)
        return us

    def update(us, j):
        v = v_ref[pl.ds(pl.multiple_of(j * t, t), t), :]
        _online_update(us[0], v, m1_s, l1_s, a1_s, row_shift)
        _online_update(us[1], v, m2_s, l2_s, a2_s, row_shift)

    _causal_chunks(i, scores, update)
    lam = _diff_lambda_vec(lq1_ref, lk1_ref, lq2_ref, lk2_ref, lam_init)
    o = (a1_s[...] / _tile_lanes(l1_s[...], 2)
         - lam * (a2_s[...] / _tile_lanes(l2_s[...], 2)))
    y = o * lax.rsqrt(jnp.mean(o * o, axis=1, keepdims=True) + EPS) * g_ref[...] * (1.0 - lam_init)
    o_ref[...] = y.astype(o_ref.dtype)


def _alibi_slopes(n):
    return jnp.asarray(2.0 ** (-8.0 * np.arange(1, n + 1) / n), dtype=F32)


def _diff_attention(st, gq, gk, gv, lam_params, g_sub, lam_init):
    s_len = st.shape[1]
    t = _pick(s_len, (512, 256, 128))
    nc = s_len // t
    w = 2 * HD_A
    slopes = jnp.broadcast_to((_alibi_slopes(H_A) * LOG2E)[:, None, None], (H_A, 1, LANES))
    vec = pl.BlockSpec((1, HD_A), lambda h, i: (0, 0))
    return pl.pallas_call(
        functools.partial(_diff_kernel, t=t, lam_init=lam_init),
        grid=(H_A, nc),
        in_specs=[pl.BlockSpec((None, t, w), lambda h, i: (gq, i, h)),
                  pl.BlockSpec((None, s_len, w), lambda h, i: (gk, 0, h)),
                  pl.BlockSpec((None, s_len, w), lambda h, i: (gv, 0, h)),
                  pl.BlockSpec((None, 1, LANES), lambda h, i: (h, 0, 0)),
                  vec, vec, vec, vec,
                  pl.BlockSpec((1, w), lambda h, i: (0, 0))],
        out_specs=pl.BlockSpec((t, w), lambda h, i: (i, h)),
        out_shape=jax.ShapeDtypeStruct((s_len, H_A * w), BF16),
        scratch_shapes=[pltpu.VMEM((t, LANES), F32), pltpu.VMEM((t, LANES), F32),
                        pltpu.VMEM((t, w), F32),
                        pltpu.VMEM((t, LANES), F32), pltpu.VMEM((t, LANES), F32),
                        pltpu.VMEM((t, w), F32)],
        compiler_params=_params("parallel", "parallel"),
        name="diff_attention",
    )(st, st, st, slopes, *lam_params, g_sub.reshape(1, w).astype(F32))


PAGE_ROWS = PAGE_SIZE * 8


def _even_sample_tables(n_t, n_new_cols):
    slopes = 2.0 ** (-8.0 * np.arange(1, H_A + 1) / H_A)
    ninf = -np.inf
    col = np.arange(PAGE_ROWS)
    p_col, j_col = col >> 3, col & 7
    ra = np.arange(2 * n_t * H_A)
    m_r, t_r, h_r = ra // (n_t * H_A), (ra // H_A) % n_t, ra % H_A
    j_r = 2 * h_r + m_r
    sl_r = slopes[h_r]
    ta = np.where(j_col[None, :] == j_r[:, None], sl_r[:, None] * p_col[None, :], ninf)
    rb = np.arange(n_t * H_B)
    tb_t, tb_h = rb // H_B, rb % H_B
    tb = np.where(j_col[None, :] == tb_h[:, None], 0.0, ninf)
    ncol = np.arange(n_new_cols)
    u_col, jn_col = ncol >> 3, ncol & 7
    ok_a = (jn_col[None, :] == j_r[:, None]) & (u_col[None, :] <= t_r[:, None])
    tan = np.where(ok_a, -sl_r[:, None] * (t_r[:, None] - u_col[None, :]), ninf)
    ok_b = (jn_col[None, :] == tb_h[:, None]) & (u_col[None, :] <= tb_t[:, None])
    tbn = np.where(ok_b, 0.0, ninf)
    rowa = np.stack([sl_r, t_r.astype(np.float64)], axis=1)
    lane = np.arange(MXU_TILE) & 7
    me = np.stack([(lane[None, :] == (4 * e + h_r)[:, None]).astype(np.float64) for e in range(2)])
    gi = np.arange(MXU_TILE) >> 3
    gt = (gi[:, None] == gi[None, :]).astype(np.float64)
    f = lambda a: jnp.asarray(a, F32)
    return f(ta), f(tb), f(tan), f(tbn), f(rowa), f(me), jnp.asarray(gt, BF16)


def _even_sample_kernel(pt_ref, qa_ref, qb_ref, lfr_ref, lfl_ref, kan_ref, van_ref, kbn_ref, vbn_ref,
                        ta_ref, tb_ref, tan_ref, tbn_ref, rowa_ref, me_ref, gt_ref,
                        lq1_ref, lk1_ref, lq2_ref, lk2_ref, gsub_ref, *rest,
                        n_pg, n_t, p_len, scale, lam_init):
    ka_refs = rest[0 * n_pg:1 * n_pg]
    va_refs = rest[1 * n_pg:2 * n_pg]
    kb_refs = rest[2 * n_pg:3 * n_pg]
    vb_refs = rest[3 * n_pg:4 * n_pg]
    lf_refs = rest[4 * n_pg:5 * n_pg]
    oa_ref, ob_ref = rest[5 * n_pg:5 * n_pg + 2]
    ma_s, la_s, acca_s, mb_s, lb_s, accb_s, carry_s = rest[5 * n_pg + 2:]
    j = pl.program_id(1)
    n_groups = pl.num_programs(1)
    n_ra = 2 * n_t * H_A
    dn = (((1,), (1,)), ((), ()))
    qa = qa_ref[...]
    qb = qb_ref[...]
    slope = rowa_ref[:, 0:1]
    t_row = rowa_ref[:, 1:2]
    gt = gt_ref[...]

    @pl.when(j == 0)
    def _():
        for m_s, l_s, a_s in ((ma_s, la_s, acca_s), (mb_s, lb_s, accb_s)):
            m_s[...] = jnp.full_like(m_s, NEG_INF)
            l_s[...] = jnp.zeros_like(l_s)
            a_s[...] = jnp.zeros_like(a_s)
        carry_s[...] = jnp.zeros_like(carry_s)

    def softmax_step(s, m_s, l_s):
        m_prev = m_s[...]
        m_new = jnp.maximum(m_prev, jnp.max(s, axis=1, keepdims=True))
        alpha = jnp.exp(m_prev - m_new)
        p = jnp.exp(s - m_new)
        l_s[...] = alpha * l_s[...] + jnp.sum(p, axis=1, keepdims=True)
        m_s[...] = m_new
        return p, alpha

    def spread(p):
        n_tiles = p.shape[1] // MXU_TILE
        pb = p.astype(BF16)
        stacked = jnp.concatenate(
            [pb[:, c * MXU_TILE:(c + 1) * MXU_TILE] for c in range(n_tiles)], axis=0)
        rep = jnp.dot(stacked, gt, preferred_element_type=F32)
        halves = []
        for e in range(2):
            me = me_ref[e]
            halves.append(jnp.concatenate(
                [rep[c * n_ra:(c + 1) * n_ra] * me for c in range(n_tiles)], axis=1))
        return jnp.concatenate(halves, axis=0).astype(BF16)

    def attend_a(s, vs):
        p, alpha = softmax_step(s, ma_s, la_s)
        p2 = spread(p)
        pv = None
        for g, v in enumerate(vs):
            rows = v.shape[0]
            y = jnp.dot(p2[:, g * rows:(g + 1) * rows], v, preferred_element_type=F32)
            pv = y if pv is None else pv + y
        acca_s[...] = jnp.concatenate([alpha, alpha], axis=0) * acca_s[...] + pv

    def attend_b(s, vs):
        p, alpha = softmax_step(s, mb_s, lb_s)
        pb = p.astype(BF16)
        pv = None
        for g, v in enumerate(vs):
            rows = v.shape[0]
            y = jnp.dot(pb[:, g * rows:(g + 1) * rows], v, preferred_element_type=F32)
            pv = y if pv is None else pv + y
        accb_s[...] = alpha * accb_s[...] + pv

    def scores(q, ks):
        return jnp.concatenate(
            [lax.dot_general(q, k, dn, preferred_element_type=F32) for k in ks], axis=1) * scale

    lfr = lfr_ref[...]
    parts = [lfr[0:H_B]]
    for t in range(1, n_t):
        parts.append(parts[-1] + lfr[t * H_B:(t + 1) * H_B])
    cn_col = jnp.concatenate(parts, axis=0)

    group = n_groups - 1 - j
    ta = ta_ref[...]
    bias_a = jnp.concatenate(
        [ta + slope * ((group * n_pg + g).astype(F32) * float(PAGE_SIZE) - (p_len + t_row))
         for g in range(n_pg)], axis=1)
    attend_a(scores(qa, [r[...].astype(BF16) for r in ka_refs]) + bias_a,
             [r[...].astype(BF16) for r in va_refs])

    lane = lax.broadcasted_iota(jnp.int32, (H_B, PAGE_ROWS), 1)
    tail = carry_s[...]
    sufs = [None] * n_pg
    for g in range(n_pg - 1, -1, -1):
        lf = lf_refs[g][...]
        inc = lf
        shift = 8
        while shift < PAGE_ROWS:
            inc = inc + jnp.where(lane < PAGE_ROWS - shift,
                                  pltpu.roll(inc, PAGE_ROWS - shift, axis=1), 0.0)
            shift *= 2
        sufs[g] = (inc - lf) + tail
        tail = tail + inc[:, 0:1]
    carry_s[...] = tail
    suf = jnp.concatenate(sufs, axis=1)
    bias_b = (jnp.concatenate([suf] * n_t, axis=0) + cn_col
              + jnp.concatenate([tb_ref[...]] * n_pg, axis=1))
    attend_b(scores(qb, [r[...].astype(BF16) for r in kb_refs]) + bias_b,
             [r[...].astype(BF16) for r in vb_refs])

    @pl.when(j == n_groups - 1)
    def _():
        attend_a(scores(qa, [kan_ref[...]]) + tan_ref[...], [van_ref[...]])
        lfl = lfl_ref[...]
        ln = lax.broadcasted_iota(jnp.int32, lfl.shape, 1)
        cn_lane = lfl
        shift = H_B
        while shift < n_t * H_B:
            cn_lane = cn_lane + jnp.where(ln >= shift, pltpu.roll(cn_lane, shift, axis=1), 0.0)
            shift *= 2
        n_new = tbn_ref.shape[1]
        cn_keys = _tile_lanes(cn_lane, n_new // LANES)
        attend_b(scores(qb, [kbn_ref[...]]) + (tbn_ref[...] + (cn_col - cn_keys)), [vbn_ref[...]])

        lam = _diff_lambda_vec(lq1_ref, lk1_ref, lq2_ref, lk2_ref, lam_init)
        la = la_s[...]
        fa = acca_s[...] / jnp.concatenate([la, la], axis=0)
        half = n_ra // 2
        o = [fa[e * n_ra:e * n_ra + half] - lam * fa[e * n_ra + half:(e + 1) * n_ra]
             for e in range(2)]
        ms = (jnp.sum(o[0] * o[0], axis=1, keepdims=True)
              + jnp.sum(o[1] * o[1], axis=1, keepdims=True)) / (2.0 * HD_A)
        inv = lax.rsqrt(ms + EPS)
        for e in range(2):
            oa_ref[e] = o[e] * inv * gsub_ref[:, e * HD_A:(e + 1) * HD_A] * (1.0 - lam_init)
        ob_ref[...] = accb_s[...] / lb_s[...]


def _even_sample_attention(qa2, qb2, lf_rows, lf_lanes, new_kv, pools, page_table, pool_off,
                           lam_params, g_sub, lam_init):
    pool_ak, pool_av, pool_bk, pool_bv, pool_lf = pools
    bsz, n_ra, hd = qa2.shape
    n_rb = qb2.shape[1]
    n_t = n_rb // H_B
    n_new = new_kv[0].shape[1]
    n_pages = page_table.shape[1]
    n_pg = _pick(n_pages, (8, 4, 2, 1))
    n_groups = n_pages // n_pg
    tables = _even_sample_tables(n_t, n_new)

    def page_map(g):
        def index(b, j, pt):
            return (pool_off + pt[b * n_pages + (n_groups - 1 - j) * n_pg + g], 0, 0)
        return index

    per_b = lambda b, j, pt: (b, 0, 0)
    const2 = lambda b, j, pt: (0, 0)
    const3 = lambda b, j, pt: (0, 0, 0)
    vec = pl.BlockSpec((1, HD_A), const2)
    in_specs = [pl.BlockSpec((None, n_ra, hd), per_b),
                pl.BlockSpec((None, n_rb, hd), per_b),
                pl.BlockSpec((None, n_rb, 1), per_b),
                pl.BlockSpec((None, 1, LANES), per_b)]
    in_specs += [pl.BlockSpec((None, n_new, hd), per_b)] * 4
    in_specs += [pl.BlockSpec(tables[0].shape, const2), pl.BlockSpec(tables[1].shape, const2),
                 pl.BlockSpec(tables[2].shape, const2), pl.BlockSpec(tables[3].shape, const2),
                 pl.BlockSpec(tables[4].shape, const2), pl.BlockSpec(tables[5].shape, const3),
                 pl.BlockSpec(tables[6].shape, const2)]
    in_specs += [vec, vec, vec, vec, pl.BlockSpec((1, 2 * HD_A), const2)]
    args = [qa2, qb2, lf_rows, lf_lanes, *new_kv, *tables, *lam_params,
            g_sub.reshape(1, 2 * HD_A).astype(F32)]
    for pool in (pool_ak, pool_av, pool_bk, pool_bv):
        for g in range(n_pg):
            in_specs.append(pl.BlockSpec((None, PAGE_ROWS, hd), page_map(g)))
            args.append(pool)
    for g in range(n_pg):
        in_specs.append(pl.BlockSpec((None, H_B, PAGE_ROWS), page_map(g)))
        args.append(pool_lf)
    grid_spec = pltpu.PrefetchScalarGridSpec(
        num_scalar_prefetch=1,
        grid=(bsz, n_groups),
        in_specs=in_specs,
        out_specs=[pl.BlockSpec((None, 2, n_ra // 2, hd), lambda b, j, pt: (b, 0, 0, 0)),
                   pl.BlockSpec((None, n_rb, hd), per_b)],
        scratch_shapes=[pltpu.VMEM((n_ra, 1), F32), pltpu.VMEM((n_ra, 1), F32),
                        pltpu.VMEM((2 * n_ra, hd), F32),
                        pltpu.VMEM((n_rb, 1), F32), pltpu.VMEM((n_rb, 1), F32),
                        pltpu.VMEM((n_rb, hd), F32),
                        pltpu.VMEM((H_B, 1), F32)])
    return pl.pallas_call(
        functools.partial(_even_sample_kernel, n_pg=n_pg, n_t=n_t,
                          p_len=float(n_pages * PAGE_SIZE), scale=HD_A ** -0.5,
                          lam_init=lam_init),
        grid_spec=grid_spec,
        out_shape=[jax.ShapeDtypeStruct((bsz, 2, n_ra // 2, hd), F32),
                   jax.ShapeDtypeStruct((bsz, n_rb, hd), F32)],
        compiler_params=_params("parallel", "arbitrary"),
        name="even_sample_attention",
    )(page_table.reshape(-1), *args)


def _pattern_count(dist):
    return sum(((dist >= 0) & (dist <= w) & (dist % d == 0)).astype(jnp.int32)
               for w, d in C_PATTERNS)


def _count_bias(dist, slopes, same_head=True):
    cnt = _pattern_count(dist)
    logc = jnp.where((cnt > 0) & same_head, jnp.log(jnp.maximum(cnt, 1).astype(F32)), NEG_INF)
    return logc - slopes * dist.astype(F32)


def _dilated_bias(t, n_off):
    idx = jnp.arange(t, dtype=jnp.int32)
    dist = (idx[None, :, None] - idx[None, None, :]
            + t * jnp.arange(n_off, dtype=jnp.int32)[:, None, None])
    return _count_bias(dist[None], _alibi_slopes(H_C)[:, None, None, None]) * LOG2E


def _dil_kernel(q_ref, k_ref, v_ref, bias_ref, o_ref, m_s, l_s, acc_s, *, t, n_off):
    i = pl.program_id(1)
    q = q_ref[...]
    m_s[...] = jnp.full_like(m_s, NEG_INF)
    l_s[...] = jnp.zeros_like(l_s)
    acc_s[...] = jnp.zeros_like(acc_s)
    dn = (((1,), (1,)), ((), ()))
    no_shift = jnp.zeros((t, LANES), F32)

    def scores(off):
        k = k_ref[pl.ds(pl.multiple_of((i - off) * t, t), t), :]
        return lax.dot_general(q, k, dn, preferred_element_type=F32) + bias_ref[off]

    def update(u, off):
        v = v_ref[pl.ds(pl.multiple_of((i - off) * t, t), t), :]
        _online_update(u, v, m_s, l_s, acc_s, no_shift)

    @pl.when(i >= n_off - 1)
    def _():
        us = [scores(off) for off in range(n_off)]
        for off in range(n_off):
            update(us[off], off)

    @pl.when(i < n_off - 1)
    def _():
        for off in range(n_off - 1):
            @pl.when(i >= off)
            def _(off=off):
                update(scores(off), off)

    o_ref[...] = (acc_s[...] / l_s[...]).astype(o_ref.dtype)


def _dilated_attention(st):
    s_len = st.shape[1]
    per = st.shape[2] // HD_C
    t = _pick(s_len, (512, 256, 128))
    n_off = min(C_WMAX // t + 1, s_len // t)
    bias = _dilated_bias(t, n_off)
    return pl.pallas_call(
        functools.partial(_dil_kernel, t=t, n_off=n_off),
        grid=(H_C, s_len // t),
        in_specs=[pl.BlockSpec((None, t, HD_C), lambda h, i: (h // per, i, h % per)),
                  pl.BlockSpec((None, s_len, HD_C), lambda h, i: (2 + h // per, 0, h % per)),
                  pl.BlockSpec((None, s_len, HD_C), lambda h, i: (4 + h // per, 0, h % per)),
                  pl.BlockSpec((None, n_off, t, t), lambda h, i: (h, 0, 0, 0))],
        out_specs=pl.BlockSpec((t, HD_C), lambda h, i: (i, h)),
        out_shape=jax.ShapeDtypeStruct((s_len, H_C * HD_C), BF16),
        scratch_shapes=[pltpu.VMEM((t, LANES), F32), pltpu.VMEM((t, LANES), F32),
                        pltpu.VMEM((t, HD_C), F32)],
        compiler_params=_params("parallel", "parallel"),
        name="dilated_attention",
    )(st, st, st, bias)


def _c_sample_bias(buf_len, n_t, n_new_cols):
    row = jnp.arange(n_t * H_C, dtype=jnp.int32)[:, None]
    t_row, h_row = row // H_C, row % H_C
    slopes = jnp.tile(_alibi_slopes(H_C), n_t)[:, None]

    def table(n_pos, pos0):
        col = jnp.arange(n_pos * H_C, dtype=jnp.int32)[None, :]
        dist = buf_len + t_row - (pos0 + col // H_C)
        return _count_bias(dist, slopes, h_row == col % H_C)

    past = table(buf_len, 0)
    new = table(n_t, buf_len)
    new = jnp.pad(new, ((0, 0), (0, n_new_cols - new.shape[1])), constant_values=NEG_INF)
    return past, new


def _c_sample_kernel(q_ref, kc_ref, kx_ref, kn_ref, vc_ref, vx_ref, vn_ref, bias_ref, biasn_ref,
                     ok_ref, ov_ref, o_ref, m_s, l_s, acc_s, *, rows, new, scale):
    c = pl.program_id(1)
    last = pl.num_programs(1) - 1
    dn = (((1,), (1,)), ((), ()))
    q = q_ref[...]

    @pl.when(c == 0)
    def _():
        m_s[...] = jnp.full_like(m_s, NEG_INF)
        l_s[...] = jnp.zeros_like(l_s)
        acc_s[...] = jnp.zeros_like(acc_s)

    for src, nxt, fresh, dst in ((kc_ref, kx_ref, kn_ref, ok_ref), (vc_ref, vx_ref, vn_ref, ov_ref)):
        dst[pl.ds(0, rows - new), :] = src[pl.ds(new, rows - new), :]

        @pl.when(c < last)
        def _(nxt=nxt, dst=dst):
            dst[pl.ds(rows - new, new), :] = nxt[...]

        @pl.when(c == last)
        def _(fresh=fresh, dst=dst):
            dst[pl.ds(rows - new, new), :] = fresh[...]

    def attend(s, v):
        m_prev = m_s[...]
        m_new = jnp.maximum(m_prev, jnp.max(s, axis=1, keepdims=True))
        alpha = jnp.exp(m_prev - m_new)
        p = jnp.exp(s - m_new)
        l_s[...] = alpha * l_s[...] + jnp.sum(p, axis=1, keepdims=True)
        acc_s[...] = alpha * acc_s[...] + jnp.dot(p.astype(BF16), v, preferred_element_type=F32)
        m_s[...] = m_new

    s = lax.dot_general(q, kc_ref[...].astype(BF16), dn, preferred_element_type=F32)
    attend(s * scale + bias_ref[...], vc_ref[...].astype(BF16))

    @pl.when(c == last)
    def _():
        pad = jnp.zeros((biasn_ref.shape[1] - new, kn_ref.shape[1]), F32)
        kn = jnp.concatenate([kn_ref[...], pad], axis=0).astype(BF16)
        vn = jnp.concatenate([vn_ref[...], pad], axis=0).astype(BF16)
        s_n = lax.dot_general(q, kn, dn, preferred_element_type=F32)
        attend(s_n * scale + biasn_ref[...], vn)
        o_ref[...] = acc_s[...] / l_s[...]


def _c_sample(q2, cache_k, cache_v, k_new, v_new, batch_off, n_t):
    bsz, n_r, hd = q2.shape
    total = cache_k.shape[1]
    new = k_new.shape[1]
    buf_len = total // H_C
    rows = _pick(total, (8192, 4096, 2048, 1024))
    assert rows % new == 0 and total % rows == 0
    n_chunks = total // rows
    n_new_cols = max(LANES, new)
    bias, bias_n = _c_sample_bias(buf_len, n_t, n_new_cols)
    blocks_per_chunk = rows // new
    n_small = total // new

    cur = pl.BlockSpec((None, rows, hd), lambda b, c: (batch_off + b, c, 0))
    nxt = pl.BlockSpec((None, new, hd),
                       lambda b, c: (batch_off + b, jnp.minimum((c + 1) * blocks_per_chunk,
                                                                n_small - 1), 0))
    per_b = pl.BlockSpec((None, new, hd), lambda b, c: (b, 0, 0))
    out_buf = pl.BlockSpec((None, rows, hd), lambda b, c: (b, c, 0))
    return pl.pallas_call(
        functools.partial(_c_sample_kernel, rows=rows, new=new, scale=HD_C ** -0.5),
        grid=(bsz, n_chunks),
        in_specs=[pl.BlockSpec((None, n_r, hd), lambda b, c: (b, 0, 0)),
                  cur, nxt, per_b, cur, nxt, per_b,
                  pl.BlockSpec((n_r, rows), lambda b, c: (0, c)),
                  pl.BlockSpec((n_r, n_new_cols), lambda b, c: (0, 0))],
        out_specs=[out_buf, out_buf, pl.BlockSpec((None, n_r, hd), lambda b, c: (b, 0, 0))],
        out_shape=[jax.ShapeDtypeStruct((bsz, total, hd), F32),
                   jax.ShapeDtypeStruct((bsz, total, hd), F32),
                   jax.ShapeDtypeStruct((bsz, n_r, hd), F32)],
        scratch_shapes=[pltpu.VMEM((n_r, 1), F32), pltpu.VMEM((n_r, 1), F32),
                        pltpu.VMEM((n_r, hd), F32)],
        compiler_params=_params("parallel", "arbitrary"),
        name="window_sample_attention",
    )(q2, cache_k, cache_k, k_new, cache_v, cache_v, v_new, bias, bias_n)


def _cross_kernel(q_ref, k_ref, v_ref, o_ref, *, scale):
    dn = (((1,), (1,)), ((), ()))
    for h in range(H_X):
        cols = slice(h * HD_X, (h + 1) * HD_X)
        s = lax.dot_general(q_ref[:, cols], k_ref[:, cols].astype(BF16), dn,
                            preferred_element_type=F32) * scale
        m = jnp.max(s, axis=1, keepdims=True)
        e = jnp.exp(s - m)
        l = jnp.sum(e, axis=1, keepdims=True)
        o = jnp.dot(e.astype(BF16), v_ref[:, cols].astype(BF16), preferred_element_type=F32)
        o_ref[:, cols] = (o / l).astype(o_ref.dtype)


def _cross_attention(q, mem_k, mem_v):
    bsz, n_q, w = q.shape
    n_m = mem_k.shape[1]
    tq = _pick(n_q, (512, 256, 128))
    return pl.pallas_call(
        functools.partial(_cross_kernel, scale=HD_X ** -0.5),
        grid=(bsz, n_q // tq),
        in_specs=[pl.BlockSpec((None, tq, w), lambda b, i: (b, i, 0)),
                  pl.BlockSpec((None, n_m, w), lambda b, i: (b, 0, 0)),
                  pl.BlockSpec((None, n_m, w), lambda b, i: (b, 0, 0))],
        out_specs=pl.BlockSpec((None, tq, w), lambda b, i: (b, i, 0)),
        out_shape=jax.ShapeDtypeStruct((bsz, n_q, w), BF16),
        compiler_params=_params("parallel", "parallel"),
        name="cross_attention",
    )(q, mem_k, mem_v)


def _cross_rows_kernel(q_ref, k_ref, v_ref, mask_ref, o_ref, *, scale):
    dn = (((1,), (1,)), ((), ()))
    s = lax.dot_general(q_ref[...], k_ref[...].astype(BF16), dn,
                        preferred_element_type=F32) * scale + mask_ref[...]
    m = jnp.max(s, axis=1, keepdims=True)
    e = jnp.exp(s - m)
    l = jnp.sum(e, axis=1, keepdims=True)
    o = jnp.dot(e.astype(BF16), v_ref[...].astype(BF16), preferred_element_type=F32)
    o_ref[...] = o / l


def _cross_attention_rows(q2, mem_k, mem_v, batch_off):
    bsz, n_r, hd = q2.shape
    n_c = mem_k.shape[1]
    same = (np.arange(n_r)[:, None] % H_X) == (np.arange(n_c)[None, :] % H_X)
    mask = jnp.asarray(np.where(same, 0.0, -np.inf), F32)
    kv = pl.BlockSpec((None, n_c, hd), lambda b: (batch_off + b, 0, 0))
    return pl.pallas_call(
        functools.partial(_cross_rows_kernel, scale=HD_X ** -0.5),
        grid=(bsz,),
        in_specs=[pl.BlockSpec((None, n_r, hd), lambda b: (b, 0, 0)), kv, kv,
                  pl.BlockSpec((n_r, n_c), lambda b: (0, 0))],
        out_specs=pl.BlockSpec((None, n_r, hd), lambda b: (b, 0, 0)),
        out_shape=jax.ShapeDtypeStruct((bsz, n_r, hd), F32),
        compiler_params=_params("parallel"),
        name="cross_attention_rows",
    )(q2, mem_k, mem_v, mask)


NEW_KEY_ROWS = MXU_TILE


def _pad_rows(x, rows):
    return jnp.pad(x, ((0, 0), (0, rows - x.shape[1]), (0, 0)))


def kernel(x_prompt, x_sample, mem_prompt, cache_a_k, cache_a_v, cache_b_k, cache_b_v, cache_b_logf, cache_c_k, cache_c_v, cache_mem_k, cache_mem_v, page_table, norm_mix, norm_cross, norm_mem, norm_ffn, norm_final, w_in_even, b_forget, lambda_q1, lambda_k1, lambda_q2, lambda_k2, subln_a, w_out_even, w_in_odd, w_out_odd, w_xq, w_xkv, w_xo, w_gate_up, w_down):
    n_b, s_len, d_model = x_prompt.shape
    d_b, d_t, _ = x_sample.shape
    assert n_b == 1
    depth = norm_mix.shape[0]
    n_mem = mem_prompt.shape[1]
    xp = x_prompt.reshape(s_len, d_model)
    xs = x_sample.reshape(d_b * d_t, d_model)
    mem = mem_prompt.reshape(n_mem, d_model)
    wa = H_A * 2 * HD_A
    wb = H_B * HD_B
    wc = H_C * HD_C
    wx = H_X * HD_X

    n_even, n_pool = cache_a_k.shape[:2]
    pool_ak = cache_a_k.reshape(n_even * n_pool, PAGE_ROWS, HD_A)
    pool_av = (cache_a_v.reshape(n_even, n_pool, PAGE_SIZE, H_A, 2, HD_A)
               .transpose(0, 1, 2, 4, 3, 5).reshape(n_even * n_pool, PAGE_ROWS, HD_A))
    pool_bk = cache_b_k.reshape(n_even * n_pool, PAGE_ROWS, HD_B)
    pool_bv = cache_b_v.reshape(n_even * n_pool, PAGE_ROWS, HD_B)
    pool_lf = jnp.repeat(jnp.swapaxes(cache_b_logf, 2, 3), H_B, axis=3).reshape(
        n_even * n_pool, H_B, PAGE_ROWS)
    n_odd, _, buf_len = cache_c_k.shape[:3]
    win_k = cache_c_k.reshape(n_odd * d_b, buf_len * H_C, HD_C)
    win_v = cache_c_v.reshape(n_odd * d_b, buf_len * H_C, HD_C)
    memc_k = cache_mem_k.reshape(depth * d_b, n_mem * H_X, HD_X)
    memc_v = cache_mem_v.reshape(depth * d_b, n_mem * H_X, HD_X)

    ak_p, av_p, bk_p, bv_p, bl_p, ck_p, cv_p, mk_p, mv_p = [], [], [], [], [], [], [], [], []
    ak_s, av_s, bk_s, bv_s, bl_s, ck_s, cv_s = [], [], [], [], [], [], []

    w_in_even16, w_out_even16 = w_in_even.astype(BF16), w_out_even.astype(BF16)
    w_in_odd16, w_out_odd16 = w_in_odd.astype(BF16), w_out_odd.astype(BF16)
    w_xq16, w_xkv16, w_xo16 = w_xq.astype(BF16), w_xkv.astype(BF16), w_xo.astype(BF16)
    w_gate_up16, w_down16 = w_gate_up.astype(BF16), w_down.astype(BF16)
    gw = 1024

    for l in range(depth):
        j = l // 2
        if l % 2 == 0:
            lam_init = 0.8 - 0.6 * math.exp(-0.3 * l)
            w_fz = jnp.pad(w_in_even[j][:, 6 * wa:], ((0, 0), (0, LANES - H_B))).astype(BF16)
            lam_params = [p[j].reshape(1, HD_A).astype(F32)
                          for p in (lambda_q1, lambda_k1, lambda_q2, lambda_k2)]
            assert wa == gw and wb == gw and HD_A == HD_B
            proj = functools.partial(_norm_proj, gain=norm_mix[l], w3=w_in_even16, layer=j,
                                     n_groups=6, gw=gw, q_groups=(0, 3), w_gate=w_fz)

            st16, st32, fz = proj(xp, q_scale=HD_A ** -0.5 * LOG2E)
            lf_t, c_t = _logf_cumsum(fz[:, :H_B].T, b_forget[j])
            za = _diff_attention(st16, 0, 1, 2, lam_params, subln_a[j], lam_init)
            zb = _fox_attention(st16, 3, 4, 5, c_t)
            xp = _matmul_res([(za, w_out_even16, j, 0), (zb, w_out_even16, j, 1)], xp)
            ak_p.append(st32[0].reshape(1, s_len, H_A, 2, HD_A))
            av_p.append(st32[1].reshape(1, s_len, H_A, 2 * HD_A))
            bk_p.append(st32[2].reshape(1, s_len, H_B, HD_B))
            bv_p.append(st32[3].reshape(1, s_len, H_B, HD_B))
            bl_p.append(lf_t.T.reshape(1, s_len, H_B))

            ss16, ss32, fz_s = proj(xs)
            qa_s, ka_s16, va_s16, qb_s, kb_s16, vb_s16 = (ss16[g] for g in range(6))
            ka_s, va_s, kb_s, vb_s = (ss32[g] for g in range(4))
            lf_s_t, _ = _logf_cumsum(fz_s[:, :H_B].T, b_forget[j])
            lf_s = lf_s_t.T
            qa2 = (qa_s.reshape(d_b, d_t, H_A, 2, HD_A).transpose(0, 3, 1, 2, 4)
                   .reshape(d_b, 2 * d_t * H_A, HD_A))
            qb2 = qb_s.reshape(d_b, d_t * H_B, HD_B)
            va_rows = (va_s16.reshape(d_b, d_t, H_A, 2, HD_A).transpose(0, 1, 3, 2, 4)
                       .reshape(d_b, d_t * 8, HD_A))
            new_kv = [_pad_rows(a, NEW_KEY_ROWS)
                      for a in (ka_s16.reshape(d_b, d_t * 8, HD_A), va_rows,
                                kb_s16.reshape(d_b, d_t * H_B, HD_B),
                                vb_s16.reshape(d_b, d_t * H_B, HD_B))]
            lf_rows = lf_s.reshape(d_b, d_t * H_B, 1)
            lf_lanes = _pad_rows(lf_rows, LANES).reshape(d_b, 1, LANES)
            oa_s, ob_s = _even_sample_attention(
                qa2, qb2, lf_rows, lf_lanes, new_kv,
                (pool_ak, pool_av, pool_bk, pool_bv, pool_lf), page_table, j * n_pool,
                lam_params, subln_a[j], lam_init)
            za_s = (oa_s.reshape(d_b, 2, d_t, H_A, HD_A).transpose(0, 2, 3, 1, 4)
                    .reshape(d_b * d_t, wa).astype(BF16))
            zb_s = ob_s.reshape(d_b * d_t, wb).astype(BF16)
            xs = _matmul_res([(za_s, w_out_even16, j, 0), (zb_s, w_out_even16, j, 1)], xs)
            ak_s.append(ka_s.reshape(d_b, d_t, H_A, 2, HD_A))
            av_s.append(va_s.reshape(d_b, d_t, H_A, 2 * HD_A))
            bk_s.append(kb_s.reshape(d_b, d_t, H_B, HD_B))
            bv_s.append(vb_s.reshape(d_b, d_t, H_B, HD_B))
            bl_s.append(lf_s.reshape(d_b, d_t, H_B))
        else:
            assert wc == 2 * gw
            proj = functools.partial(_norm_proj, gain=norm_mix[l], w3=w_in_odd16, layer=j,
                                     n_groups=6, gw=gw, q_groups=(0, 1))
            halves = lambda s32, a: jnp.concatenate([s32[a], s32[a + 1]], axis=1)

            st16, st32 = proj(xp, q_scale=HD_C ** -0.5 * LOG2E)
            o = _dilated_attention(st16)
            xp = _matmul_res([(o, w_out_odd16, j, 0)], xp)
            keep = min(C_WMAX, s_len)
            ck_p.append(halves(st32[:, s_len - keep:], 0).reshape(1, keep, H_C, HD_C))
            cv_p.append(halves(st32[:, s_len - keep:], 2).reshape(1, keep, H_C, HD_C))

            ss16, ss32 = proj(xs)
            q_s = jnp.concatenate([ss16[0], ss16[1]], axis=1)
            k_s, v_s = halves(ss32, 0), halves(ss32, 2)
            new_k, new_v, o_s = _c_sample(
                q_s.reshape(d_b, d_t * H_C, HD_C), win_k, win_v,
                k_s.reshape(d_b, d_t * H_C, HD_C), v_s.reshape(d_b, d_t * H_C, HD_C),
                j * d_b, d_t)
            xs = _matmul_res([(o_s.reshape(d_b * d_t, wc).astype(BF16), w_out_odd16, j, 0)], xs)
            ck_s.append(new_k.reshape(d_b, buf_len, H_C, HD_C))
            cv_s.append(new_v.reshape(d_b, buf_len, H_C, HD_C))

        mem_kv = _norm_proj(mem, norm_mem[l], w_xkv16, l, 2, wx, want_bf16=False)
        mk_p.append(mem_kv[0].reshape(1, n_mem, H_X, HD_X))
        mv_p.append(mem_kv[1].reshape(1, n_mem, H_X, HD_X))
        qx = _norm_proj(xp, norm_cross[l], w_xq16, l, 1, wx, want_f32=False)
        ox = _cross_attention(qx, mem_kv[0:1], mem_kv[1:2])
        xp = _matmul_res([(ox.reshape(s_len, wx), w_xo16, l, 0)], xp)
        qx_s = _norm_proj(xs, norm_cross[l], w_xq16, l, 1, wx, want_f32=False)
        ox_s = _cross_attention_rows(qx_s.reshape(d_b, d_t * H_X, HD_X), memc_k, memc_v, l * d_b)
        xs = _matmul_res([(ox_s.reshape(d_b * d_t, wx).astype(BF16), w_xo16, l, 0)], xs)

        final_gain = norm_final if l == depth - 1 else None
        xp = _ffn(xp, norm_ffn[l], w_gate_up16, w_down16, l, final_gain)
        xs = _ffn(xs, norm_ffn[l], w_gate_up16, w_down16, l, final_gain)

    y_prompt = xp.reshape(1, s_len, d_model)
    y_sample = xs.reshape(d_b, d_t, d_model)
    st = jnp.stack
    return (y_prompt, y_sample,
            st(ak_p), st(av_p), st(bk_p), st(bv_p), st(bl_p), st(ck_p), st(cv_p), st(mk_p), st(mv_p),
            st(ak_s), st(av_s), st(bk_s), st(bv_s), st(bl_s), st(ck_s), st(cv_s))
```

```python
import functools
import math

import numpy as np
import jax
import jax.numpy as jnp
from jax import lax
from jax.experimental import pallas as pl
from jax.experimental.pallas import tpu as pltpu

F32 = jnp.float32
BF16 = jnp.bfloat16

EPS = 1e-6
PAGE_SIZE = 128
H_A, HD_A = 4, 128
H_B, HD_B = 8, 128
H_C, HD_C = 16, 128
H_X, HD_X = 4, 128
C_PATTERNS = ((128, 1), (512, 4), (2048, 16))
C_WMAX = 2048
LANES = 128
MXU_TILE = 256
VMEM_LIMIT = 52 * 1024 * 1024
NEG_INF = float("-inf")
LOG2E = 1.4426950408889634


def _params(*sem):
    return pltpu.CompilerParams(dimension_semantics=sem, vmem_limit_bytes=VMEM_LIMIT)


def _tile_lanes(x, reps):
    return x if reps == 1 else jnp.concatenate([x] * reps, axis=1)


def _pick(n, pref):
    for t in pref:
        if n % t == 0:
            return t
    return n


def _mm_kernel(*refs, n_pairs):
    acc = refs[2 * n_pairs][...]
    for p in range(n_pairs):
        acc = acc + jnp.dot(refs[2 * p][...], refs[2 * p + 1][...], preferred_element_type=F32)
    refs[2 * n_pairs + 1][...] = acc


def _matmul_res(pairs, res):
    m = res.shape[0]
    n = res.shape[1]
    tm = _pick(m, (1024, 512, 256, 128))
    tn = _pick(n, (1024, 512, 256, 128))
    in_specs, args = [], []
    for a, w3, layer, row_blk in pairs:
        k = a.shape[1]
        in_specs += [pl.BlockSpec((tm, k), lambda i, j: (i, 0)),
                     pl.BlockSpec((None, k, tn),
                                  lambda i, j, layer=layer, row_blk=row_blk: (layer, row_blk, j))]
        args += [a, w3]
    in_specs.append(pl.BlockSpec((tm, tn), lambda i, j: (i, j)))
    return pl.pallas_call(
        functools.partial(_mm_kernel, n_pairs=len(pairs)),
        grid=(m // tm, n // tn),
        in_specs=in_specs,
        out_specs=pl.BlockSpec((tm, tn), lambda i, j: (i, j)),
        out_shape=jax.ShapeDtypeStruct((m, n), F32),
        compiler_params=_params("parallel", "parallel"),
        name="matmul_residual",
    )(*args, res)


def _proj_kernel(*refs, q_groups, q_scale, has_bf16, has_f32, has_gate):
    x_ref, gain_ref, w_ref = refs[:3]
    pos = 3
    wz_ref = ob_ref = of_ref = oz_ref = None
    if has_gate:
        wz_ref = refs[pos]
        pos += 1
    if has_bf16:
        ob_ref = refs[pos]
        pos += 1
    if has_f32:
        of_ref = refs[pos]
        pos += 1
    if has_gate:
        oz_ref = refs[pos]
        pos += 1
    xn_s = refs[pos]
    g = pl.program_id(1)

    @pl.when(g == 0)
    def _():
        x = x_ref[...]
        ms = jnp.mean(x * x, axis=-1, keepdims=True)
        xn_s[...] = ((x * lax.rsqrt(ms + EPS)) * gain_ref[...]).astype(BF16)
        if has_gate:
            oz_ref[...] = jnp.dot(xn_s[...], wz_ref[...], preferred_element_type=F32)

    y = jnp.dot(xn_s[...], w_ref[...], preferred_element_type=F32)
    is_q = None
    for qg in q_groups:
        is_q = (g == qg) if is_q is None else jnp.logical_or(is_q, g == qg)
    if has_bf16:
        ob_ref[...] = (y if is_q is None else y * jnp.where(is_q, q_scale, 1.0)).astype(BF16)
    if has_f32:
        if is_q is None:
            of_ref[...] = y
        else:
            @pl.when(jnp.logical_not(is_q))
            def _():
                of_ref[...] = y


def _norm_proj(x, gain, w3, layer, n_groups, gw, q_groups=(), q_scale=1.0,
               want_bf16=True, want_f32=True, w_gate=None):
    m, d = x.shape
    tm = _pick(m, (1024, 512, 256, 128))
    kv_groups = [g for g in range(n_groups) if g not in q_groups]
    assert not q_groups or max(q_groups) < max(kv_groups)

    def f32_slot(g):
        return sum(jnp.where(g > k, 1, 0) for k in kv_groups)

    in_specs = [pl.BlockSpec((tm, d), lambda i, g: (i, 0)),
                pl.BlockSpec((1, d), lambda i, g: (0, 0)),
                pl.BlockSpec((None, d, gw), lambda i, g: (layer, 0, g))]
    args = [x, gain.reshape(1, d).astype(F32), w3]
    if w_gate is not None:
        in_specs.append(pl.BlockSpec(w_gate.shape, lambda i, g: (0, 0)))
        args.append(w_gate)
    out_specs, out_shape = [], []
    if want_bf16:
        out_specs.append(pl.BlockSpec((None, tm, gw), lambda i, g: (g, i, 0)))
        out_shape.append(jax.ShapeDtypeStruct((n_groups, m, gw), BF16))
    if want_f32:
        out_specs.append(pl.BlockSpec((None, tm, gw), lambda i, g: (f32_slot(g), i, 0)))
        out_shape.append(jax.ShapeDtypeStruct((len(kv_groups), m, gw), F32))
    if w_gate is not None:
        out_specs.append(pl.BlockSpec((tm, w_gate.shape[1]), lambda i, g: (i, 0)))
        out_shape.append(jax.ShapeDtypeStruct((m, w_gate.shape[1]), F32))
    outs = pl.pallas_call(
        functools.partial(_proj_kernel, q_groups=tuple(q_groups), q_scale=q_scale,
                          has_bf16=want_bf16, has_f32=want_f32, has_gate=w_gate is not None),
        grid=(m // tm, n_groups),
        in_specs=in_specs, out_specs=out_specs, out_shape=out_shape,
        scratch_shapes=[pltpu.VMEM((tm, d), BF16)],
        compiler_params=_params("parallel", "arbitrary"),
        name="norm_projection",
    )(*args)
    return outs[0] if len(outs) == 1 else tuple(outs)


def _ffn_kernel(x_ref, gain_ref, wg_ref, wu_ref, wd_ref, *rest, final_norm):
    if final_norm:
        fgain_ref, o_ref, xn_s, h_s, acc_ref = rest
    else:
        o_ref, xn_s, h_s, acc_ref = rest
    f = pl.program_id(1)
    last = pl.num_programs(1) - 1

    def hidden():
        xn = xn_s[...]
        g = jnp.dot(xn, wg_ref[...], preferred_element_type=F32)
        u = jnp.dot(xn, wu_ref[...], preferred_element_type=F32)
        h_s[...] = ((g * jax.nn.sigmoid(g)) * u).astype(BF16)

    def down():
        acc_ref[...] += jnp.dot(h_s[...], wd_ref[...], preferred_element_type=F32)

    @pl.when(f == 0)
    def _():
        x = x_ref[...]
        ms = jnp.mean(x * x, axis=-1, keepdims=True)
        xn_s[...] = ((x * lax.rsqrt(ms + EPS)) * gain_ref[...]).astype(BF16)
        acc_ref[...] = jnp.zeros_like(acc_ref)
        hidden()

    @pl.when(jnp.logical_and(f > 0, f < last))
    def _():
        down()
        hidden()

    @pl.when(f == last)
    def _():
        down()
        y = x_ref[...] + acc_ref[...]
        if final_norm:
            ms = jnp.mean(y * y, axis=-1, keepdims=True)
            y = (y * lax.rsqrt(ms + EPS)) * fgain_ref[...]
        o_ref[...] = y


def _ffn(x, gain, w_gate_up3, w_down3, layer, final_gain=None):
    m, d = x.shape
    dff = w_down3.shape[1]
    tm = _pick(m, (512, 256, 128))
    tf = _pick(dff, (512, 256, 128))
    n_f = dff // tf
    vec = pl.BlockSpec((1, d), lambda i, f: (0, 0))
    up_blk = lambda f: jnp.minimum(f, n_f - 1)
    down_blk = lambda f: jnp.maximum(f - 1, 0)
    in_specs = [pl.BlockSpec((tm, d), lambda i, f: (i, 0)),
                vec,
                pl.BlockSpec((None, d, tf), lambda i, f: (layer, 0, up_blk(f))),
                pl.BlockSpec((None, d, tf), lambda i, f: (layer, 0, n_f + up_blk(f))),
                pl.BlockSpec((None, tf, d), lambda i, f: (layer, down_blk(f), 0))]
    args = [x, gain.reshape(1, d).astype(F32), w_gate_up3, w_gate_up3, w_down3]
    if final_gain is not None:
        in_specs.append(vec)
        args.append(final_gain.reshape(1, d).astype(F32))
    return pl.pallas_call(
        functools.partial(_ffn_kernel, final_norm=final_gain is not None),
        grid=(m // tm, n_f + 1),
        in_specs=in_specs,
        out_specs=pl.BlockSpec((tm, d), lambda i, f: (i, 0)),
        out_shape=jax.ShapeDtypeStruct((m, d), F32),
        scratch_shapes=[pltpu.VMEM((tm, d), BF16), pltpu.VMEM((tm, tf), BF16),
                        pltpu.VMEM((tm, d), F32)],
        compiler_params=_params("parallel", "arbitrary"),
        name="swiglu",
    )(*args)


def _log_sigmoid(x):
    return jnp.minimum(x, 0.0) - jnp.log1p(jnp.exp(-jnp.abs(x)))


def _logf_kernel(fz_ref, b_ref, lf_ref, c2_ref):
    lf = _log_sigmoid(fz_ref[...] + b_ref[...])
    lf_ref[...] = lf
    n = lf.shape[1]
    lane = lax.broadcasted_iota(jnp.int32, lf.shape, 1)
    c = lf
    shift = 1
    while shift < n:
        c = c + jnp.where(lane >= shift, pltpu.roll(c, shift, axis=1), 0.0)
        shift *= 2
    c2_ref[...] = c * LOG2E


def _logf_cumsum(fz_t, b_f):
    h, n = fz_t.shape
    return pl.pallas_call(
        _logf_kernel,
        out_shape=[jax.ShapeDtypeStruct((h, n), F32)] * 2,
        name="logf_cumsum",
    )(fz_t, b_f.reshape(h, 1).astype(F32))


def _online_update(u, v, m_s, l_s, acc_s, row_shift):
    reps = u.shape[1] // LANES
    m_prev = m_s[...]
    m_new = jnp.maximum(m_prev, jnp.max(u, axis=1, keepdims=True) + row_shift)
    alpha = jnp.exp2(m_prev - m_new)
    p = jnp.exp2(u - _tile_lanes(m_new - row_shift, reps))
    l_s[...] = alpha * l_s[...] + jnp.sum(p, axis=1, keepdims=True)
    acc_s[...] = (_tile_lanes(alpha, acc_s.shape[1] // LANES) * acc_s[...]
                  + jnp.dot(p.astype(BF16), v, preferred_element_type=F32))
    m_s[...] = m_new


def _causal_mask(u):
    row = lax.broadcasted_iota(jnp.int32, u.shape, 0)
    col = lax.broadcasted_iota(jnp.int32, u.shape, 1)
    return jnp.where(row >= col, u, NEG_INF)


def _causal_chunks(i, scores, update):
    def pair(jj, c):
        j0 = 2 * jj
        u0, u1 = scores(j0, False), scores(j0 + 1, False)
        update(u0, j0)
        update(u1, j0 + 1)
        return c

    lax.fori_loop(0, lax.div(i, 2), pair, 0)
    odd = lax.rem(i, 2) == 1

    @pl.when(odd)
    def _():
        u0, u1 = scores(i - 1, False), scores(i, True)
        update(u0, i - 1)
        update(u1, i)

    @pl.when(jnp.logical_not(odd))
    def _():
        update(scores(i, True), i)


def _fox_kernel(q_ref, k_ref, v_ref, ccol_ref, crow_ref, o_ref, m_s, l_s, acc_s, *, t):
    i = pl.program_id(1)
    q = q_ref[...]
    cq = jnp.broadcast_to(ccol_ref[...], (t, LANES))
    m_s[...] = jnp.full_like(m_s, NEG_INF)
    l_s[...] = jnp.zeros_like(l_s)
    acc_s[...] = jnp.zeros_like(acc_s)
    dn = (((1,), (1,)), ((), ()))

    def scores(j, masked):
        k = k_ref[pl.ds(pl.multiple_of(j * t, t), t), :]
        u = lax.dot_general(q, k, dn, preferred_element_type=F32) - crow_ref[j]
        return _causal_mask(u) if masked else u

    def update(u, j):
        v = v_ref[pl.ds(pl.multiple_of(j * t, t), t), :]
        _online_update(u, v, m_s, l_s, acc_s, cq)

    _causal_chunks(i, scores, update)
    o_ref[...] = (acc_s[...] / l_s[...]).astype(o_ref.dtype)


def _fox_attention(st, gq, gk, gv, c_t):
    s_len = st.shape[1]
    t = _pick(s_len, (512, 256, 128))
    nc = s_len // t
    c_col = c_t.reshape(H_B, s_len, 1)
    c_row = c_t.reshape(H_B, nc, 1, t)
    return pl.pallas_call(
        functools.partial(_fox_kernel, t=t),
        grid=(H_B, nc),
        in_specs=[pl.BlockSpec((None, t, HD_B), lambda h, i: (gq, i, h)),
                  pl.BlockSpec((None, s_len, HD_B), lambda h, i: (gk, 0, h)),
                  pl.BlockSpec((None, s_len, HD_B), lambda h, i: (gv, 0, h)),
                  pl.BlockSpec((None, t, 1), lambda h, i: (h, i, 0)),
                  pl.BlockSpec((None, nc, 1, t), lambda h, i: (h, 0, 0, 0))],
        out_specs=pl.BlockSpec((t, HD_B), lambda h, i: (i, h)),
        out_shape=jax.ShapeDtypeStruct((s_len, H_B * HD_B), BF16),
        scratch_shapes=[pltpu.VMEM((t, LANES), F32), pltpu.VMEM((t, LANES), F32),
                        pltpu.VMEM((t, HD_B), F32)],
        compiler_params=_params("parallel", "parallel"),
        name="forget_attention",
    )(st, st, st, c_col, c_row)


def _diff_lambda_vec(lq1_ref, lk1_ref, lq2_ref, lk2_ref, lam_init):
    a = jnp.sum(lq1_ref[...] * lk1_ref[...], axis=1, keepdims=True)
    b = jnp.sum(lq2_ref[...] * lk2_ref[...], axis=1, keepdims=True)
    return jnp.exp(a) - jnp.exp(b) + lam_init


def _diff_kernel(q_ref, k_ref, v_ref, slope_ref, lq1_ref, lk1_ref, lq2_ref, lk2_ref, g_ref,
                 o_ref, m1_s, l1_s, a1_s, m2_s, l2_s, a2_s, *, t, lam_init):
    i = pl.program_id(1)
    q1 = q_ref[:, :HD_A]
    q2 = q_ref[:, HD_A:]
    slope = slope_ref[...]
    row_pos = (i * t + lax.broadcasted_iota(jnp.int32, (t, LANES), 0)).astype(F32)
    row_shift = -(slope * row_pos)
    for m_s, l_s, a_s in ((m1_s, l1_s, a1_s), (m2_s, l2_s, a2_s)):
        m_s[...] = jnp.full_like(m_s, NEG_INF)
        l_s[...] = jnp.zeros_like(l_s)
        a_s[...] = jnp.zeros_like(a_s)
    reps = t // LANES
    dn = (((1,), (1,)), ((), ()))

    def scores(j, masked):
        k = k_ref[pl.ds(pl.multiple_of(j * t, t), t), :]
        col_pos = (j * t + lax.broadcasted_iota(jnp.int32, (1, t), 1)).astype(F32)
        col_term = _tile_lanes(slope, reps) * col_pos
        us = []
        for qm, km in ((q1, k[:, :HD_A]), (q2, k[:, HD_A:])):
            u = lax.dot_general(qm, km, dn, preferred_element_type=F32) + col_term
            us.append(_causal_mask(u) if masked else u)
        return us

    def update(us, j):
        v = v_ref[pl.ds(pl.multiple_of(j * t, t), t), :]
        _online_update(us[0], v, m1_s, l1_s, a1_s, row_shift)
        _online_update(us[1], v, m2_s, l2_s, a2_s, row_shift)

    _causal_chunks(i, scores, update)
    lam = _diff_lambda_vec(lq1_ref, lk1_ref, lq2_ref, lk2_ref, lam_init)
    o = (a1_s[...] / _tile_lanes(l1_s[...], 2)
         - lam * (a2_s[...] / _tile_lanes(l2_s[...], 2)))
    y = o * lax.rsqrt(jnp.mean(o * o, axis=1, keepdims=True) + EPS) * g_ref[...] * (1.0 - lam_init)
    o_ref[...] = y.astype(o_ref.dtype)


def _alibi_slopes(n):
    return jnp.asarray(2.0 ** (-8.0 * np.arange(1, n + 1) / n), dtype=F32)


def _diff_attention(st, gq, gk, gv, lam_params, g_sub, lam_init):
    s_len = st.shape[1]
    t = _pick(s_len, (512, 256, 128))
    nc = s_len // t
    w = 2 * HD_A
    slopes = jnp.broadcast_to((_alibi_slopes(H_A) * LOG2E)[:, None, None], (H_A, 1, LANES))
    vec = pl.BlockSpec((1, HD_A), lambda h, i: (0, 0))
    return pl.pallas_call(
        functools.partial(_diff_kernel, t=t, lam_init=lam_init),
        grid=(H_A, nc),
        in_specs=[pl.BlockSpec((None, t, w), lambda h, i: (gq, i, h)),
                  pl.BlockSpec((None, s_len, w), lambda h, i: (gk, 0, h)),
                  pl.BlockSpec((None, s_len, w), lambda h, i: (gv, 0, h)),
                  pl.BlockSpec((None, 1, LANES), lambda h, i: (h, 0, 0)),
                  vec, vec, vec, vec,
                  pl.BlockSpec((1, w), lambda h, i: (0, 0))],
        out_specs=pl.BlockSpec((t, w), lambda h, i: (i, h)),
        out_shape=jax.ShapeDtypeStruct((s_len, H_A * w), BF16),
        scratch_shapes=[pltpu.VMEM((t, LANES), F32), pltpu.VMEM((t, LANES), F32),
                        pltpu.VMEM((t, w), F32),
                        pltpu.VMEM((t, LANES), F32), pltpu.VMEM((t, LANES), F32),
                        pltpu.VMEM((t, w), F32)],
        compiler_params=_params("parallel", "parallel"),
        name="diff_attention",
    )(st, st, st, slopes, *lam_params, g_sub.reshape(1, w).astype(F32))


PAGE_ROWS = PAGE_SIZE * 8


def _even_sample_tables(n_t, n_new_cols):
    slopes = 2.0 ** (-8.0 * np.arange(1, H_A + 1) / H_A)
    ninf = -np.inf
    col = np.arange(PAGE_ROWS)
    p_col, j_col = col >> 3, col & 7
    ra = np.arange(2 * n_t * H_A)
    m_r, t_r, h_r = ra // (n_t * H_A), (ra // H_A) % n_t, ra % H_A
    j_r = 2 * h_r + m_r
    sl_r = slopes[h_r]
    ta = np.where(j_col[None, :] == j_r[:, None], sl_r[:, None] * p_col[None, :], ninf)
    rb = np.arange(n_t * H_B)
    tb_t, tb_h = rb // H_B, rb % H_B
    tb = np.where(j_col[None, :] == tb_h[:, None], 0.0, ninf)
    ncol = np.arange(n_new_cols)
    u_col, jn_col = ncol >> 3, ncol & 7
    ok_a = (jn_col[None, :] == j_r[:, None]) & (u_col[None, :] <= t_r[:, None])
    tan = np.where(ok_a, -sl_r[:, None] * (t_r[:, None] - u_col[None, :]), ninf)
    ok_b = (jn_col[None, :] == tb_h[:, None]) & (u_col[None, :] <= tb_t[:, None])
    tbn = np.where(ok_b, 0.0, ninf)
    rowa = np.stack([sl_r, t_r.astype(np.float64)], axis=1)
    lane = np.arange(MXU_TILE) & 7
    me = np.stack([(lane[None, :] == (4 * e + h_r)[:, None]).astype(np.float64) for e in range(2)])
    gi = np.arange(MXU_TILE) >> 3
    gt = (gi[:, None] == gi[None, :]).astype(np.float64)
    f = lambda a: jnp.asarray(a, F32)
    return f(ta), f(tb), f(tan), f(tbn), f(rowa), f(me), jnp.asarray(gt, BF16)


def _even_sample_kernel(pt_ref, qa_ref, qb_ref, lfr_ref, lfl_ref, kan_ref, van_ref, kbn_ref, vbn_ref,
                        ta_ref, tb_ref, tan_ref, tbn_ref, rowa_ref, me_ref, gt_ref,
                        lq1_ref, lk1_ref, lq2_ref, lk2_ref, gsub_ref, *rest,
                        n_pg, n_t, p_len, scale, lam_init):
    ka_refs = rest[0 * n_pg:1 * n_pg]
    va_refs = rest[1 * n_pg:2 * n_pg]
    kb_refs = rest[2 * n_pg:3 * n_pg]
    vb_refs = rest[3 * n_pg:4 * n_pg]
    lf_refs = rest[4 * n_pg:5 * n_pg]
    oa_ref, ob_ref = rest[5 * n_pg:5 * n_pg + 2]
    ma_s, la_s, acca_s, mb_s, lb_s, accb_s, carry_s = rest[5 * n_pg + 2:]
    j = pl.program_id(1)
    n_groups = pl.num_programs(1)
    n_ra = 2 * n_t * H_A
    dn = (((1,), (1,)), ((), ()))
    qa = qa_ref[...]
    qb = qb_ref[...]
    slope = rowa_ref[:, 0:1]
    t_row = rowa_ref[:, 1:2]
    gt = gt_ref[...]

    @pl.when(j == 0)
    def _():
        for m_s, l_s, a_s in ((ma_s, la_s, acca_s), (mb_s, lb_s, accb_s)):
            m_s[...] = jnp.full_like(m_s, NEG_INF)
            l_s[...] = jnp.zeros_like(l_s)
            a_s[...] = jnp.zeros_like(a_s)
        carry_s[...] = jnp.zeros_like(carry_s)

    def softmax_step(s, m_s, l_s):
        m_prev = m_s[...]
        m_new = jnp.maximum(m_prev, jnp.max(s, axis=1, keepdims=True))
        alpha = jnp.exp(m_prev - m_new)
        p = jnp.exp(s - m_new)
        l_s[...] = alpha * l_s[...] + jnp.sum(p, axis=1, keepdims=True)
        m_s[...] = m_new
        return p, alpha

    def spread(p):
        n_tiles = p.shape[1] // MXU_TILE
        pb = p.astype(BF16)
        stacked = jnp.concatenate(
            [pb[:, c * MXU_TILE:(c + 1) * MXU_TILE] for c in range(n_tiles)], axis=0)
        rep = jnp.dot(stacked, gt, preferred_element_type=F32)
        halves = []
        for e in range(2):
            me = me_ref[e]
            halves.append(jnp.concatenate(
                [rep[c * n_ra:(c + 1) * n_ra] * me for c in range(n_tiles)], axis=1))
        return jnp.concatenate(halves, axis=0).astype(BF16)

    def attend_a(s, vs):
        p, alpha = softmax_step(s, ma_s, la_s)
        p2 = spread(p)
        pv = None
        for g, v in enumerate(vs):
            rows = v.shape[0]
            y = jnp.dot(p2[:, g * rows:(g + 1) * rows], v, preferred_element_type=F32)
            pv = y if pv is None else pv + y
        acca_s[...] = jnp.concatenate([alpha, alpha], axis=0) * acca_s[...] + pv

    def attend_b(s, vs):
        p, alpha = softmax_step(s, mb_s, lb_s)
        pb = p.astype(BF16)
        pv = None
        for g, v in enumerate(vs):
            rows = v.shape[0]
            y = jnp.dot(pb[:, g * rows:(g + 1) * rows], v, preferred_element_type=F32)
            pv = y if pv is None else pv + y
        accb_s[...] = alpha * accb_s[...] + pv

    def scores(q, ks):
        return jnp.concatenate(
            [lax.dot_general(q, k, dn, preferred_element_type=F32) for k in ks], axis=1) * scale

    lfr = lfr_ref[...]
    parts = [lfr[0:H_B]]
    for t in range(1, n_t):
        parts.append(parts[-1] + lfr[t * H_B:(t + 1) * H_B])
    cn_col = jnp.concatenate(parts, axis=0)

    group = n_groups - 1 - j
    ta = ta_ref[...]
    bias_a = jnp.concatenate(
        [ta + slope * ((group * n_pg + g).astype(F32) * float(PAGE_SIZE) - (p_len + t_row))
         for g in range(n_pg)], axis=1)
    attend_a(scores(qa, [r[...].astype(BF16) for r in ka_refs]) + bias_a,
             [r[...].astype(BF16) for r in va_refs])

    lane = lax.broadcasted_iota(jnp.int32, (H_B, PAGE_ROWS), 1)
    tail = carry_s[...]
    sufs = [None] * n_pg
    for g in range(n_pg - 1, -1, -1):
        lf = lf_refs[g][...]
        inc = lf
        shift = 8
        while shift < PAGE_ROWS:
            inc = inc + jnp.where(lane < PAGE_ROWS - shift,
                                  pltpu.roll(inc, PAGE_ROWS - shift, axis=1), 0.0)
            shift *= 2
        sufs[g] = (inc - lf) + tail
        tail = tail + inc[:, 0:1]
    carry_s[...] = tail
    suf = jnp.concatenate(sufs, axis=1)
    bias_b = (jnp.concatenate([suf] * n_t, axis=0) + cn_col
              + jnp.concatenate([tb_ref[...]] * n_pg, axis=1))
    attend_b(scores(qb, [r[...].astype(BF16) for r in kb_refs]) + bias_b,
             [r[...].astype(BF16) for r in vb_refs])

    @pl.when(j == n_groups - 1)
    def _():
        attend_a(scores(qa, [kan_ref[...]]) + tan_ref[...], [van_ref[...]])
        lfl = lfl_ref[...]
        ln = lax.broadcasted_iota(jnp.int32, lfl.shape, 1)
        cn_lane = lfl
        shift = H_B
        while shift < n_t * H_B:
            cn_lane = cn_lane + jnp.where(ln >= shift, pltpu.roll(cn_lane, shift, axis=1), 0.0)
            shift *= 2
        n_new = tbn_ref.shape[1]
        cn_keys = _tile_lanes(cn_lane, n_new // LANES)
        attend_b(scores(qb, [kbn_ref[...]]) + (tbn_ref[...] + (cn_col - cn_keys)), [vbn_ref[...]])

        lam = _diff_lambda_vec(lq1_ref, lk1_ref, lq2_ref, lk2_ref, lam_init)
        la = la_s[...]
        fa = acca_s[...] / jnp.concatenate([la, la], axis=0)
        half = n_ra // 2
        o = [fa[e * n_ra:e * n_ra + half] - lam * fa[e * n_ra + half:(e + 1) * n_ra]
             for e in range(2)]
        ms = (jnp.sum(o[0] * o[0], axis=1, keepdims=True)
              + jnp.sum(o[1] * o[1], axis=1, keepdims=True)) / (2.0 * HD_A)
        inv = lax.rsqrt(ms + EPS)
        for e in range(2):
            oa_ref[e] = o[e] * inv * gsub_ref[:, e * HD_A:(e + 1) * HD_A] * (1.0 - lam_init)
        ob_ref[...] = accb_s[...] / lb_s[...]


def _even_sample_attention(qa2, qb2, lf_rows, lf_lanes, new_kv, pools, page_table, pool_off,
                           lam_params, g_sub, lam_init):
    pool_ak, pool_av, pool_bk, pool_bv, pool_lf = pools
    bsz, n_ra, hd = qa2.shape
    n_rb = qb2.shape[1]
    n_t = n_rb // H_B
    n_new = new_kv[0].shape[1]
    n_pages = page_table.shape[1]
    n_pg = _pick(n_pages, (8, 4, 2, 1))
    n_groups = n_pages // n_pg
    tables = _even_sample_tables(n_t, n_new)

    def page_map(g):
        def index(b, j, pt):
            return (pool_off + pt[b * n_pages + (n_groups - 1 - j) * n_pg + g], 0, 0)
        return index

    per_b = lambda b, j, pt: (b, 0, 0)
    const2 = lambda b, j, pt: (0, 0)
    const3 = lambda b, j, pt: (0, 0, 0)
    vec = pl.BlockSpec((1, HD_A), const2)
    in_specs = [pl.BlockSpec((None, n_ra, hd), per_b),
                pl.BlockSpec((None, n_rb, hd), per_b),
                pl.BlockSpec((None, n_rb, 1), per_b),
                pl.BlockSpec((None, 1, LANES), per_b)]
    in_specs += [pl.BlockSpec((None, n_new, hd), per_b)] * 4
    in_specs += [pl.BlockSpec(tables[0].shape, const2), pl.BlockSpec(tables[1].shape, const2),
                 pl.BlockSpec(tables[2].shape, const2), pl.BlockSpec(tables[3].shape, const2),
                 pl.BlockSpec(tables[4].shape, const2), pl.BlockSpec(tables[5].shape, const3),
                 pl.BlockSpec(tables[6].shape, const2)]
    in_specs += [vec, vec, vec, vec, pl.BlockSpec((1, 2 * HD_A), const2)]
    args = [qa2, qb2, lf_rows, lf_lanes, *new_kv, *tables, *lam_params,
            g_sub.reshape(1, 2 * HD_A).astype(F32)]
    for pool in (pool_ak, pool_av, pool_bk, pool_bv):
        for g in range(n_pg):
            in_specs.append(pl.BlockSpec((None, PAGE_ROWS, hd), page_map(g)))
            args.append(pool)
    for g in range(n_pg):
        in_specs.append(pl.BlockSpec((None, H_B, PAGE_ROWS), page_map(g)))
        args.append(pool_lf)
    grid_spec = pltpu.PrefetchScalarGridSpec(
        num_scalar_prefetch=1,
        grid=(bsz, n_groups),
        in_specs=in_specs,
        out_specs=[pl.BlockSpec((None, 2, n_ra // 2, hd), lambda b, j, pt: (b, 0, 0, 0)),
                   pl.BlockSpec((None, n_rb, hd), per_b)],
        scratch_shapes=[pltpu.VMEM((n_ra, 1), F32), pltpu.VMEM((n_ra, 1), F32),
                        pltpu.VMEM((2 * n_ra, hd), F32),
                        pltpu.VMEM((n_rb, 1), F32), pltpu.VMEM((n_rb, 1), F32),
                        pltpu.VMEM((n_rb, hd), F32),
                        pltpu.VMEM((H_B, 1), F32)])
    return pl.pallas_call(
        functools.partial(_even_sample_kernel, n_pg=n_pg, n_t=n_t,
                          p_len=float(n_pages * PAGE_SIZE), scale=HD_A ** -0.5,
                          lam_init=lam_init),
        grid_spec=grid_spec,
        out_shape=[jax.ShapeDtypeStruct((bsz, 2, n_ra // 2, hd), F32),
                   jax.ShapeDtypeStruct((bsz, n_rb, hd), F32)],
        compiler_params=_params("parallel", "arbitrary"),
        name="even_sample_attention",
    )(page_table.reshape(-1), *args)


def _pattern_count(dist):
    return sum(((dist >= 0) & (dist <= w) & (dist % d == 0)).astype(jnp.int32)
               for w, d in C_PATTERNS)


def _count_bias(dist, slopes, same_head=True):
    cnt = _pattern_count(dist)
    logc = jnp.where((cnt > 0) & same_head, jnp.log(jnp.maximum(cnt, 1).astype(F32)), NEG_INF)
    return logc - slopes * dist.astype(F32)


def _dilated_bias(t, n_off):
    idx = jnp.arange(t, dtype=jnp.int32)
    dist = (idx[None, :, None] - idx[None, None, :]
            + t * jnp.arange(n_off, dtype=jnp.int32)[:, None, None])
    return _count_bias(dist[None], _alibi_slopes(H_C)[:, None, None, None]) * LOG2E


def _dil_kernel(q_ref, k_ref, v_ref, bias_ref, o_ref, m_s, l_s, acc_s, *, t, n_off):
    i = pl.program_id(1)
    q = q_ref[...]
    m_s[...] = jnp.full_like(m_s, NEG_INF)
    l_s[...] = jnp.zeros_like(l_s)
    acc_s[...] = jnp.zeros_like(acc_s)
    dn = (((1,), (1,)), ((), ()))
    no_shift = jnp.zeros((t, LANES), F32)

    def scores(off):
        k = k_ref[pl.ds(pl.multiple_of((i - off) * t, t), t), :]
        return lax.dot_general(q, k, dn, preferred_element_type=F32) + bias_ref[off]

    def update(u, off):
        v = v_ref[pl.ds(pl.multiple_of((i - off) * t, t), t), :]
        _online_update(u, v, m_s, l_s, acc_s, no_shift)

    @pl.when(i >= n_off - 1)
    def _():
        us = [scores(off) for off in range(n_off)]
        for off in range(n_off):
            update(us[off], off)

    @pl.when(i < n_off - 1)
    def _():
        for off in range(n_off - 1):
            @pl.when(i >= off)
            def _(off=off):
                update(scores(off), off)

    o_ref[...] = (acc_s[...] / l_s[...]).astype(o_ref.dtype)


def _dilated_attention(st):
    s_len = st.shape[1]
    per = st.shape[2] // HD_C
    t = _pick(s_len, (512, 256, 128))
    n_off = min(C_WMAX // t + 1, s_len // t)
    bias = _dilated_bias(t, n_off)
    return pl.pallas_call(
        functools.partial(_dil_kernel, t=t, n_off=n_off),
        grid=(H_C, s_len // t),
        in_specs=[pl.BlockSpec((None, t, HD_C), lambda h, i: (h // per, i, h % per)),
                  pl.BlockSpec((None, s_len, HD_C), lambda h, i: (2 + h // per, 0, h % per)),
                  pl.BlockSpec((None, s_len, HD_C), lambda h, i: (4 + h // per, 0, h % per)),
                  pl.BlockSpec((None, n_off, t, t), lambda h, i: (h, 0, 0, 0))],
        out_specs=pl.BlockSpec((t, HD_C), lambda h, i: (i, h)),
        out_shape=jax.ShapeDtypeStruct((s_len, H_C * HD_C), BF16),
        scratch_shapes=[pltpu.VMEM((t, LANES), F32), pltpu.VMEM((t, LANES), F32),
                        pltpu.VMEM((t, HD_C), F32)],
        compiler_params=_params("parallel", "parallel"),
        name="dilated_attention",
    )(st, st, st, bias)


def _c_sample_bias(buf_len, n_t, n_new_cols):
    row = jnp.arange(n_t * H_C, dtype=jnp.int32)[:, None]
    t_row, h_row = row // H_C, row % H_C
    slopes = jnp.tile(_alibi_slopes(H_C), n_t)[:, None]

    def table(n_pos, pos0):
        col = jnp.arange(n_pos * H_C, dtype=jnp.int32)[None, :]
        dist = buf_len + t_row - (pos0 + col // H_C)
        return _count_bias(dist, slopes, h_row == col % H_C)

    past = table(buf_len, 0)
    new = table(n_t, buf_len)
    new = jnp.pad(new, ((0, 0), (0, n_new_cols - new.shape[1])), constant_values=NEG_INF)
    return past, new


def _c_sample_kernel(q_ref, kc_ref, kx_ref, kn_ref, vc_ref, vx_ref, vn_ref, bias_ref, biasn_ref,
                     ok_ref, ov_ref, o_ref, m_s, l_s, acc_s, *, rows, new, scale):
    c = pl.program_id(1)
    last = pl.num_programs(1) - 1
    dn = (((1,), (1,)), ((), ()))
    q = q_ref[...]

    @pl.when(c == 0)
    def _():
        m_s[...] = jnp.full_like(m_s, NEG_INF)
        l_s[...] = jnp.zeros_like(l_s)
        acc_s[...] = jnp.zeros_like(acc_s)

    for src, nxt, fresh, dst in ((kc_ref, kx_ref, kn_ref, ok_ref), (vc_ref, vx_ref, vn_ref, ov_ref)):
        dst[pl.ds(0, rows - new), :] = src[pl.ds(new, rows - new), :]

        @pl.when(c < last)
        def _(nxt=nxt, dst=dst):
            dst[pl.ds(rows - new, new), :] = nxt[...]

        @pl.when(c == last)
        def _(fresh=fresh, dst=dst):
            dst[pl.ds(rows - new, new), :] = fresh[...]

    def attend(s, v):
        m_prev = m_s[...]
        m_new = jnp.maximum(m_prev, jnp.max(s, axis=1, keepdims=True))
        alpha = jnp.exp(m_prev - m_new)
        p = jnp.exp(s - m_new)
        l_s[...] = alpha * l_s[...] + jnp.sum(p, axis=1, keepdims=True)
        acc_s[...] = alpha * acc_s[...] + jnp.dot(p.astype(BF16), v, preferred_element_type=F32)
        m_s[...] = m_new

    s = lax.dot_general(q, kc_ref[...].astype(BF16), dn, preferred_element_type=F32)
    attend(s * scale + bias_ref[...], vc_ref[...].astype(BF16))

    @pl.when(c == last)
    def _():
        pad = jnp.zeros((biasn_ref.shape[1] - new, kn_ref.shape[1]), F32)
        kn = jnp.concatenate([kn_ref[...], pad], axis=0).astype(BF16)
        vn = jnp.concatenate([vn_ref[...], pad], axis=0).astype(BF16)
        s_n = lax.dot_general(q, kn, dn, preferred_element_type=F32)
        attend(s_n * scale + biasn_ref[...], vn)
        o_ref[...] = acc_s[...] / l_s[...]


def _c_sample(q2, cache_k, cache_v, k_new, v_new, batch_off, n_t):
    bsz, n_r, hd = q2.shape
    total = cache_k.shape[1]
    new = k_new.shape[1]
    buf_len = total // H_C
    rows = _pick(total, (8192, 4096, 2048, 1024))
    assert rows % new == 0 and total % rows == 0
    n_chunks = total // rows
    n_new_cols = max(LANES, new)
    bias, bias_n = _c_sample_bias(buf_len, n_t, n_new_cols)
    blocks_per_chunk = rows // new
    n_small = total // new

    cur = pl.BlockSpec((None, rows, hd), lambda b, c: (batch_off + b, c, 0))
    nxt = pl.BlockSpec((None, new, hd),
                       lambda b, c: (batch_off + b, jnp.minimum((c + 1) * blocks_per_chunk,
                                                                n_small - 1), 0))
    per_b = pl.BlockSpec((None, new, hd), lambda b, c: (b, 0, 0))
    out_buf = pl.BlockSpec((None, rows, hd), lambda b, c: (b, c, 0))
    return pl.pallas_call(
        functools.partial(_c_sample_kernel, rows=rows, new=new, scale=HD_C ** -0.5),
        grid=(bsz, n_chunks),
        in_specs=[pl.BlockSpec((None, n_r, hd), lambda b, c: (b, 0, 0)),
                  cur, nxt, per_b, cur, nxt, per_b,
                  pl.BlockSpec((n_r, rows), lambda b, c: (0, c)),
                  pl.BlockSpec((n_r, n_new_cols), lambda b, c: (0, 0))],
        out_specs=[out_buf, out_buf, pl.BlockSpec((None, n_r, hd), lambda b, c: (b, 0, 0))],
        out_shape=[jax.ShapeDtypeStruct((bsz, total, hd), F32),
                   jax.ShapeDtypeStruct((bsz, total, hd), F32),
                   jax.ShapeDtypeStruct((bsz, n_r, hd), F32)],
        scratch_shapes=[pltpu.VMEM((n_r, 1), F32), pltpu.VMEM((n_r, 1), F32),
                        pltpu.VMEM((n_r, hd), F32)],
        compiler_params=_params("parallel", "arbitrary"),
        name="window_sample_attention",
    )(q2, cache_k, cache_k, k_new, cache_v, cache_v, v_new, bias, bias_n)


def _cross_kernel(q_ref, k_ref, v_ref, o_ref, *, scale):
    dn = (((1,), (1,)), ((), ()))
    for h in range(H_X):
        cols = slice(h * HD_X, (h + 1) * HD_X)
        s = lax.dot_general(q_ref[:, cols], k_ref[:, cols].astype(BF16), dn,
                            preferred_element_type=F32) * scale
        m = jnp.max(s, axis=1, keepdims=True)
        e = jnp.exp(s - m)
        l = jnp.sum(e, axis=1, keepdims=True)
        o = jnp.dot(e.astype(BF16), v_ref[:, cols].astype(BF16), preferred_element_type=F32)
        o_ref[:, cols] = (o / l).astype(o_ref.dtype)


def _cross_attention(q, mem_k, mem_v):
    bsz, n_q, w = q.shape
    n_m = mem_k.shape[1]
    tq = _pick(n_q, (512, 256, 128))
    return pl.pallas_call(
        functools.partial(_cross_kernel, scale=HD_X ** -0.5),
        grid=(bsz, n_q // tq),
        in_specs=[pl.BlockSpec((None, tq, w), lambda b, i: (b, i, 0)),
                  pl.BlockSpec((None, n_m, w), lambda b, i: (b, 0, 0)),
                  pl.BlockSpec((None, n_m, w), lambda b, i: (b, 0, 0))],
        out_specs=pl.BlockSpec((None, tq, w), lambda b, i: (b, i, 0)),
        out_shape=jax.ShapeDtypeStruct((bsz, n_q, w), BF16),
        compiler_params=_params("parallel", "parallel"),
        name="cross_attention",
    )(q, mem_k, mem_v)


def _cross_rows_kernel(q_ref, k_ref, v_ref, mask_ref, o_ref, *, scale):
    dn = (((1,), (1,)), ((), ()))
    s = lax.dot_general(q_ref[...], k_ref[...].astype(BF16), dn,
                        preferred_element_type=F32) * scale + mask_ref[...]
    m = jnp.max(s, axis=1, keepdims=True)
    e = jnp.exp(s - m)
    l = jnp.sum(e, axis=1, keepdims=True)
    o = jnp.dot(e.astype(BF16), v_ref[...].astype(BF16), preferred_element_type=F32)
    o_ref[...] = o / l


def _cross_attention_rows(q2, mem_k, mem_v, batch_off):
    bsz, n_r, hd = q2.shape
    n_c = mem_k.shape[1]
    same = (np.arange(n_r)[:, None] % H_X) == (np.arange(n_c)[None, :] % H_X)
    mask = jnp.asarray(np.where(same, 0.0, -np.inf), F32)
    kv = pl.BlockSpec((None, n_c, hd), lambda b: (batch_off + b, 0, 0))
    return pl.pallas_call(
        functools.partial(_cross_rows_kernel, scale=HD_X ** -0.5),
        grid=(bsz,),
        in_specs=[pl.BlockSpec((None, n_r, hd), lambda b: (b, 0, 0)), kv, kv,
                  pl.BlockSpec((n_r, n_c), lambda b: (0, 0))],
        out_specs=pl.BlockSpec((None, n_r, hd), lambda b: (b, 0, 0)),
        out_shape=jax.ShapeDtypeStruct((bsz, n_r, hd), F32),
        compiler_params=_params("parallel"),
        name="cross_attention_rows",
    )(q2, mem_k, mem_v, mask)


NEW_KEY_ROWS = MXU_TILE


def _pad_rows(x, rows):
    return jnp.pad(x, ((0, 0), (0, rows - x.shape[1]), (0, 0)))


def kernel(x_prompt, x_sample, mem_prompt, cache_a_k, cache_a_v, cache_b_k, cache_b_v, cache_b_logf, cache_c_k, cache_c_v, cache_mem_k, cache_mem_v, page_table, norm_mix, norm_cross, norm_mem, norm_ffn, norm_final, w_in_even, b_forget, lambda_q1, lambda_k1, lambda_q2, lambda_k2, subln_a, w_out_even, w_in_odd, w_out_odd, w_xq, w_xkv, w_xo, w_gate_up, w_down):
    n_b, s_len, d_model = x_prompt.shape
    d_b, d_t, _ = x_sample.shape
    assert n_b == 1
    depth = norm_mix.shape[0]
    n_mem = mem_prompt.shape[1]
    xp = x_prompt.reshape(s_len, d_model)
    xs = x_sample.reshape(d_b * d_t, d_model)
    mem = mem_prompt.reshape(n_mem, d_model)
    wa = H_A * 2 * HD_A
    wb = H_B * HD_B
    wc = H_C * HD_C
    wx = H_X * HD_X

    n_even, n_pool = cache_a_k.shape[:2]
    pool_ak = cache_a_k.reshape(n_even * n_pool, PAGE_ROWS, HD_A)
    pool_av = (cache_a_v.reshape(n_even, n_pool, PAGE_SIZE, H_A, 2, HD_A)
               .transpose(0, 1, 2, 4, 3, 5).reshape(n_even * n_pool, PAGE_ROWS, HD_A))
    pool_bk = cache_b_k.reshape(n_even * n_pool, PAGE_ROWS, HD_B)
    pool_bv = cache_b_v.reshape(n_even * n_pool, PAGE_ROWS, HD_B)
    pool_lf = jnp.repeat(jnp.swapaxes(cache_b_logf, 2, 3), H_B, axis=3).reshape(
        n_even * n_pool, H_B, PAGE_ROWS)
    n_odd, _, buf_len = cache_c_k.shape[:3]
    win_k = cache_c_k.reshape(n_odd * d_b, buf_len * H_C, HD_C)
    win_v = cache_c_v.reshape(n_odd * d_b, buf_len * H_C, HD_C)
    memc_k = cache_mem_k.reshape(depth * d_b, n_mem * H_X, HD_X)
    memc_v = cache_mem_v.reshape(depth * d_b, n_mem * H_X, HD_X)

    ak_p, av_p, bk_p, bv_p, bl_p, ck_p, cv_p, mk_p, mv_p = [], [], [], [], [], [], [], [], []
    ak_s, av_s, bk_s, bv_s, bl_s, ck_s, cv_s = [], [], [], [], [], [], []

    w_in_even16, w_out_even16 = w_in_even.astype(BF16), w_out_even.astype(BF16)
    w_in_odd16, w_out_odd16 = w_in_odd.astype(BF16), w_out_odd.astype(BF16)
    w_xq16, w_xkv16, w_xo16 = w_xq.astype(BF16), w_xkv.astype(BF16), w_xo.astype(BF16)
    w_gate_up16, w_down16 = w_gate_up.astype(BF16), w_down.astype(BF16)
    gw = 1024

    for l in range(depth):
        j = l // 2
        if l % 2 == 0:
            lam_init = 0.8 - 0.6 * math.exp(-0.3 * l)
            w_fz = jnp.pad(w_in_even[j][:, 6 * wa:], ((0, 0), (0, LANES - H_B))).astype(BF16)
            lam_params = [p[j].reshape(1, HD_A).astype(F32)
                          for p in (lambda_q1, lambda_k1, lambda_q2, lambda_k2)]
            assert wa == gw and wb == gw and HD_A == HD_B
            proj = functools.partial(_norm_proj, gain=norm_mix[l], w3=w_in_even16, layer=j,
                                     n_groups=6, gw=gw, q_groups=(0, 3), w_gate=w_fz)

            st16, st32, fz = proj(xp, q_scale=HD_A ** -0.5 * LOG2E)
            lf_t, c_t = _logf_cumsum(fz[:, :H_B].T, b_forget[j])
            za = _diff_attention(st16, 0, 1, 2, lam_params, subln_a[j], lam_init)
            zb = _fox_attention(st16, 3, 4, 5, c_t)
            xp = _matmul_res([(za, w_out_even16, j, 0), (zb, w_out_even16, j, 1)], xp)
            ak_p.append(st32[0].reshape(1, s_len, H_A, 2, HD_A))
            av_p.append(st32[1].reshape(1, s_len, H_A, 2 * HD_A))
            bk_p.append(st32[2].reshape(1, s_len, H_B, HD_B))
            bv_p.append(st32[3].reshape(1, s_len, H_B, HD_B))
            bl_p.append(lf_t.T.reshape(1, s_len, H_B))

            ss16, ss32, fz_s = proj(xs)
            qa_s, ka_s16, va_s16, qb_s, kb_s16, vb_s16 = (ss16[g] for g in range(6))
            ka_s, va_s, kb_s, vb_s = (ss32[g] for g in range(4))
            lf_s_t, _ = _logf_cumsum(fz_s[:, :H_B].T, b_forget[j])
            lf_s = lf_s_t.T
            qa2 = (qa_s.reshape(d_b, d_t, H_A, 2, HD_A).transpose(0, 3, 1, 2, 4)
                   .reshape(d_b, 2 * d_t * H_A, HD_A))
            qb2 = qb_s.reshape(d_b, d_t * H_B, HD_B)
            va_rows = (va_s16.reshape(d_b, d_t, H_A, 2, HD_A).transpose(0, 1, 3, 2, 4)
                       .reshape(d_b, d_t * 8, HD_A))
            new_kv = [_pad_rows(a, NEW_KEY_ROWS)
                      for a in (ka_s16.reshape(d_b, d_t * 8, HD_A), va_rows,
                                kb_s16.reshape(d_b, d_t * H_B, HD_B),
                                vb_s16.reshape(d_b, d_t * H_B, HD_B))]
            lf_rows = lf_s.reshape(d_b, d_t * H_B, 1)
            lf_lanes = _pad_rows(lf_rows, LANES).reshape(d_b, 1, LANES)
            oa_s, ob_s = _even_sample_attention(
                qa2, qb2, lf_rows, lf_lanes, new_kv,
                (pool_ak, pool_av, pool_bk, pool_bv, pool_lf), page_table, j * n_pool,
                lam_params, subln_a[j], lam_init)
            za_s = (oa_s.reshape(d_b, 2, d_t, H_A, HD_A).transpose(0, 2, 3, 1, 4)
                    .reshape(d_b * d_t, wa).astype(BF16))
            zb_s = ob_s.reshape(d_b * d_t, wb).astype(BF16)
            xs = _matmul_res([(za_s, w_out_even16, j, 0), (zb_s, w_out_even16, j, 1)], xs)
            ak_s.append(ka_s.reshape(d_b, d_t, H_A, 2, HD_A))
            av_s.append(va_s.reshape(d_b, d_t, H_A, 2 * HD_A))
            bk_s.append(kb_s.reshape(d_b, d_t, H_B, HD_B))
            bv_s.append(vb_s.reshape(d_b, d_t, H_B, HD_B))
            bl_s.append(lf_s.reshape(d_b, d_t, H_B))
        else:
            assert wc == 2 * gw
            proj = functools.partial(_norm_proj, gain=norm_mix[l], w3=w_in_odd16, layer=j,
                                     n_groups=6, gw=gw, q_groups=(0, 1))
            halves = lambda s32, a: jnp.concatenate([s32[a], s32[a + 1]], axis=1)

            st16, st32 = proj(xp, q_scale=HD_C ** -0.5 * LOG2E)
            o = _dilated_attention(st16)
            xp = _matmul_res([(o, w_out_odd16, j, 0)], xp)
            keep = min(C_WMAX, s_len)
            ck_p.append(halves(st32[:, s_len - keep:], 0).reshape(1, keep, H_C, HD_C))
            cv_p.append(halves(st32[:, s_len - keep:], 2).reshape(1, keep, H_C, HD_C))

            ss16, ss32 = proj(xs)
            q_s = jnp.concatenate([ss16[0], ss16[1]], axis=1)
            k_s, v_s = halves(ss32, 0), halves(ss32, 2)
            new_k, new_v, o_s = _c_sample(
                q_s.reshape(d_b, d_t * H_C, HD_C), win_k, win_v,
                k_s.reshape(d_b, d_t * H_C, HD_C), v_s.reshape(d_b, d_t * H_C, HD_C),
                j * d_b, d_t)
            xs = _matmul_res([(o_s.reshape(d_b * d_t, wc).astype(BF16), w_out_odd16, j, 0)], xs)
            ck_s.append(new_k.reshape(d_b, buf_len, H_C, HD_C))
            cv_s.append(new_v.reshape(d_b, buf_len, H_C, HD_C))

        mem_kv = _norm_proj(mem, norm_mem[l], w_xkv16, l, 2, wx, want_bf16=False)
        mk_p.append(mem_kv[0].reshape(1, n_mem, H_X, HD_X))
        mv_p.append(mem_kv[1].reshape(1, n_mem, H_X, HD_X))
        qx = _norm_proj(xp, norm_cross[l], w_xq16, l, 1, wx, want_f32=False)
        ox = _cross_attention(qx, mem_kv[0:1], mem_kv[1:2])
        xp = _matmul_res([(ox.reshape(s_len, wx), w_xo16, l, 0)], xp)
        qx_s = _norm_proj(xs, norm_cross[l], w_xq16, l, 1, wx, want_f32=False)
        ox_s = _cross_attention_rows(qx_s.reshape(d_b, d_t * H_X, HD_X), memc_k, memc_v, l * d_b)
        xs = _matmul_res([(ox_s.reshape(d_b * d_t, wx).astype(BF16), w_xo16, l, 0)], xs)

        final_gain = norm_final if l == depth - 1 else None
        xp = _ffn(xp, norm_ffn[l], w_gate_up16, w_down16, l, final_gain)
        xs = _ffn(xs, norm_ffn[l], w_gate_up16, w_down16, l, final_gain)

    y_prompt = xp.reshape(1, s_len, d_model)
    y_sample = xs.reshape(d_b, d_t, d_model)
    st = jnp.stack
    return (y_prompt, y_sample,
            st(ak_p), st(av_p), st(bk_p), st(bv_p), st(bl_p), st(ck_p), st(cv_p), st(mk_p), st(mv_p),
            st(ak_s), st(av_s), st(bk_s), st(bv_s), st(bl_s), st(ck_s), st(cv_s))
```

```python
import functools
import math

import numpy as np
import jax
import jax.numpy as jnp
from jax import lax
from jax.experimental import pallas as pl
from jax.experimental.pallas import tpu as pltpu

F32 = jnp.float32
BF16 = jnp.bfloat16

EPS = 1e-6
PAGE_SIZE = 128
H_A, HD_A = 4, 128
H_B, HD_B = 8, 128
H_C, HD_C = 16, 128
H_X, HD_X = 4, 128
C_PATTERNS = ((128, 1), (512, 4), (2048, 16))
C_WMAX = 2048
LANES = 128
MXU_TILE = 256
VMEM_LIMIT = 52 * 1024 * 1024
NEG_INF = float("-inf")
LOG2E = 1.4426950408889634


def _params(*sem):
    return pltpu.CompilerParams(dimension_semantics=sem, vmem_limit_bytes=VMEM_LIMIT)


def _tile_lanes(x, reps):
    return x if reps == 1 else jnp.concatenate([x] * reps, axis=1)


def _pick(n, pref):
    for t in pref:
        if n % t == 0:
            return t
    return n


def _mm_kernel(*refs, n_pairs):
    acc = refs[2 * n_pairs][...]
    for p in range(n_pairs):
        acc = acc + jnp.dot(refs[2 * p][...], refs[2 * p + 1][...], preferred_element_type=F32)
    refs[2 * n_pairs + 1][...] = acc


def _matmul_res(pairs, res):
    m = res.shape[0]
    n = res.shape[1]
    tm = _pick(m, (1024, 512, 256, 128))
    tn = _pick(n, (1024, 512, 256, 128))
    in_specs, args = [], []
    for a, w3, layer, row_blk in pairs:
        k = a.shape[1]
        in_specs += [pl.BlockSpec((tm, k), lambda i, j: (i, 0)),
                     pl.BlockSpec((None, k, tn),
                                  lambda i, j, layer=layer, row_blk=row_blk: (layer, row_blk, j))]
        args += [a, w3]
    in_specs.append(pl.BlockSpec((tm, tn), lambda i, j: (i, j)))
    return pl.pallas_call(
        functools.partial(_mm_kernel, n_pairs=len(pairs)),
        grid=(m // tm, n // tn),
        in_specs=in_specs,
        out_specs=pl.BlockSpec((tm, tn), lambda i, j: (i, j)),
        out_shape=jax.ShapeDtypeStruct((m, n), F32),
        compiler_params=_params("parallel", "parallel"),
        name="matmul_residual",
    )(*args, res)


def _proj_kernel(*refs, q_groups, q_scale, has_bf16, has_f32, has_gate):
    x_ref, gain_ref, w_ref = refs[:3]
    pos = 3
    wz_ref = ob_ref = of_ref = oz_ref = None
    if has_gate:
        wz_ref = refs[pos]
        pos += 1
    if has_bf16:
        ob_ref = refs[pos]
        pos += 1
    if has_f32:
        of_ref = refs[pos]
        pos += 1
    if has_gate:
        oz_ref = refs[pos]
        pos += 1
    xn_s = refs[pos]
    g = pl.program_id(1)

    @pl.when(g == 0)
    def _():
        x = x_ref[...]
        ms = jnp.mean(x * x, axis=-1, keepdims=True)
        xn_s[...] = ((x * lax.rsqrt(ms + EPS)) * gain_ref[...]).astype(BF16)
        if has_gate:
            oz_ref[...] = jnp.dot(xn_s[...], wz_ref[...], preferred_element_type=F32)

    y = jnp.dot(xn_s[...], w_ref[...], preferred_element_type=F32)
    is_q = None
    for qg in q_groups:
        is_q = (g == qg) if is_q is None else jnp.logical_or(is_q, g == qg)
    if has_bf16:
        ob_ref[...] = (y if is_q is None else y * jnp.where(is_q, q_scale, 1.0)).astype(BF16)
    if has_f32:
        if is_q is None:
            of_ref[...] = y
        else:
            @pl.when(jnp.logical_not(is_q))
            def _():
                of_ref[...] = y


def _norm_proj(x, gain, w3, layer, n_groups, gw, q_groups=(), q_scale=1.0,
               want_bf16=True, want_f32=True, w_gate=None):
    m, d = x.shape
    tm = _pick(m, (1024, 512, 256, 128))
    kv_groups = [g for g in range(n_groups) if g not in q_groups]
    assert not q_groups or max(q_groups) < max(kv_groups)

    def f32_slot(g):
        return sum(jnp.where(g > k, 1, 0) for k in kv_groups)

    in_specs = [pl.BlockSpec((tm, d), lambda i, g: (i, 0)),
                pl.BlockSpec((1, d), lambda i, g: (0, 0)),
                pl.BlockSpec((None, d, gw), lambda i, g: (layer, 0, g))]
    args = [x, gain.reshape(1, d).astype(F32), w3]
    if w_gate is not None:
        in_specs.append(pl.BlockSpec(w_gate.shape, lambda i, g: (0, 0)))
        args.append(w_gate)
    out_specs, out_shape = [], []
    if want_bf16:
        out_specs.append(pl.BlockSpec((None, tm, gw), lambda i, g: (g, i, 0)))
        out_shape.append(jax.ShapeDtypeStruct((n_groups, m, gw), BF16))
    if want_f32:
        out_specs.append(pl.BlockSpec((None, tm, gw), lambda i, g: (f32_slot(g), i, 0)))
        out_shape.append(jax.ShapeDtypeStruct((len(kv_groups), m, gw), F32))
    if w_gate is not None:
        out_specs.append(pl.BlockSpec((tm, w_gate.shape[1]), lambda i, g: (i, 0)))
        out_shape.append(jax.ShapeDtypeStruct((m, w_gate.shape[1]), F32))
    outs = pl.pallas_call(
        functools.partial(_proj_kernel, q_groups=tuple(q_groups), q_scale=q_scale,
                          has_bf16=want_bf16, has_f32=want_f32, has_gate=w_gate is not None),
        grid=(m // tm, n_groups),
        in_specs=in_specs, out_specs=out_specs, out_shape=out_shape,
        scratch_shapes=[pltpu.VMEM((tm, d), BF16)],
        compiler_params=_params("parallel", "arbitrary"),
        name="norm_projection",
    )(*args)
    return outs[0] if len(outs) == 1 else tuple(outs)


def _ffn_kernel(x_ref, gain_ref, wg_ref, wu_ref, wd_ref, *rest, final_norm):
    if final_norm:
        fgain_ref, o_ref, xn_s, h_s = rest
    else:
        o_ref, xn_s, h_s = rest
    f = pl.program_id(1)
    last = pl.num_programs(1) - 1

    def hidden():
        xn = xn_s[...]
        g = jnp.dot(xn, wg_ref[...], preferred_element_type=F32)
        u = jnp.dot(xn, wu_ref[...], preferred_element_type=F32)
        h_s[...] = ((g * jax.nn.sigmoid(g)) * u).astype(BF16)

    def down():
        o_ref[...] += jnp.dot(h_s[...], wd_ref[...], preferred_element_type=F32)

    @pl.when(f == 0)
    def _():
        x = x_ref[...]
        ms = jnp.mean(x * x, axis=-1, keepdims=True)
        xn_s[...] = ((x * lax.rsqrt(ms + EPS)) * gain_ref[...]).astype(BF16)
        o_ref[...] = x
        hidden()

    @pl.when(jnp.logical_and(f > 0, f < last))
    def _():
        down()
        hidden()

    @pl.when(f == last)
    def _():
        down()
        if final_norm:
            y = o_ref[...]
            ms = jnp.mean(y * y, axis=-1, keepdims=True)
            o_ref[...] = (y * lax.rsqrt(ms + EPS)) * fgain_ref[...]


def _ffn(x, gain, w_gate_up3, w_down3, layer, final_gain=None):
    m, d = x.shape
    dff = w_down3.shape[1]
    tm = _pick(m, (1024, 512, 256, 128))
    tf = _pick(dff, (512, 256, 128))
    n_f = dff // tf
    vec = pl.BlockSpec((1, d), lambda i, f: (0, 0))
    up_blk = lambda f: jnp.minimum(f, n_f - 1)
    down_blk = lambda f: jnp.maximum(f - 1, 0)
    in_specs = [pl.BlockSpec((tm, d), lambda i, f: (i, 0), pipeline_mode=pl.Buffered(1)),
                vec,
                pl.BlockSpec((None, d, tf), lambda i, f: (layer, 0, up_blk(f))),
                pl.BlockSpec((None, d, tf), lambda i, f: (layer, 0, n_f + up_blk(f))),
                pl.BlockSpec((None, tf, d), lambda i, f: (layer, down_blk(f), 0))]
    args = [x, gain.reshape(1, d).astype(F32), w_gate_up3, w_gate_up3, w_down3]
    if final_gain is not None:
        in_specs.append(vec)
        args.append(final_gain.reshape(1, d).astype(F32))
    return pl.pallas_call(
        functools.partial(_ffn_kernel, final_norm=final_gain is not None),
        grid=(m // tm, n_f + 1),
        in_specs=in_specs,
        out_specs=pl.BlockSpec((tm, d), lambda i, f: (i, 0)),
        out_shape=jax.ShapeDtypeStruct((m, d), F32),
        scratch_shapes=[pltpu.VMEM((tm, d), BF16), pltpu.VMEM((tm, tf), BF16)],
        compiler_params=_params("parallel", "arbitrary"),
        name="swiglu",
    )(*args)


def _log_sigmoid(x):
    return jnp.minimum(x, 0.0) - jnp.log1p(jnp.exp(-jnp.abs(x)))


def _logf_kernel(fz_ref, b_ref, lf_ref, c2_ref):
    lf = _log_sigmoid(fz_ref[...] + b_ref[...])
    lf_ref[...] = lf
    n = lf.shape[1]
    lane = lax.broadcasted_iota(jnp.int32, lf.shape, 1)
    c = lf
    shift = 1
    while shift < n:
        c = c + jnp.where(lane >= shift, pltpu.roll(c, shift, axis=1), 0.0)
        shift *= 2
    c2_ref[...] = c * LOG2E


def _logf_cumsum(fz_t, b_f):
    h, n = fz_t.shape
    return pl.pallas_call(
        _logf_kernel,
        out_shape=[jax.ShapeDtypeStruct((h, n), F32)] * 2,
        name="logf_cumsum",
    )(fz_t, b_f.reshape(h, 1).astype(F32))


def _online_update(u, v, m_s, l_s, acc_s, row_shift):
    reps = u.shape[1] // LANES
    m_prev = m_s[...]
    m_new = jnp.maximum(m_prev, jnp.max(u, axis=1, keepdims=True) + row_shift)
    alpha = jnp.exp2(m_prev - m_new)
    p = jnp.exp2(u - _tile_lanes(m_new - row_shift, reps))
    l_s[...] = alpha * l_s[...] + jnp.sum(p, axis=1, keepdims=True)
    acc_s[...] = (_tile_lanes(alpha, acc_s.shape[1] // LANES) * acc_s[...]
                  + jnp.dot(p.astype(BF16), v, preferred_element_type=F32))
    m_s[...] = m_new


def _causal_mask(u):
    row = lax.broadcasted_iota(jnp.int32, u.shape, 0)
    col = lax.broadcasted_iota(jnp.int32, u.shape, 1)
    return jnp.where(row >= col, u, NEG_INF)


def _causal_chunks(i, scores, update):
    def pair(jj, c):
        j0 = 2 * jj
        u0, u1 = scores(j0, False), scores(j0 + 1, False)
        update(u0, j0)
        update(u1, j0 + 1)
        return c

    lax.fori_loop(0, lax.div(i, 2), pair, 0)
    odd = lax.rem(i, 2) == 1

    @pl.when(odd)
    def _():
        u0, u1 = scores(i - 1, False), scores(i, True)
        update(u0, i - 1)
        update(u1, i)

    @pl.when(jnp.logical_not(odd))
    def _():
        update(scores(i, True), i)


def _fox_kernel(q_ref, k_ref, v_ref, ccol_ref, crow_ref, o_ref, m_s, l_s, acc_s, *, t):
    i = pl.program_id(1)
    q = q_ref[...]
    cq = jnp.broadcast_to(ccol_ref[...], (t, LANES))
    m_s[...] = jnp.full_like(m_s, NEG_INF)
    l_s[...] = jnp.zeros_like(l_s)
    acc_s[...] = jnp.zeros_like(acc_s)
    dn = (((1,), (1,)), ((), ()))

    def scores(j, masked):
        k = k_ref[pl.ds(pl.multiple_of(j * t, t), t), :]
        u = lax.dot_general(q, k, dn, preferred_element_type=F32) - crow_ref[j]
        return _causal_mask(u) if masked else u

    def update(u, j):
        v = v_ref[pl.ds(pl.multiple_of(j * t, t), t), :]
        _online_update(u, v, m_s, l_s, acc_s, cq)

    _causal_chunks(i, scores, update)
    o_ref[...] = (acc_s[...] / l_s[...]).astype(o_ref.dtype)


def _fox_attention(st, gq, gk, gv, c_t):
    s_len = st.shape[1]
    t = _pick(s_len, (512, 256, 128))
    nc = s_len // t
    c_col = c_t.reshape(H_B, s_len, 1)
    c_row = c_t.reshape(H_B, nc, 1, t)
    return pl.pallas_call(
        functools.partial(_fox_kernel, t=t),
        grid=(H_B, nc),
        in_specs=[pl.BlockSpec((None, t, HD_B), lambda h, i: (gq, i, h)),
                  pl.BlockSpec((None, s_len, HD_B), lambda h, i: (gk, 0, h)),
                  pl.BlockSpec((None, s_len, HD_B), lambda h, i: (gv, 0, h)),
                  pl.BlockSpec((None, t, 1), lambda h, i: (h, i, 0)),
                  pl.BlockSpec((None, nc, 1, t), lambda h, i: (h, 0, 0, 0))],
        out_specs=pl.BlockSpec((t, HD_B), lambda h, i: (i, h)),
        out_shape=jax.ShapeDtypeStruct((s_len, H_B * HD_B), BF16),
        scratch_shapes=[pltpu.VMEM((t, LANES), F32), pltpu.VMEM((t, LANES), F32),
                        pltpu.VMEM((t, HD_B), F32)],
        compiler_params=_params("parallel", "parallel"),
        name="forget_attention",
    )(st, st, st, c_col, c_row)


def _diff_lambda_vec(lq1_ref, lk1_ref, lq2_ref, lk2_ref, lam_init):
    a = jnp.sum(lq1_ref[...] * lk1_ref[...], axis=1, keepdims=True)
    b = jnp.sum(lq2_ref[...] * lk2_ref[...], axis=1, keepdims=True)
    return jnp.exp(a) - jnp.exp(b) + lam_init


def _diff_kernel(q_ref, k_ref, v_ref, slope_ref, lq1_ref, lk1_ref, lq2_ref, lk2_ref, g_ref,
                 o_ref, m1_s, l1_s, a1_s, m2_s, l2_s, a2_s, *, t, lam_init):
    i = pl.program_id(1)
    q1 = q_ref[:, :HD_A]
    q2 = q_ref[:, HD_A:]
    slope = slope_ref[...]
    row_pos = (i * t + lax.broadcasted_iota(jnp.int32, (t, LANES), 0)).astype(F32)
    row_shift = -(slope * row_pos)
    for m_s, l_s, a_s in ((m1_s, l1_s, a1_s), (m2_s, l2_s, a2_s)):
        m_s[...] = jnp.full_like(m_s, NEG_INF)
        l_s[...] = jnp.zeros_like(l_s)
        a_s[...] = jnp.zeros_like(a_s)
    reps = t // LANES
    dn = (((1,), (1,)), ((), ()))

    def scores(j, masked):
        k = k_ref[pl.ds(pl.multiple_of(j * t, t), t), :]
        col_pos = (j * t + lax.broadcasted_iota(jnp.int32, (1, t), 1)).astype(F32)
        col_term = _tile_lanes(slope, reps) * col_pos
        us = []
        for qm, km in ((q1, k[:, :HD_A]), (q2, k[:, HD_A:])):
            u = lax.dot_general(qm, km, dn, preferred_element_type=F32) + col_term
            us.append(_causal_mask(u) if masked else u)
        return us

    def update(us, j):
        v = v_ref[pl.ds(pl.multiple_of(j * t, t), t), :]
        _online_update(us[0], v, m1_s, l1_s, a1_s, row_shift)
        _online_update(us[1], v, m2_s, l2_s, a2_s, row_shift)

    _causal_chunks(i, scores, update)
    lam = _diff_lambda_vec(lq1_ref, lk1_ref, lq2_ref, lk2_ref, lam_init)
    o = (a1_s[...] / _tile_lanes(l1_s[...], 2)
         - lam * (a2_s[...] / _tile_lanes(l2_s[...], 2)))
    y = o * lax.rsqrt(jnp.mean(o * o, axis=1, keepdims=True) + EPS) * g_ref[...] * (1.0 - lam_init)
    o_ref[...] = y.astype(o_ref.dtype)


def _alibi_slopes(n):
    return jnp.asarray(2.0 ** (-8.0 * np.arange(1, n + 1) / n), dtype=F32)


def _diff_attention(st, gq, gk, gv, lam_params, g_sub, lam_init):
    s_len = st.shape[1]
    t = _pick(s_len, (512, 256, 128))
    nc = s_len // t
    w = 2 * HD_A
    slopes = jnp.broadcast_to((_alibi_slopes(H_A) * LOG2E)[:, None, None], (H_A, 1, LANES))
    vec = pl.BlockSpec((1, HD_A), lambda h, i: (0, 0))
    return pl.pallas_call(
        functools.partial(_diff_kernel, t=t, lam_init=lam_init),
        grid=(H_A, nc),
        in_specs=[pl.BlockSpec((None, t, w), lambda h, i: (gq, i, h)),
                  pl.BlockSpec((None, s_len, w), lambda h, i: (gk, 0, h)),
                  pl.BlockSpec((None, s_len, w), lambda h, i: (gv, 0, h)),
                  pl.BlockSpec((None, 1, LANES), lambda h, i: (h, 0, 0)),
                  vec, vec, vec, vec,
                  pl.BlockSpec((1, w), lambda h, i: (0, 0))],
        out_specs=pl.BlockSpec((t, w), lambda h, i: (i, h)),
        out_shape=jax.ShapeDtypeStruct((s_len, H_A * w), BF16),
        scratch_shapes=[pltpu.VMEM((t, LANES), F32), pltpu.VMEM((t, LANES), F32),
                        pltpu.VMEM((t, w), F32),
                        pltpu.VMEM((t, LANES), F32), pltpu.VMEM((t, LANES), F32),
                        pltpu.VMEM((t, w), F32)],
        compiler_params=_params("parallel", "parallel"),
        name="diff_attention",
    )(st, st, st, slopes, *lam_params, g_sub.reshape(1, w).astype(F32))


PAGE_ROWS = PAGE_SIZE * 8


def _even_sample_tables(n_t, n_new_cols):
    slopes = 2.0 ** (-8.0 * np.arange(1, H_A + 1) / H_A)
    ninf = -np.inf
    col = np.arange(PAGE_ROWS)
    p_col, j_col = col >> 3, col & 7
    ra = np.arange(2 * n_t * H_A)
    m_r, t_r, h_r = ra // (n_t * H_A), (ra // H_A) % n_t, ra % H_A
    j_r = 2 * h_r + m_r
    sl_r = slopes[h_r]
    ta = np.where(j_col[None, :] == j_r[:, None], sl_r[:, None] * p_col[None, :], ninf)
    rb = np.arange(n_t * H_B)
    tb_t, tb_h = rb // H_B, rb % H_B
    tb = np.where(j_col[None, :] == tb_h[:, None], 0.0, ninf)
    ncol = np.arange(n_new_cols)
    u_col, jn_col = ncol >> 3, ncol & 7
    ok_a = (jn_col[None, :] == j_r[:, None]) & (u_col[None, :] <= t_r[:, None])
    tan = np.where(ok_a, -sl_r[:, None] * (t_r[:, None] - u_col[None, :]), ninf)
    ok_b = (jn_col[None, :] == tb_h[:, None]) & (u_col[None, :] <= tb_t[:, None])
    tbn = np.where(ok_b, 0.0, ninf)
    rowa = np.stack([sl_r, t_r.astype(np.float64)], axis=1)
    lane = np.arange(MXU_TILE) & 7
    me = np.stack([(lane[None, :] == (4 * e + h_r)[:, None]).astype(np.float64) for e in range(2)])
    gi = np.arange(MXU_TILE) >> 3
    gt = (gi[:, None] == gi[None, :]).astype(np.float64)
    f = lambda a: jnp.asarray(a, F32)
    return f(ta), f(tb), f(tan), f(tbn), f(rowa), f(me), jnp.asarray(gt, BF16)


def _even_sample_kernel(pt_ref, qa_ref, qb_ref, lfr_ref, lfl_ref, kan_ref, van_ref, kbn_ref, vbn_ref,
                        ta_ref, tb_ref, tan_ref, tbn_ref, rowa_ref, me_ref, gt_ref,
                        lq1_ref, lk1_ref, lq2_ref, lk2_ref, gsub_ref, *rest,
                        n_pg, n_t, p_len, scale, lam_init):
    ka_refs = rest[0 * n_pg:1 * n_pg]
    va_refs = rest[1 * n_pg:2 * n_pg]
    kb_refs = rest[2 * n_pg:3 * n_pg]
    vb_refs = rest[3 * n_pg:4 * n_pg]
    lf_refs = rest[4 * n_pg:5 * n_pg]
    oa_ref, ob_ref = rest[5 * n_pg:5 * n_pg + 2]
    ma_s, la_s, acca_s, mb_s, lb_s, accb_s, carry_s = rest[5 * n_pg + 2:]
    j = pl.program_id(1)
    n_groups = pl.num_programs(1)
    n_ra = 2 * n_t * H_A
    dn = (((1,), (1,)), ((), ()))
    qa = qa_ref[...]
    qb = qb_ref[...]
    slope = rowa_ref[:, 0:1]
    t_row = rowa_ref[:, 1:2]
    gt = gt_ref[...]

    @pl.when(j == 0)
    def _():
        for m_s, l_s, a_s in ((ma_s, la_s, acca_s), (mb_s, lb_s, accb_s)):
            m_s[...] = jnp.full_like(m_s, NEG_INF)
            l_s[...] = jnp.zeros_like(l_s)
            a_s[...] = jnp.zeros_like(a_s)
        carry_s[...] = jnp.zeros_like(carry_s)

    def softmax_step(s, m_s, l_s):
        m_prev = m_s[...]
        m_new = jnp.maximum(m_prev, jnp.max(s, axis=1, keepdims=True))
        alpha = jnp.exp(m_prev - m_new)
        p = jnp.exp(s - m_new)
        l_s[...] = alpha * l_s[...] + jnp.sum(p, axis=1, keepdims=True)
        m_s[...] = m_new
        return p, alpha

    def spread(p):
        n_tiles = p.shape[1] // MXU_TILE
        pb = p.astype(BF16)
        stacked = jnp.concatenate(
            [pb[:, c * MXU_TILE:(c + 1) * MXU_TILE] for c in range(n_tiles)], axis=0)
        rep = jnp.dot(stacked, gt, preferred_element_type=F32)
        halves = []
        for e in range(2):
            me = me_ref[e]
            halves.append(jnp.concatenate(
                [rep[c * n_ra:(c + 1) * n_ra] * me for c in range(n_tiles)], axis=1))
        return jnp.concatenate(halves, axis=0).astype(BF16)

    def attend_a(s, vs):
        p, alpha = softmax_step(s, ma_s, la_s)
        p2 = spread(p)
        pv = None
        for g, v in enumerate(vs):
            rows = v.shape[0]
            y = jnp.dot(p2[:, g * rows:(g + 1) * rows], v, preferred_element_type=F32)
            pv = y if pv is None else pv + y
        acca_s[...] = jnp.concatenate([alpha, alpha], axis=0) * acca_s[...] + pv

    def attend_b(s, vs):
        p, alpha = softmax_step(s, mb_s, lb_s)
        pb = p.astype(BF16)
        pv = None
        for g, v in enumerate(vs):
            rows = v.shape[0]
            y = jnp.dot(pb[:, g * rows:(g + 1) * rows], v, preferred_element_type=F32)
            pv = y if pv is None else pv + y
        accb_s[...] = alpha * accb_s[...] + pv

    def scores(q, ks):
        return jnp.concatenate(
            [lax.dot_general(q, k, dn, preferred_element_type=F32) for k in ks], axis=1) * scale

    lfr = lfr_ref[...]
    parts = [lfr[0:H_B]]
    for t in range(1, n_t):
        parts.append(parts[-1] + lfr[t * H_B:(t + 1) * H_B])
    cn_col = jnp.concatenate(parts, axis=0)

    group = n_groups - 1 - j
    ta = ta_ref[...]
    bias_a = jnp.concatenate(
        [ta + slope * ((group * n_pg + g).astype(F32) * float(PAGE_SIZE) - (p_len + t_row))
         for g in range(n_pg)], axis=1)
    attend_a(scores(qa, [r[...].astype(BF16) for r in ka_refs]) + bias_a,
             [r[...].astype(BF16) for r in va_refs])

    lane = lax.broadcasted_iota(jnp.int32, (H_B, PAGE_ROWS), 1)
    tail = carry_s[...]
    sufs = [None] * n_pg
    for g in range(n_pg - 1, -1, -1):
        lf = lf_refs[g][...]
        inc = lf
        shift = 8
        while shift < PAGE_ROWS:
            inc = inc + jnp.where(lane < PAGE_ROWS - shift,
                                  pltpu.roll(inc, PAGE_ROWS - shift, axis=1), 0.0)
            shift *= 2
        sufs[g] = (inc - lf) + tail
        tail = tail + inc[:, 0:1]
    carry_s[...] = tail
    suf = jnp.concatenate(sufs, axis=1)
    bias_b = (jnp.concatenate([suf] * n_t, axis=0) + cn_col
              + jnp.concatenate([tb_ref[...]] * n_pg, axis=1))
    attend_b(scores(qb, [r[...].astype(BF16) for r in kb_refs]) + bias_b,
             [r[...].astype(BF16) for r in vb_refs])

    @pl.when(j == n_groups - 1)
    def _():
        attend_a(scores(qa, [kan_ref[...]]) + tan_ref[...], [van_ref[...]])
        lfl = lfl_ref[...]
        ln = lax.broadcasted_iota(jnp.int32, lfl.shape, 1)
        cn_lane = lfl
        shift = H_B
        while shift < n_t * H_B:
            cn_lane = cn_lane + jnp.where(ln >= shift, pltpu.roll(cn_lane, shift, axis=1), 0.0)
            shift *= 2
        n_new = tbn_ref.shape[1]
        cn_keys = _tile_lanes(cn_lane, n_new // LANES)
        attend_b(scores(qb, [kbn_ref[...]]) + (tbn_ref[...] + (cn_col - cn_keys)), [vbn_ref[...]])

        lam = _diff_lambda_vec(lq1_ref, lk1_ref, lq2_ref, lk2_ref, lam_init)
        la = la_s[...]
        fa = acca_s[...] / jnp.concatenate([la, la], axis=0)
        half = n_ra // 2
        o = [fa[e * n_ra:e * n_ra + half] - lam * fa[e * n_ra + half:(e + 1) * n_ra]
             for e in range(2)]
        ms = (jnp.sum(o[0] * o[0], axis=1, keepdims=True)
              + jnp.sum(o[1] * o[1], axis=1, keepdims=True)) / (2.0 * HD_A)
        inv = lax.rsqrt(ms + EPS)
        for e in range(2):
            oa_ref[e] = o[e] * inv * gsub_ref[:, e * HD_A:(e + 1) * HD_A] * (1.0 - lam_init)
        ob_ref[...] = accb_s[...] / lb_s[...]


def _even_sample_attention(qa2, qb2, lf_rows, lf_lanes, new_kv, pools, page_table, pool_off,
                           lam_params, g_sub, lam_init):
    pool_ak, pool_av, pool_bk, pool_bv, pool_lf = pools
    bsz, n_ra, hd = qa2.shape
    n_rb = qb2.shape[1]
    n_t = n_rb // H_B
    n_new = new_kv[0].shape[1]
    n_pages = page_table.shape[1]
    n_pg = _pick(n_pages, (8, 4, 2, 1))
    n_groups = n_pages // n_pg
    tables = _even_sample_tables(n_t, n_new)

    def page_map(g):
        def index(b, j, pt):
            return (pool_off + pt[b * n_pages + (n_groups - 1 - j) * n_pg + g], 0, 0)
        return index

    per_b = lambda b, j, pt: (b, 0, 0)
    const2 = lambda b, j, pt: (0, 0)
    const3 = lambda b, j, pt: (0, 0, 0)
    vec = pl.BlockSpec((1, HD_A), const2)
    in_specs = [pl.BlockSpec((None, n_ra, hd), per_b),
                pl.BlockSpec((None, n_rb, hd), per_b),
                pl.BlockSpec((None, n_rb, 1), per_b),
                pl.BlockSpec((None, 1, LANES), per_b)]
    in_specs += [pl.BlockSpec((None, n_new, hd), per_b)] * 4
    in_specs += [pl.BlockSpec(tables[0].shape, const2), pl.BlockSpec(tables[1].shape, const2),
                 pl.BlockSpec(tables[2].shape, const2), pl.BlockSpec(tables[3].shape, const2),
                 pl.BlockSpec(tables[4].shape, const2), pl.BlockSpec(tables[5].shape, const3),
                 pl.BlockSpec(tables[6].shape, const2)]
    in_specs += [vec, vec, vec, vec, pl.BlockSpec((1, 2 * HD_A), const2)]
    args = [qa2, qb2, lf_rows, lf_lanes, *new_kv, *tables, *lam_params,
            g_sub.reshape(1, 2 * HD_A).astype(F32)]
    for pool in (pool_ak, pool_av, pool_bk, pool_bv):
        for g in range(n_pg):
            in_specs.append(pl.BlockSpec((None, PAGE_ROWS, hd), page_map(g)))
            args.append(pool)
    for g in range(n_pg):
        in_specs.append(pl.BlockSpec((None, H_B, PAGE_ROWS), page_map(g)))
        args.append(pool_lf)
    grid_spec = pltpu.PrefetchScalarGridSpec(
        num_scalar_prefetch=1,
        grid=(bsz, n_groups),
        in_specs=in_specs,
        out_specs=[pl.BlockSpec((None, 2, n_ra // 2, hd), lambda b, j, pt: (b, 0, 0, 0)),
                   pl.BlockSpec((None, n_rb, hd), per_b)],
        scratch_shapes=[pltpu.VMEM((n_ra, 1), F32), pltpu.VMEM((n_ra, 1), F32),
                        pltpu.VMEM((2 * n_ra, hd), F32),
                        pltpu.VMEM((n_rb, 1), F32), pltpu.VMEM((n_rb, 1), F32),
                        pltpu.VMEM((n_rb, hd), F32),
                        pltpu.VMEM((H_B, 1), F32)])
    return pl.pallas_call(
        functools.partial(_even_sample_kernel, n_pg=n_pg, n_t=n_t,
                          p_len=float(n_pages * PAGE_SIZE), scale=HD_A ** -0.5,
                          lam_init=lam_init),
        grid_spec=grid_spec,
        out_shape=[jax.ShapeDtypeStruct((bsz, 2, n_ra // 2, hd), F32),
                   jax.ShapeDtypeStruct((bsz, n_rb, hd), F32)],
        compiler_params=_params("parallel", "arbitrary"),
        name="even_sample_attention",
    )(page_table.reshape(-1), *args)


def _pattern_count(dist):
    return sum(((dist >= 0) & (dist <= w) & (dist % d == 0)).astype(jnp.int32)
               for w, d in C_PATTERNS)


def _count_bias(dist, slopes, same_head=True):
    cnt = _pattern_count(dist)
    logc = jnp.where((cnt > 0) & same_head, jnp.log(jnp.maximum(cnt, 1).astype(F32)), NEG_INF)
    return logc - slopes * dist.astype(F32)


def _dilated_bias(t, n_off):
    idx = jnp.arange(t, dtype=jnp.int32)
    dist = (idx[None, :, None] - idx[None, None, :]
            + t * jnp.arange(n_off, dtype=jnp.int32)[:, None, None])
    return _count_bias(dist[None], _alibi_slopes(H_C)[:, None, None, None]) * LOG2E


def _dil_kernel(q_ref, k_ref, v_ref, bias_ref, o_ref, m_s, l_s, acc_s, *, t, n_off):
    i = pl.program_id(1)
    q = q_ref[...]
    m_s[...] = jnp.full_like(m_s, NEG_INF)
    l_s[...] = jnp.zeros_like(l_s)
    acc_s[...] = jnp.zeros_like(acc_s)
    dn = (((1,), (1,)), ((), ()))
    no_shift = jnp.zeros((t, LANES), F32)

    def scores(off):
        k = k_ref[pl.ds(pl.multiple_of((i - off) * t, t), t), :]
        return lax.dot_general(q, k, dn, preferred_element_type=F32) + bias_ref[off]

    def update(u, off):
        v = v_ref[pl.ds(pl.multiple_of((i - off) * t, t), t), :]
        _online_update(u, v, m_s, l_s, acc_s, no_shift)

    @pl.when(i >= n_off - 1)
    def _():
        us = [scores(off) for off in range(n_off)]
        for off in range(n_off):
            update(us[off], off)

    @pl.when(i < n_off - 1)
    def _():
        for off in range(n_off - 1):
            @pl.when(i >= off)
            def _(off=off):
                update(scores(off), off)

    o_ref[...] = (acc_s[...] / l_s[...]).astype(o_ref.dtype)


def _dilated_attention(st):
    s_len = st.shape[1]
    per = st.shape[2] // HD_C
    t = _pick(s_len, (512, 256, 128))
    n_off = min(C_WMAX // t + 1, s_len // t)
    bias = _dilated_bias(t, n_off)
    return pl.pallas_call(
        functools.partial(_dil_kernel, t=t, n_off=n_off),
        grid=(H_C, s_len // t),
        in_specs=[pl.BlockSpec((None, t, HD_C), lambda h, i: (h // per, i, h % per)),
                  pl.BlockSpec((None, s_len, HD_C), lambda h, i: (2 + h // per, 0, h % per)),
                  pl.BlockSpec((None, s_len, HD_C), lambda h, i: (4 + h // per, 0, h % per)),
                  pl.BlockSpec((None, n_off, t, t), lambda h, i: (h, 0, 0, 0))],
        out_specs=pl.BlockSpec((t, HD_C), lambda h, i: (i, h)),
        out_shape=jax.ShapeDtypeStruct((s_len, H_C * HD_C), BF16),
        scratch_shapes=[pltpu.VMEM((t, LANES), F32), pltpu.VMEM((t, LANES), F32),
                        pltpu.VMEM((t, HD_C), F32)],
        compiler_params=_params("parallel", "parallel"),
        name="dilated_attention",
    )(st, st, st, bias)


def _c_sample_bias(buf_len, n_t, n_new_cols):
    row = jnp.arange(n_t * H_C, dtype=jnp.int32)[:, None]
    t_row, h_row = row // H_C, row % H_C
    slopes = jnp.tile(_alibi_slopes(H_C), n_t)[:, None]

    def table(n_pos, pos0):
        col = jnp.arange(n_pos * H_C, dtype=jnp.int32)[None, :]
        dist = buf_len + t_row - (pos0 + col // H_C)
        return _count_bias(dist, slopes, h_row == col % H_C)

    past = table(buf_len, 0)
    new = table(n_t, buf_len)
    new = jnp.pad(new, ((0, 0), (0, n_new_cols - new.shape[1])), constant_values=NEG_INF)
    return past, new


def _c_sample_kernel(q_ref, kc_ref, kx_ref, kn_ref, vc_ref, vx_ref, vn_ref, bias_ref, biasn_ref,
                     ok_ref, ov_ref, o_ref, m_s, l_s, acc_s, *, rows, new, scale):
    c = pl.program_id(1)
    last = pl.num_programs(1) - 1
    dn = (((1,), (1,)), ((), ()))
    q = q_ref[...]

    @pl.when(c == 0)
    def _():
        m_s[...] = jnp.full_like(m_s, NEG_INF)
        l_s[...] = jnp.zeros_like(l_s)
        acc_s[...] = jnp.zeros_like(acc_s)

    for src, nxt, fresh, dst in ((kc_ref, kx_ref, kn_ref, ok_ref), (vc_ref, vx_ref, vn_ref, ov_ref)):
        dst[pl.ds(0, rows - new), :] = src[pl.ds(new, rows - new), :]

        @pl.when(c < last)
        def _(nxt=nxt, dst=dst):
            dst[pl.ds(rows - new, new), :] = nxt[...]

        @pl.when(c == last)
        def _(fresh=fresh, dst=dst):
            dst[pl.ds(rows - new, new), :] = fresh[...]

    def attend(s, v):
        m_prev = m_s[...]
        m_new = jnp.maximum(m_prev, jnp.max(s, axis=1, keepdims=True))
        alpha = jnp.exp(m_prev - m_new)
        p = jnp.exp(s - m_new)
        l_s[...] = alpha * l_s[...] + jnp.sum(p, axis=1, keepdims=True)
        acc_s[...] = alpha * acc_s[...] + jnp.dot(p.astype(BF16), v, preferred_element_type=F32)
        m_s[...] = m_new

    s = lax.dot_general(q, kc_ref[...].astype(BF16), dn, preferred_element_type=F32)
    attend(s * scale + bias_ref[...], vc_ref[...].astype(BF16))

    @pl.when(c == last)
    def _():
        pad = jnp.zeros((biasn_ref.shape[1] - new, kn_ref.shape[1]), F32)
        kn = jnp.concatenate([kn_ref[...], pad], axis=0).astype(BF16)
        vn = jnp.concatenate([vn_ref[...], pad], axis=0).astype(BF16)
        s_n = lax.dot_general(q, kn, dn, preferred_element_type=F32)
        attend(s_n * scale + biasn_ref[...], vn)
        o_ref[...] = acc_s[...] / l_s[...]


def _c_sample(q2, cache_k, cache_v, k_new, v_new, batch_off, n_t):
    bsz, n_r, hd = q2.shape
    total = cache_k.shape[1]
    new = k_new.shape[1]
    buf_len = total // H_C
    rows = _pick(total, (8192, 4096, 2048, 1024))
    assert rows % new == 0 and total % rows == 0
    n_chunks = total // rows
    n_new_cols = max(LANES, new)
    bias, bias_n = _c_sample_bias(buf_len, n_t, n_new_cols)
    blocks_per_chunk = rows // new
    n_small = total // new

    cur = pl.BlockSpec((None, rows, hd), lambda b, c: (batch_off + b, c, 0))
    nxt = pl.BlockSpec((None, new, hd),
                       lambda b, c: (batch_off + b, jnp.minimum((c + 1) * blocks_per_chunk,
                                                                n_small - 1), 0))
    per_b = pl.BlockSpec((None, new, hd), lambda b, c: (b, 0, 0))
    out_buf = pl.BlockSpec((None, rows, hd), lambda b, c: (b, c, 0))
    return pl.pallas_call(
        functools.partial(_c_sample_kernel, rows=rows, new=new, scale=HD_C ** -0.5),
        grid=(bsz, n_chunks),
        in_specs=[pl.BlockSpec((None, n_r, hd), lambda b, c: (b, 0, 0)),
                  cur, nxt, per_b, cur, nxt, per_b,
                  pl.BlockSpec((n_r, rows), lambda b, c: (0, c)),
                  pl.BlockSpec((n_r, n_new_cols), lambda b, c: (0, 0))],
        out_specs=[out_buf, out_buf, pl.BlockSpec((None, n_r, hd), lambda b, c: (b, 0, 0))],
        out_shape=[jax.ShapeDtypeStruct((bsz, total, hd), F32),
                   jax.ShapeDtypeStruct((bsz, total, hd), F32),
                   jax.ShapeDtypeStruct((bsz, n_r, hd), F32)],
        scratch_shapes=[pltpu.VMEM((n_r, 1), F32), pltpu.VMEM((n_r, 1), F32),
                        pltpu.VMEM((n_r, hd), F32)],
        compiler_params=_params("parallel", "arbitrary"),
        name="window_sample_attention",
    )(q2, cache_k, cache_k, k_new, cache_v, cache_v, v_new, bias, bias_n)


def _cross_kernel(q_ref, k_ref, v_ref, o_ref, *, scale):
    dn = (((1,), (1,)), ((), ()))
    for h in range(H_X):
        cols = slice(h * HD_X, (h + 1) * HD_X)
        s = lax.dot_general(q_ref[:, cols], k_ref[:, cols].astype(BF16), dn,
                            preferred_element_type=F32) * scale
        m = jnp.max(s, axis=1, keepdims=True)
        e = jnp.exp(s - m)
        l = jnp.sum(e, axis=1, keepdims=True)
        o = jnp.dot(e.astype(BF16), v_ref[:, cols].astype(BF16), preferred_element_type=F32)
        o_ref[:, cols] = (o / l).astype(o_ref.dtype)


def _cross_attention(q, mem_k, mem_v):
    bsz, n_q, w = q.shape
    n_m = mem_k.shape[1]
    tq = _pick(n_q, (512, 256, 128))
    return pl.pallas_call(
        functools.partial(_cross_kernel, scale=HD_X ** -0.5),
        grid=(bsz, n_q // tq),
        in_specs=[pl.BlockSpec((None, tq, w), lambda b, i: (b, i, 0)),
                  pl.BlockSpec((None, n_m, w), lambda b, i: (b, 0, 0)),
                  pl.BlockSpec((None, n_m, w), lambda b, i: (b, 0, 0))],
        out_specs=pl.BlockSpec((None, tq, w), lambda b, i: (b, i, 0)),
        out_shape=jax.ShapeDtypeStruct((bsz, n_q, w), BF16),
        compiler_params=_params("parallel", "parallel"),
        name="cross_attention",
    )(q, mem_k, mem_v)


def _cross_rows_kernel(q_ref, k_ref, v_ref, mask_ref, o_ref, *, scale):
    dn = (((1,), (1,)), ((), ()))
    s = lax.dot_general(q_ref[...], k_ref[...].astype(BF16), dn,
                        preferred_element_type=F32) * scale + mask_ref[...]
    m = jnp.max(s, axis=1, keepdims=True)
    e = jnp.exp(s - m)
    l = jnp.sum(e, axis=1, keepdims=True)
    o = jnp.dot(e.astype(BF16), v_ref[...].astype(BF16), preferred_element_type=F32)
    o_ref[...] = o / l


def _cross_attention_rows(q2, mem_k, mem_v, batch_off):
    bsz, n_r, hd = q2.shape
    n_c = mem_k.shape[1]
    same = (np.arange(n_r)[:, None] % H_X) == (np.arange(n_c)[None, :] % H_X)
    mask = jnp.asarray(np.where(same, 0.0, -np.inf), F32)
    kv = pl.BlockSpec((None, n_c, hd), lambda b: (batch_off + b, 0, 0))
    return pl.pallas_call(
        functools.partial(_cross_rows_kernel, scale=HD_X ** -0.5),
        grid=(bsz,),
        in_specs=[pl.BlockSpec((None, n_r, hd), lambda b: (b, 0, 0)), kv, kv,
                  pl.BlockSpec((n_r, n_c), lambda b: (0, 0))],
        out_specs=pl.BlockSpec((None, n_r, hd), lambda b: (b, 0, 0)),
        out_shape=jax.ShapeDtypeStruct((bsz, n_r, hd), F32),
        compiler_params=_params("parallel"),
        name="cross_attention_rows",
    )(q2, mem_k, mem_v, mask)


NEW_KEY_ROWS = MXU_TILE


def _pad_rows(x, rows):
    return jnp.pad(x, ((0, 0), (0, rows - x.shape[1]), (0, 0)))


def kernel(x_prompt, x_sample, mem_prompt, cache_a_k, cache_a_v, cache_b_k, cache_b_v, cache_b_logf, cache_c_k, cache_c_v, cache_mem_k, cache_mem_v, page_table, norm_mix, norm_cross, norm_mem, norm_ffn, norm_final, w_in_even, b_forget, lambda_q1, lambda_k1, lambda_q2, lambda_k2, subln_a, w_out_even, w_in_odd, w_out_odd, w_xq, w_xkv, w_xo, w_gate_up, w_down):
    n_b, s_len, d_model = x_prompt.shape
    d_b, d_t, _ = x_sample.shape
    assert n_b == 1
    depth = norm_mix.shape[0]
    n_mem = mem_prompt.shape[1]
    xp = x_prompt.reshape(s_len, d_model)
    xs = x_sample.reshape(d_b * d_t, d_model)
    mem = mem_prompt.reshape(n_mem, d_model)
    wa = H_A * 2 * HD_A
    wb = H_B * HD_B
    wc = H_C * HD_C
    wx = H_X * HD_X

    n_even, n_pool = cache_a_k.shape[:2]
    pool_ak = cache_a_k.reshape(n_even * n_pool, PAGE_ROWS, HD_A)
    pool_av = (cache_a_v.reshape(n_even, n_pool, PAGE_SIZE, H_A, 2, HD_A)
               .transpose(0, 1, 2, 4, 3, 5).reshape(n_even * n_pool, PAGE_ROWS, HD_A))
    pool_bk = cache_b_k.reshape(n_even * n_pool, PAGE_ROWS, HD_B)
    pool_bv = cache_b_v.reshape(n_even * n_pool, PAGE_ROWS, HD_B)
    pool_lf = jnp.repeat(jnp.swapaxes(cache_b_logf, 2, 3), H_B, axis=3).reshape(
        n_even * n_pool, H_B, PAGE_ROWS)
    n_odd, _, buf_len = cache_c_k.shape[:3]
    win_k = cache_c_k.reshape(n_odd * d_b, buf_len * H_C, HD_C)
    win_v = cache_c_v.reshape(n_odd * d_b, buf_len * H_C, HD_C)
    memc_k = cache_mem_k.reshape(depth * d_b, n_mem * H_X, HD_X)
    memc_v = cache_mem_v.reshape(depth * d_b, n_mem * H_X, HD_X)

    ak_p, av_p, bk_p, bv_p, bl_p, ck_p, cv_p, mk_p, mv_p = [], [], [], [], [], [], [], [], []
    ak_s, av_s, bk_s, bv_s, bl_s, ck_s, cv_s = [], [], [], [], [], [], []

    w_in_even16, w_out_even16 = w_in_even.astype(BF16), w_out_even.astype(BF16)
    w_in_odd16, w_out_odd16 = w_in_odd.astype(BF16), w_out_odd.astype(BF16)
    w_xq16, w_xkv16, w_xo16 = w_xq.astype(BF16), w_xkv.astype(BF16), w_xo.astype(BF16)
    w_gate_up16, w_down16 = w_gate_up.astype(BF16), w_down.astype(BF16)
    gw = 1024

    for l in range(depth):
        j = l // 2
        if l % 2 == 0:
            lam_init = 0.8 - 0.6 * math.exp(-0.3 * l)
            w_fz = jnp.pad(w_in_even[j][:, 6 * wa:], ((0, 0), (0, LANES - H_B))).astype(BF16)
            lam_params = [p[j].reshape(1, HD_A).astype(F32)
                          for p in (lambda_q1, lambda_k1, lambda_q2, lambda_k2)]
            assert wa == gw and wb == gw and HD_A == HD_B
            proj = functools.partial(_norm_proj, gain=norm_mix[l], w3=w_in_even16, layer=j,
                                     n_groups=6, gw=gw, q_groups=(0, 3), w_gate=w_fz)

            st16, st32, fz = proj(xp, q_scale=HD_A ** -0.5 * LOG2E)
            lf_t, c_t = _logf_cumsum(fz[:, :H_B].T, b_forget[j])
            za = _diff_attention(st16, 0, 1, 2, lam_params, subln_a[j], lam_init)
            zb = _fox_attention(st16, 3, 4, 5, c_t)
            xp = _matmul_res([(za, w_out_even16, j, 0), (zb, w_out_even16, j, 1)], xp)
            ak_p.append(st32[0].reshape(1, s_len, H_A, 2, HD_A))
            av_p.append(st32[1].reshape(1, s_len, H_A, 2 * HD_A))
            bk_p.append(st32[2].reshape(1, s_len, H_B, HD_B))
            bv_p.append(st32[3].reshape(1, s_len, H_B, HD_B))
            bl_p.append(lf_t.T.reshape(1, s_len, H_B))

            ss16, ss32, fz_s = proj(xs)
            qa_s, ka_s16, va_s16, qb_s, kb_s16, vb_s16 = (ss16[g] for g in range(6))
            ka_s, va_s, kb_s, vb_s = (ss32[g] for g in range(4))
            lf_s_t, _ = _logf_cumsum(fz_s[:, :H_B].T, b_forget[j])
            lf_s = lf_s_t.T
            qa2 = (qa_s.reshape(d_b, d_t, H_A, 2, HD_A).transpose(0, 3, 1, 2, 4)
                   .reshape(d_b, 2 * d_t * H_A, HD_A))
            qb2 = qb_s.reshape(d_b, d_t * H_B, HD_B)
            va_rows = (va_s16.reshape(d_b, d_t, H_A, 2, HD_A).transpose(0, 1, 3, 2, 4)
                       .reshape(d_b, d_t * 8, HD_A))
            new_kv = [_pad_rows(a, NEW_KEY_ROWS)
                      for a in (ka_s16.reshape(d_b, d_t * 8, HD_A), va_rows,
                                kb_s16.reshape(d_b, d_t * H_B, HD_B),
                                vb_s16.reshape(d_b, d_t * H_B, HD_B))]
            lf_rows = lf_s.reshape(d_b, d_t * H_B, 1)
            lf_lanes = _pad_rows(lf_rows, LANES).reshape(d_b, 1, LANES)
            oa_s, ob_s = _even_sample_attention(
                qa2, qb2, lf_rows, lf_lanes, new_kv,
                (pool_ak, pool_av, pool_bk, pool_bv, pool_lf), page_table, j * n_pool,
                lam_params, subln_a[j], lam_init)
            za_s = (oa_s.reshape(d_b, 2, d_t, H_A, HD_A).transpose(0, 2, 3, 1, 4)
                    .reshape(d_b * d_t, wa).astype(BF16))
            zb_s = ob_s.reshape(d_b * d_t, wb).astype(BF16)
            xs = _matmul_res([(za_s, w_out_even16, j, 0), (zb_s, w_out_even16, j, 1)], xs)
            ak_s.append(ka_s.reshape(d_b, d_t, H_A, 2, HD_A))
            av_s.append(va_s.reshape(d_b, d_t, H_A, 2 * HD_A))
            bk_s.append(kb_s.reshape(d_b, d_t, H_B, HD_B))
            bv_s.append(vb_s.reshape(d_b, d_t, H_B, HD_B))
            bl_s.append(lf_s.reshape(d_b, d_t, H_B))
        else:
            assert wc == 2 * gw
            proj = functools.partial(_norm_proj, gain=norm_mix[l], w3=w_in_odd16, layer=j,
                                     n_groups=6, gw=gw, q_groups=(0, 1))
            halves = lambda s32, a: jnp.concatenate([s32[a], s32[a + 1]], axis=1)

            st16, st32 = proj(xp, q_scale=HD_C ** -0.5 * LOG2E)
            o = _dilated_attention(st16)
            xp = _matmul_res([(o, w_out_odd16, j, 0)], xp)
            keep = min(C_WMAX, s_len)
            ck_p.append(halves(st32[:, s_len - keep:], 0).reshape(1, keep, H_C, HD_C))
            cv_p.append(halves(st32[:, s_len - keep:], 2).reshape(1, keep, H_C, HD_C))

            ss16, ss32 = proj(xs)
            q_s = jnp.concatenate([ss16[0], ss16[1]], axis=1)
            k_s, v_s = halves(ss32, 0), halves(ss32, 2)
            new_k, new_v, o_s = _c_sample(
                q_s.reshape(d_b, d_t * H_C, HD_C), win_k, win_v,
                k_s.reshape(d_b, d_t * H_C, HD_C), v_s.reshape(d_b, d_t * H_C, HD_C),
                j * d_b, d_t)
            xs = _matmul_res([(o_s.reshape(d_b * d_t, wc).astype(BF16), w_out_odd16, j, 0)], xs)
            ck_s.append(new_k.reshape(d_b, buf_len, H_C, HD_C))
            cv_s.append(new_v.reshape(d_b, buf_len, H_C, HD_C))

        mem_kv = _norm_proj(mem, norm_mem[l], w_xkv16, l, 2, wx, want_bf16=False)
        mk_p.append(mem_kv[0].reshape(1, n_mem, H_X, HD_X))
        mv_p.append(mem_kv[1].reshape(1, n_mem, H_X, HD_X))
        qx = _norm_proj(xp, norm_cross[l], w_xq16, l, 1, wx, want_f32=False)
        ox = _cross_attention(qx, mem_kv[0:1], mem_kv[1:2])
        xp = _matmul_res([(ox.reshape(s_len, wx), w_xo16, l, 0)], xp)
        qx_s = _norm_proj(xs, norm_cross[l], w_xq16, l, 1, wx, want_f32=False)
        ox_s = _cross_attention_rows(qx_s.reshape(d_b, d_t * H_X, HD_X), memc_k, memc_v, l * d_b)
        xs = _matmul_res([(ox_s.reshape(d_b * d_t, wx).astype(BF16), w_xo16, l, 0)], xs)

        final_gain = norm_final if l == depth - 1 else None
        xp = _ffn(xp, norm_ffn[l], w_gate_up16, w_down16, l, final_gain)
        xs = _ffn(xs, norm_ffn[l], w_gate_up16, w_down16, l, final_gain)

    y_prompt = xp.reshape(1, s_len, d_model)
    y_sample = xs.reshape(d_b, d_t, d_model)
    st = jnp.stack
    return (y_prompt, y_sample,
            st(ak_p), st(av_p), st(bk_p), st(bv_p), st(bl_p), st(ck_p), st(cv_p), st(mk_p), st(mv_p),
            st(ak_s), st(av_s), st(bk_s), st(bv_s), st(bl_s), st(ck_s), st(cv_s))
```

```python
import functools
import math

import numpy as np
import jax
import jax.numpy as jnp
from jax import lax
from jax.experimental import pallas as pl
from jax.experimental.pallas import tpu as pltpu

F32 = jnp.float32
BF16 = jnp.bfloat16

EPS = 1e-6
PAGE_SIZE = 128
H_A, HD_A = 4, 128
H_B, HD_B = 8, 128
H_C, HD_C = 16, 128
H_X, HD_X = 4, 128
C_PATTERNS = ((128, 1), (512, 4), (2048, 16))
C_WMAX = 2048
LANES = 128
MXU_TILE = 256
VMEM_LIMIT = 52 * 1024 * 1024
NEG_INF = float("-inf")
LOG2E = 1.4426950408889634


def _params(*sem):
    return pltpu.CompilerParams(dimension_semantics=sem, vmem_limit_bytes=VMEM_LIMIT)


def _tile_lanes(x, reps):
    return x if reps == 1 else jnp.concatenate([x] * reps, axis=1)


def _pick(n, pref):
    for t in pref:
        if n % t == 0:
            return t
    return n


def _mm_kernel(*refs, n_pairs):
    acc = refs[2 * n_pairs][...]
    for p in range(n_pairs):
        acc = acc + jnp.dot(refs[2 * p][...], refs[2 * p + 1][...], preferred_element_type=F32)
    refs[2 * n_pairs + 1][...] = acc


def _matmul_res(pairs, res):
    m = res.shape[0]
    n = res.shape[1]
    tm = _pick(m, (1024, 512, 256, 128))
    tn = _pick(n, (1024, 512, 256, 128))
    in_specs, args = [], []
    for a, w3, layer, row_blk in pairs:
        k = a.shape[1]
        in_specs += [pl.BlockSpec((tm, k), lambda i, j: (i, 0)),
                     pl.BlockSpec((None, k, tn),
                                  lambda i, j, layer=layer, row_blk=row_blk: (layer, row_blk, j))]
        args += [a, w3]
    in_specs.append(pl.BlockSpec((tm, tn), lambda i, j: (i, j)))
    return pl.pallas_call(
        functools.partial(_mm_kernel, n_pairs=len(pairs)),
        grid=(m // tm, n // tn),
        in_specs=in_specs,
        out_specs=pl.BlockSpec((tm, tn), lambda i, j: (i, j)),
        out_shape=jax.ShapeDtypeStruct((m, n), F32),
        compiler_params=_params("parallel", "parallel"),
        name="matmul_residual",
    )(*args, res)


def _proj_kernel(*refs, q_groups, q_scale, has_bf16, has_f32, has_gate, row_perms):
    x_ref, gain_ref, w_ref = refs[:3]
    pos = 3
    wz_ref = ob_ref = of_ref = oz_ref = None
    if has_gate:
        wz_ref = refs[pos]
        pos += 1
    if has_bf16:
        ob_ref = refs[pos]
        pos += 1
    if has_f32:
        of_ref = refs[pos]
        pos += 1
    if has_gate:
        oz_ref = refs[pos]
        pos += 1
    xn_s = refs[pos]
    g = pl.program_id(1)

    @pl.when(g == 0)
    def _():
        x = x_ref[...]
        ms = jnp.mean(x * x, axis=-1, keepdims=True)
        xn_s[...] = ((x * lax.rsqrt(ms + EPS)) * gain_ref[...]).astype(BF16)
        if has_gate:
            oz_ref[...] = jnp.dot(xn_s[...], wz_ref[...], preferred_element_type=F32)

    y = jnp.dot(xn_s[...], w_ref[...], preferred_element_type=F32)
    is_q = None
    for qg in q_groups:
        is_q = (g == qg) if is_q is None else jnp.logical_or(is_q, g == qg)
    if has_bf16:
        ob_ref[...] = (y if is_q is None else y * jnp.where(is_q, q_scale, 1.0)).astype(BF16)
    if has_f32:
        if row_perms is not None:
            n_blk = y.shape[1] // LANES
            for perm in sorted(set(row_perms.values())):
                hit = None
                for grp, p in row_perms.items():
                    if p == perm:
                        hit = (g == grp) if hit is None else jnp.logical_or(hit, g == grp)

                @pl.when(hit)
                def _(perm=perm):
                    for c in range(n_blk):
                        of_ref[pl.ds(perm[c], y.shape[0], stride=n_blk), :] = (
                            y[:, c * LANES:(c + 1) * LANES])
        elif is_q is None:
            of_ref[...] = y
        else:
            @pl.when(jnp.logical_not(is_q))
            def _():
                of_ref[...] = y


def _norm_proj(x, gain, w3, layer, n_groups, gw, q_groups=(), q_scale=1.0,
               want_bf16=True, want_f32=True, w_gate=None, row_perms=None):
    m, d = x.shape
    tm = _pick(m, (1024, 512, 256, 128))
    kv_groups = [g for g in range(n_groups) if g not in q_groups]
    assert not q_groups or max(q_groups) < max(kv_groups)
    assert row_perms is None or sorted(row_perms) == kv_groups

    def f32_slot(g):
        return sum(jnp.where(g > k, 1, 0) for k in kv_groups)

    in_specs = [pl.BlockSpec((tm, d), lambda i, g: (i, 0)),
                pl.BlockSpec((1, d), lambda i, g: (0, 0)),
                pl.BlockSpec((None, d, gw), lambda i, g: (layer, 0, g))]
    args = [x, gain.reshape(1, d).astype(F32), w3]
    if w_gate is not None:
        in_specs.append(pl.BlockSpec(w_gate.shape, lambda i, g: (0, 0)))
        args.append(w_gate)
    out_specs, out_shape = [], []
    if want_bf16:
        out_specs.append(pl.BlockSpec((None, tm, gw), lambda i, g: (g, i, 0)))
        out_shape.append(jax.ShapeDtypeStruct((n_groups, m, gw), BF16))
    if want_f32:
        blk = (tm, gw) if row_perms is None else (tm * (gw // LANES), LANES)
        out_specs.append(pl.BlockSpec((None,) + blk, lambda i, g: (f32_slot(g), i, 0)))
        out_shape.append(jax.ShapeDtypeStruct((len(kv_groups), m // tm * blk[0], blk[1]), F32))
    if w_gate is not None:
        out_specs.append(pl.BlockSpec((tm, w_gate.shape[1]), lambda i, g: (i, 0)))
        out_shape.append(jax.ShapeDtypeStruct((m, w_gate.shape[1]), F32))
    outs = pl.pallas_call(
        functools.partial(_proj_kernel, q_groups=tuple(q_groups), q_scale=q_scale,
                          has_bf16=want_bf16, has_f32=want_f32, has_gate=w_gate is not None,
                          row_perms=row_perms),
        grid=(m // tm, n_groups),
        in_specs=in_specs, out_specs=out_specs, out_shape=out_shape,
        scratch_shapes=[pltpu.VMEM((tm, d), BF16)],
        compiler_params=_params("parallel", "arbitrary"),
        name="norm_projection",
    )(*args)
    return outs[0] if len(outs) == 1 else tuple(outs)


def _ffn_kernel(x_ref, gain_ref, wg_ref, wu_ref, wd_ref, *rest, final_norm):
    if final_norm:
        fgain_ref, o_ref, xn_s, h_s = rest
    else:
        o_ref, xn_s, h_s = rest
    f = pl.program_id(1)
    last = pl.num_programs(1) - 1

    def hidden():
        xn = xn_s[...]
        g = jnp.dot(xn, wg_ref[...], preferred_element_type=F32)
        u = jnp.dot(xn, wu_ref[...], preferred_element_type=F32)
        h_s[...] = ((g * jax.nn.sigmoid(g)) * u).astype(BF16)

    def down():
        o_ref[...] += jnp.dot(h_s[...], wd_ref[...], preferred_element_type=F32)

    @pl.when(f == 0)
    def _():
        x = x_ref[...]
        ms = jnp.mean(x * x, axis=-1, keepdims=True)
        xn_s[...] = ((x * lax.rsqrt(ms + EPS)) * gain_ref[...]).astype(BF16)
        o_ref[...] = x
        hidden()

    @pl.when(jnp.logical_and(f > 0, f < last))
    def _():
        down()
        hidden()

    @pl.when(f == last)
    def _():
        down()
        if final_norm:
            y = o_ref[...]
            ms = jnp.mean(y * y, axis=-1, keepdims=True)
            o_ref[...] = (y * lax.rsqrt(ms + EPS)) * fgain_ref[...]


def _ffn(x, gain, w_gate_up3, w_down3, layer, final_gain=None):
    m, d = x.shape
    dff = w_down3.shape[1]
    tm = _pick(m, (1024, 512, 256, 128))
    tf = _pick(dff, (512, 256, 128))
    n_f = dff // tf
    vec = pl.BlockSpec((1, d), lambda i, f: (0, 0))
    up_blk = lambda f: jnp.minimum(f, n_f - 1)
    down_blk = lambda f: jnp.maximum(f - 1, 0)
    in_specs = [pl.BlockSpec((tm, d), lambda i, f: (i, 0), pipeline_mode=pl.Buffered(1)),
                vec,
                pl.BlockSpec((None, d, tf), lambda i, f: (layer, 0, up_blk(f))),
                pl.BlockSpec((None, d, tf), lambda i, f: (layer, 0, n_f + up_blk(f))),
                pl.BlockSpec((None, tf, d), lambda i, f: (layer, down_blk(f), 0))]
    args = [x, gain.reshape(1, d).astype(F32), w_gate_up3, w_gate_up3, w_down3]
    if final_gain is not None:
        in_specs.append(vec)
        args.append(final_gain.reshape(1, d).astype(F32))
    return pl.pallas_call(
        functools.partial(_ffn_kernel, final_norm=final_gain is not None),
        grid=(m // tm, n_f + 1),
        in_specs=in_specs,
        out_specs=pl.BlockSpec((tm, d), lambda i, f: (i, 0)),
        out_shape=jax.ShapeDtypeStruct((m, d), F32),
        scratch_shapes=[pltpu.VMEM((tm, d), BF16), pltpu.VMEM((tm, tf), BF16)],
        compiler_params=_params("parallel", "arbitrary"),
        name="swiglu",
    )(*args)


def _log_sigmoid(x):
    return jnp.minimum(x, 0.0) - jnp.log1p(jnp.exp(-jnp.abs(x)))


def _logf_kernel(fz_ref, b_ref, lf_ref, c2_ref):
    lf = _log_sigmoid(fz_ref[...] + b_ref[...])
    lf_ref[...] = lf
    n = lf.shape[1]
    lane = lax.broadcasted_iota(jnp.int32, lf.shape, 1)
    c = lf
    shift = 1
    while shift < n:
        c = c + jnp.where(lane >= shift, pltpu.roll(c, shift, axis=1), 0.0)
        shift *= 2
    c2_ref[...] = c * LOG2E


def _logf_cumsum(fz_t, b_f):
    h, n = fz_t.shape
    return pl.pallas_call(
        _logf_kernel,
        out_shape=[jax.ShapeDtypeStruct((h, n), F32)] * 2,
        name="logf_cumsum",
    )(fz_t, b_f.reshape(h, 1).astype(F32))


def _online_update(u, v, m_s, l_s, acc_s, row_shift):
    reps = u.shape[1] // LANES
    m_prev = m_s[...]
    m_new = jnp.maximum(m_prev, jnp.max(u, axis=1, keepdims=True) + row_shift)
    alpha = jnp.exp2(m_prev - m_new)
    p = jnp.exp2(u - _tile_lanes(m_new - row_shift, reps))
    l_s[...] = alpha * l_s[...] + jnp.sum(p, axis=1, keepdims=True)
    acc_s[...] = (_tile_lanes(alpha, acc_s.shape[1] // LANES) * acc_s[...]
                  + jnp.dot(p.astype(BF16), v, preferred_element_type=F32))
    m_s[...] = m_new


def _causal_mask(u):
    row = lax.broadcasted_iota(jnp.int32, u.shape, 0)
    col = lax.broadcasted_iota(jnp.int32, u.shape, 1)
    return jnp.where(row >= col, u, NEG_INF)


def _causal_chunks(i, scores, update):
    def pair(jj, c):
        j0 = 2 * jj
        u0, u1 = scores(j0, False), scores(j0 + 1, False)
        update(u0, j0)
        update(u1, j0 + 1)
        return c

    lax.fori_loop(0, lax.div(i, 2), pair, 0)
    odd = lax.rem(i, 2) == 1

    @pl.when(odd)
    def _():
        u0, u1 = scores(i - 1, False), scores(i, True)
        update(u0, i - 1)
        update(u1, i)

    @pl.when(jnp.logical_not(odd))
    def _():
        update(scores(i, True), i)


def _fox_kernel(q_ref, k_ref, v_ref, ccol_ref, crow_ref, o_ref, m_s, l_s, acc_s, *, t):
    i = pl.program_id(1)
    q = q_ref[...]
    cq = jnp.broadcast_to(ccol_ref[...], (t, LANES))
    m_s[...] = jnp.full_like(m_s, NEG_INF)
    l_s[...] = jnp.zeros_like(l_s)
    acc_s[...] = jnp.zeros_like(acc_s)
    dn = (((1,), (1,)), ((), ()))

    def scores(j, masked):
        k = k_ref[pl.ds(pl.multiple_of(j * t, t), t), :]
        u = lax.dot_general(q, k, dn, preferred_element_type=F32) - crow_ref[j]
        return _causal_mask(u) if masked else u

    def update(u, j):
        v = v_ref[pl.ds(pl.multiple_of(j * t, t), t), :]
        _online_update(u, v, m_s, l_s, acc_s, cq)

    _causal_chunks(i, scores, update)
    o_ref[...] = (acc_s[...] / l_s[...]).astype(o_ref.dtype)


def _fox_attention(st, gq, gk, gv, c_t):
    s_len = st.shape[1]
    t = _pick(s_len, (512, 256, 128))
    nc = s_len // t
    c_col = c_t.reshape(H_B, s_len, 1)
    c_row = c_t.reshape(H_B, nc, 1, t)
    return pl.pallas_call(
        functools.partial(_fox_kernel, t=t),
        grid=(H_B, nc),
        in_specs=[pl.BlockSpec((None, t, HD_B), lambda h, i: (gq, i, h)),
                  pl.BlockSpec((None, s_len, HD_B), lambda h, i: (gk, 0, h)),
                  pl.BlockSpec((None, s_len, HD_B), lambda h, i: (gv, 0, h)),
                  pl.BlockSpec((None, t, 1), lambda h, i: (h, i, 0)),
                  pl.BlockSpec((None, nc, 1, t), lambda h, i: (h, 0, 0, 0))],
        out_specs=pl.BlockSpec((t, HD_B), lambda h, i: (i, h)),
        out_shape=jax.ShapeDtypeStruct((s_len, H_B * HD_B), BF16),
        scratch_shapes=[pltpu.VMEM((t, LANES), F32), pltpu.VMEM((t, LANES), F32),
                        pltpu.VMEM((t, HD_B), F32)],
        compiler_params=_params("parallel", "parallel"),
        name="forget_attention",
    )(st, st, st, c_col, c_row)


def _diff_lambda_vec(lq1_ref, lk1_ref, lq2_ref, lk2_ref, lam_init):
    a = jnp.sum(lq1_ref[...] * lk1_ref[...], axis=1, keepdims=True)
    b = jnp.sum(lq2_ref[...] * lk2_ref[...], axis=1, keepdims=True)
    return jnp.exp(a) - jnp.exp(b) + lam_init


def _diff_kernel(q_ref, k_ref, v_ref, slope_ref, lq1_ref, lk1_ref, lq2_ref, lk2_ref, g_ref,
                 o_ref, m1_s, l1_s, a1_s, m2_s, l2_s, a2_s, *, t, lam_init):
    i = pl.program_id(1)
    q1 = q_ref[:, :HD_A]
    q2 = q_ref[:, HD_A:]
    slope = slope_ref[...]
    row_pos = (i * t + lax.broadcasted_iota(jnp.int32, (t, LANES), 0)).astype(F32)
    row_shift = -(slope * row_pos)
    for m_s, l_s, a_s in ((m1_s, l1_s, a1_s), (m2_s, l2_s, a2_s)):
        m_s[...] = jnp.full_like(m_s, NEG_INF)
        l_s[...] = jnp.zeros_like(l_s)
        a_s[...] = jnp.zeros_like(a_s)
    reps = t // LANES
    dn = (((1,), (1,)), ((), ()))

    def scores(j, masked):
        k = k_ref[pl.ds(pl.multiple_of(j * t, t), t), :]
        col_pos = (j * t + lax.broadcasted_iota(jnp.int32, (1, t), 1)).astype(F32)
        col_term = _tile_lanes(slope, reps) * col_pos
        us = []
        for qm, km in ((q1, k[:, :HD_A]), (q2, k[:, HD_A:])):
            u = lax.dot_general(qm, km, dn, preferred_element_type=F32) + col_term
            us.append(_causal_mask(u) if masked else u)
        return us

    def update(us, j):
        v = v_ref[pl.ds(pl.multiple_of(j * t, t), t), :]
        _online_update(us[0], v, m1_s, l1_s, a1_s, row_shift)
        _online_update(us[1], v, m2_s, l2_s, a2_s, row_shift)

    _causal_chunks(i, scores, update)
    lam = _diff_lambda_vec(lq1_ref, lk1_ref, lq2_ref, lk2_ref, lam_init)
    o = (a1_s[...] / _tile_lanes(l1_s[...], 2)
         - lam * (a2_s[...] / _tile_lanes(l2_s[...], 2)))
    y = o * lax.rsqrt(jnp.mean(o * o, axis=1, keepdims=True) + EPS) * g_ref[...] * (1.0 - lam_init)
    o_ref[...] = y.astype(o_ref.dtype)


def _alibi_slopes(n):
    return jnp.asarray(2.0 ** (-8.0 * np.arange(1, n + 1) / n), dtype=F32)


def _diff_attention(st, gq, gk, gv, lam_params, g_sub, lam_init):
    s_len = st.shape[1]
    t = _pick(s_len, (512, 256, 128))
    nc = s_len // t
    w = 2 * HD_A
    slopes = jnp.broadcast_to((_alibi_slopes(H_A) * LOG2E)[:, None, None], (H_A, 1, LANES))
    vec = pl.BlockSpec((1, HD_A), lambda h, i: (0, 0))
    return pl.pallas_call(
        functools.partial(_diff_kernel, t=t, lam_init=lam_init),
        grid=(H_A, nc),
        in_specs=[pl.BlockSpec((None, t, w), lambda h, i: (gq, i, h)),
                  pl.BlockSpec((None, s_len, w), lambda h, i: (gk, 0, h)),
                  pl.BlockSpec((None, s_len, w), lambda h, i: (gv, 0, h)),
                  pl.BlockSpec((None, 1, LANES), lambda h, i: (h, 0, 0)),
                  vec, vec, vec, vec,
                  pl.BlockSpec((1, w), lambda h, i: (0, 0))],
        out_specs=pl.BlockSpec((t, w), lambda h, i: (i, h)),
        out_shape=jax.ShapeDtypeStruct((s_len, H_A * w), BF16),
        scratch_shapes=[pltpu.VMEM((t, LANES), F32), pltpu.VMEM((t, LANES), F32),
                        pltpu.VMEM((t, w), F32),
                        pltpu.VMEM((t, LANES), F32), pltpu.VMEM((t, LANES), F32),
                        pltpu.VMEM((t, w), F32)],
        compiler_params=_params("parallel", "parallel"),
        name="diff_attention",
    )(st, st, st, slopes, *lam_params, g_sub.reshape(1, w).astype(F32))


PAGE_ROWS = PAGE_SIZE * 8


def _even_sample_tables(n_t, n_new_cols):
    slopes = 2.0 ** (-8.0 * np.arange(1, H_A + 1) / H_A)
    ninf = -np.inf
    col = np.arange(PAGE_ROWS)
    p_col, j_col = col >> 3, col & 7
    ra = np.arange(2 * n_t * H_A)
    m_r, t_r, h_r = ra // (n_t * H_A), (ra // H_A) % n_t, ra % H_A
    j_r = 2 * h_r + m_r
    sl_r = slopes[h_r]
    ta = np.where(j_col[None, :] == j_r[:, None], sl_r[:, None] * p_col[None, :], ninf)
    rb = np.arange(n_t * H_B)
    tb_t, tb_h = rb // H_B, rb % H_B
    tb = np.where(j_col[None, :] == tb_h[:, None], 0.0, ninf)
    ncol = np.arange(n_new_cols)
    u_col, jn_col = ncol >> 3, ncol & 7
    ok_a = (jn_col[None, :] == j_r[:, None]) & (u_col[None, :] <= t_r[:, None])
    tan = np.where(ok_a, -sl_r[:, None] * (t_r[:, None] - u_col[None, :]), ninf)
    ok_b = (jn_col[None, :] == tb_h[:, None]) & (u_col[None, :] <= tb_t[:, None])
    tbn = np.where(ok_b, 0.0, ninf)
    rowa = np.stack([sl_r, t_r.astype(np.float64)], axis=1)
    lane = np.arange(MXU_TILE) & 7
    me = np.stack([(lane[None, :] == (4 * e + h_r)[:, None]).astype(np.float64) for e in range(2)])
    gi = np.arange(MXU_TILE) >> 3
    gt = (gi[:, None] == gi[None, :]).astype(np.float64)
    f = lambda a: jnp.asarray(a, F32)
    return f(ta), f(tb), f(tan), f(tbn), f(rowa), f(me), jnp.asarray(gt, BF16)


def _even_sample_kernel(pt_ref, qa_ref, qb_ref, lfr_ref, lfl_ref, kan_ref, van_ref, kbn_ref, vbn_ref,
                        ta_ref, tb_ref, tan_ref, tbn_ref, rowa_ref, me_ref, gt_ref,
                        lq1_ref, lk1_ref, lq2_ref, lk2_ref, gsub_ref, *rest,
                        n_pg, n_t, p_len, scale, lam_init):
    ka_refs = rest[0 * n_pg:1 * n_pg]
    va_refs = rest[1 * n_pg:2 * n_pg]
    kb_refs = rest[2 * n_pg:3 * n_pg]
    vb_refs = rest[3 * n_pg:4 * n_pg]
    lf_refs = rest[4 * n_pg:5 * n_pg]
    oa_ref, ob_ref = rest[5 * n_pg:5 * n_pg + 2]
    ma_s, la_s, acca_s, mb_s, lb_s, accb_s, carry_s = rest[5 * n_pg + 2:]
    j = pl.program_id(1)
    n_groups = pl.num_programs(1)
    n_ra = 2 * n_t * H_A
    dn = (((1,), (1,)), ((), ()))
    qa = qa_ref[...]
    qb = qb_ref[...]
    slope = rowa_ref[:, 0:1]
    t_row = rowa_ref[:, 1:2]
    gt = gt_ref[...]

    @pl.when(j == 0)
    def _():
        for m_s, l_s, a_s in ((ma_s, la_s, acca_s), (mb_s, lb_s, accb_s)):
            m_s[...] = jnp.full_like(m_s, NEG_INF)
            l_s[...] = jnp.zeros_like(l_s)
            a_s[...] = jnp.zeros_like(a_s)
        carry_s[...] = jnp.zeros_like(carry_s)

    def softmax_step(s, m_s, l_s):
        m_prev = m_s[...]
        m_new = jnp.maximum(m_prev, jnp.max(s, axis=1, keepdims=True))
        alpha = jnp.exp(m_prev - m_new)
        p = jnp.exp(s - m_new)
        l_s[...] = alpha * l_s[...] + jnp.sum(p, axis=1, keepdims=True)
        m_s[...] = m_new
        return p, alpha

    def spread(p):
        n_tiles = p.shape[1] // MXU_TILE
        pb = p.astype(BF16)
        stacked = jnp.concatenate(
            [pb[:, c * MXU_TILE:(c + 1) * MXU_TILE] for c in range(n_tiles)], axis=0)
        rep = jnp.dot(stacked, gt, preferred_element_type=F32)
        halves = []
        for e in range(2):
            me = me_ref[e]
            halves.append(jnp.concatenate(
                [rep[c * n_ra:(c + 1) * n_ra] * me for c in range(n_tiles)], axis=1))
        return jnp.concatenate(halves, axis=0).astype(BF16)

    def attend_a(s, vs):
        p, alpha = softmax_step(s, ma_s, la_s)
        p2 = spread(p)
        pv = None
        for g, v in enumerate(vs):
            rows = v.shape[0]
            y = jnp.dot(p2[:, g * rows:(g + 1) * rows], v, preferred_element_type=F32)
            pv = y if pv is None else pv + y
        acca_s[...] = jnp.concatenate([alpha, alpha], axis=0) * acca_s[...] + pv

    def attend_b(s, vs):
        p, alpha = softmax_step(s, mb_s, lb_s)
        pb = p.astype(BF16)
        pv = None
        for g, v in enumerate(vs):
            rows = v.shape[0]
            y = jnp.dot(pb[:, g * rows:(g + 1) * rows], v, preferred_element_type=F32)
            pv = y if pv is None else pv + y
        accb_s[...] = alpha * accb_s[...] + pv

    def scores(q, ks):
        return jnp.concatenate(
            [lax.dot_general(q, k, dn, preferred_element_type=F32) for k in ks], axis=1) * scale

    lfr = lfr_ref[...]
    parts = [lfr[0:H_B]]
    for t in range(1, n_t):
        parts.append(parts[-1] + lfr[t * H_B:(t + 1) * H_B])
    cn_col = jnp.concatenate(parts, axis=0)

    group = n_groups - 1 - j
    ta = ta_ref[...]
    bias_a = jnp.concatenate(
        [ta + slope * ((group * n_pg + g).astype(F32) * float(PAGE_SIZE) - (p_len + t_row))
         for g in range(n_pg)], axis=1)
    attend_a(scores(qa, [r[...].astype(BF16) for r in ka_refs]) + bias_a,
             [r[...].astype(BF16) for r in va_refs])

    lane = lax.broadcasted_iota(jnp.int32, (H_B, PAGE_ROWS), 1)
    tail = carry_s[...]
    sufs = [None] * n_pg
    for g in range(n_pg - 1, -1, -1):
        lf = lf_refs[g][...]
        inc = lf
        shift = 8
        while shift < PAGE_ROWS:
            inc = inc + jnp.where(lane < PAGE_ROWS - shift,
                                  pltpu.roll(inc, PAGE_ROWS - shift, axis=1), 0.0)
            shift *= 2
        sufs[g] = (inc - lf) + tail
        tail = tail + inc[:, 0:1]
    carry_s[...] = tail
    suf = jnp.concatenate(sufs, axis=1)
    bias_b = (jnp.concatenate([suf] * n_t, axis=0) + cn_col
              + jnp.concatenate([tb_ref[...]] * n_pg, axis=1))
    attend_b(scores(qb, [r[...].astype(BF16) for r in kb_refs]) + bias_b,
             [r[...].astype(BF16) for r in vb_refs])

    @pl.when(j == n_groups - 1)
    def _():
        attend_a(scores(qa, [kan_ref[...]]) + tan_ref[...], [van_ref[...]])
        lfl = lfl_ref[...]
        ln = lax.broadcasted_iota(jnp.int32, lfl.shape, 1)
        cn_lane = lfl
        shift = H_B
        while shift < n_t * H_B:
            cn_lane = cn_lane + jnp.where(ln >= shift, pltpu.roll(cn_lane, shift, axis=1), 0.0)
            shift *= 2
        n_new = tbn_ref.shape[1]
        cn_keys = _tile_lanes(cn_lane, n_new // LANES)
        attend_b(scores(qb, [kbn_ref[...]]) + (tbn_ref[...] + (cn_col - cn_keys)), [vbn_ref[...]])

        lam = _diff_lambda_vec(lq1_ref, lk1_ref, lq2_ref, lk2_ref, lam_init)
        la = la_s[...]
        fa = acca_s[...] / jnp.concatenate([la, la], axis=0)
        half = n_ra // 2
        o = [fa[e * n_ra:e * n_ra + half] - lam * fa[e * n_ra + half:(e + 1) * n_ra]
             for e in range(2)]
        ms = (jnp.sum(o[0] * o[0], axis=1, keepdims=True)
              + jnp.sum(o[1] * o[1], axis=1, keepdims=True)) / (2.0 * HD_A)
        inv = lax.rsqrt(ms + EPS)
        for e in range(2):
            oa_ref[e] = o[e] * inv * gsub_ref[:, e * HD_A:(e + 1) * HD_A] * (1.0 - lam_init)
        ob_ref[...] = accb_s[...] / lb_s[...]


def _even_sample_attention(qa2, qb2, lf_rows, lf_lanes, new_kv, pools, page_table, pool_off,
                           lam_params, g_sub, lam_init):
    pool_ak, pool_av, pool_bk, pool_bv, pool_lf = pools
    bsz, n_ra, hd = qa2.shape
    n_rb = qb2.shape[1]
    n_t = n_rb // H_B
    n_new = new_kv[0].shape[1]
    n_pages = page_table.shape[1]
    n_pg = _pick(n_pages, (8, 4, 2, 1))
    n_groups = n_pages // n_pg
    tables = _even_sample_tables(n_t, n_new)

    def page_map(g):
        def index(b, j, pt):
            return (pool_off + pt[b * n_pages + (n_groups - 1 - j) * n_pg + g], 0, 0)
        return index

    per_b = lambda b, j, pt: (b, 0, 0)
    const2 = lambda b, j, pt: (0, 0)
    const3 = lambda b, j, pt: (0, 0, 0)
    vec = pl.BlockSpec((1, HD_A), const2)
    in_specs = [pl.BlockSpec((None, n_ra, hd), per_b),
                pl.BlockSpec((None, n_rb, hd), per_b),
                pl.BlockSpec((None, n_rb, 1), per_b),
                pl.BlockSpec((None, 1, LANES), per_b)]
    in_specs += [pl.BlockSpec((None, n_new, hd), per_b)] * 4
    in_specs += [pl.BlockSpec(tables[0].shape, const2), pl.BlockSpec(tables[1].shape, const2),
                 pl.BlockSpec(tables[2].shape, const2), pl.BlockSpec(tables[3].shape, const2),
                 pl.BlockSpec(tables[4].shape, const2), pl.BlockSpec(tables[5].shape, const3),
                 pl.BlockSpec(tables[6].shape, const2)]
    in_specs += [vec, vec, vec, vec, pl.BlockSpec((1, 2 * HD_A), const2)]
    args = [qa2, qb2, lf_rows, lf_lanes, *new_kv, *tables, *lam_params,
            g_sub.reshape(1, 2 * HD_A).astype(F32)]
    for pool in (pool_ak, pool_av, pool_bk, pool_bv):
        for g in range(n_pg):
            in_specs.append(pl.BlockSpec((None, PAGE_ROWS, hd), page_map(g)))
            args.append(pool)
    for g in range(n_pg):
        in_specs.append(pl.BlockSpec((None, H_B, PAGE_ROWS), page_map(g)))
        args.append(pool_lf)
    grid_spec = pltpu.PrefetchScalarGridSpec(
        num_scalar_prefetch=1,
        grid=(bsz, n_groups),
        in_specs=in_specs,
        out_specs=[pl.BlockSpec((None, 2, n_ra // 2, hd), lambda b, j, pt: (b, 0, 0, 0)),
                   pl.BlockSpec((None, n_rb, hd), per_b)],
        scratch_shapes=[pltpu.VMEM((n_ra, 1), F32), pltpu.VMEM((n_ra, 1), F32),
                        pltpu.VMEM((2 * n_ra, hd), F32),
                        pltpu.VMEM((n_rb, 1), F32), pltpu.VMEM((n_rb, 1), F32),
                        pltpu.VMEM((n_rb, hd), F32),
                        pltpu.VMEM((H_B, 1), F32)])
    return pl.pallas_call(
        functools.partial(_even_sample_kernel, n_pg=n_pg, n_t=n_t,
                          p_len=float(n_pages * PAGE_SIZE), scale=HD_A ** -0.5,
                          lam_init=lam_init),
        grid_spec=grid_spec,
        out_shape=[jax.ShapeDtypeStruct((bsz, 2, n_ra // 2, hd), F32),
                   jax.ShapeDtypeStruct((bsz, n_rb, hd), F32)],
        compiler_params=_params("parallel", "arbitrary"),
        name="even_sample_attention",
    )(page_table.reshape(-1), *args)


def _pattern_count(dist):
    return sum(((dist >= 0) & (dist <= w) & (dist % d == 0)).astype(jnp.int32)
               for w, d in C_PATTERNS)


def _count_bias(dist, slopes, same_head=True):
    cnt = _pattern_count(dist)
    logc = jnp.where((cnt > 0) & same_head, jnp.log(jnp.maximum(cnt, 1).astype(F32)), NEG_INF)
    return logc - slopes * dist.astype(F32)


def _dilated_bias(t, n_off):
    idx = jnp.arange(t, dtype=jnp.int32)
    dist = (idx[None, :, None] - idx[None, None, :]
            + t * jnp.arange(n_off, dtype=jnp.int32)[:, None, None])
    return _count_bias(dist[None], _alibi_slopes(H_C)[:, None, None, None]) * LOG2E


def _dil_kernel(q_ref, k_ref, v_ref, bias_ref, o_ref, m_s, l_s, acc_s, *, t, n_off):
    i = pl.program_id(1)
    q = q_ref[...]
    m_s[...] = jnp.full_like(m_s, NEG_INF)
    l_s[...] = jnp.zeros_like(l_s)
    acc_s[...] = jnp.zeros_like(acc_s)
    dn = (((1,), (1,)), ((), ()))
    no_shift = jnp.zeros((t, LANES), F32)

    def scores(off):
        k = k_ref[pl.ds(pl.multiple_of((i - off) * t, t), t), :]
        return lax.dot_general(q, k, dn, preferred_element_type=F32) + bias_ref[off]

    def update(u, off):
        v = v_ref[pl.ds(pl.multiple_of((i - off) * t, t), t), :]
        _online_update(u, v, m_s, l_s, acc_s, no_shift)

    @pl.when(i >= n_off - 1)
    def _():
        us = [scores(off) for off in range(n_off)]
        for off in range(n_off):
            update(us[off], off)

    @pl.when(i < n_off - 1)
    def _():
        for off in range(n_off - 1):
            @pl.when(i >= off)
            def _(off=off):
                update(scores(off), off)

    o_ref[...] = (acc_s[...] / l_s[...]).astype(o_ref.dtype)


def _dilated_attention(st):
    s_len = st.shape[1]
    per = st.shape[2] // HD_C
    t = _pick(s_len, (512, 256, 128))
    n_off = min(C_WMAX // t + 1, s_len // t)
    bias = _dilated_bias(t, n_off)
    return pl.pallas_call(
        functools.partial(_dil_kernel, t=t, n_off=n_off),
        grid=(H_C, s_len // t),
        in_specs=[pl.BlockSpec((None, t, HD_C), lambda h, i: (h // per, i, h % per)),
                  pl.BlockSpec((None, s_len, HD_C), lambda h, i: (2 + h // per, 0, h % per)),
                  pl.BlockSpec((None, s_len, HD_C), lambda h, i: (4 + h // per, 0, h % per)),
                  pl.BlockSpec((None, n_off, t, t), lambda h, i: (h, 0, 0, 0))],
        out_specs=pl.BlockSpec((t, HD_C), lambda h, i: (i, h)),
        out_shape=jax.ShapeDtypeStruct((s_len, H_C * HD_C), BF16),
        scratch_shapes=[pltpu.VMEM((t, LANES), F32), pltpu.VMEM((t, LANES), F32),
                        pltpu.VMEM((t, HD_C), F32)],
        compiler_params=_params("parallel", "parallel"),
        name="dilated_attention",
    )(st, st, st, bias)


def _c_sample_bias(buf_len, n_t, n_new_cols):
    row = jnp.arange(n_t * H_C, dtype=jnp.int32)[:, None]
    t_row, h_row = row // H_C, row % H_C
    slopes = jnp.tile(_alibi_slopes(H_C), n_t)[:, None]

    def table(n_pos, pos0):
        col = jnp.arange(n_pos * H_C, dtype=jnp.int32)[None, :]
        dist = buf_len + t_row - (pos0 + col // H_C)
        return _count_bias(dist, slopes, h_row == col % H_C)

    past = table(buf_len, 0)
    new = table(n_t, buf_len)
    new = jnp.pad(new, ((0, 0), (0, n_new_cols - new.shape[1])), constant_values=NEG_INF)
    return past, new


def _c_sample_kernel(q_ref, kc_ref, kx_ref, kn_ref, vc_ref, vx_ref, vn_ref, bias_ref, biasn_ref,
                     ok_ref, ov_ref, o_ref, m_s, l_s, acc_s, *, rows, new, scale):
    c = pl.program_id(1)
    last = pl.num_programs(1) - 1
    dn = (((1,), (1,)), ((), ()))
    q = q_ref[...]

    @pl.when(c == 0)
    def _():
        m_s[...] = jnp.full_like(m_s, NEG_INF)
        l_s[...] = jnp.zeros_like(l_s)
        acc_s[...] = jnp.zeros_like(acc_s)

    for src, nxt, fresh, dst in ((kc_ref, kx_ref, kn_ref, ok_ref), (vc_ref, vx_ref, vn_ref, ov_ref)):
        dst[pl.ds(0, rows - new), :] = src[pl.ds(new, rows - new), :]

        @pl.when(c < last)
        def _(nxt=nxt, dst=dst):
            dst[pl.ds(rows - new, new), :] = nxt[...]

        @pl.when(c == last)
        def _(fresh=fresh, dst=dst):
            dst[pl.ds(rows - new, new), :] = fresh[...]

    def attend(s, v):
        m_prev = m_s[...]
        m_new = jnp.maximum(m_prev, jnp.max(s, axis=1, keepdims=True))
        alpha = jnp.exp(m_prev - m_new)
        p = jnp.exp(s - m_new)
        l_s[...] = alpha * l_s[...] + jnp.sum(p, axis=1, keepdims=True)
        acc_s[...] = alpha * acc_s[...] + jnp.dot(p.astype(BF16), v, preferred_element_type=F32)
        m_s[...] = m_new

    s = lax.dot_general(q, kc_ref[...].astype(BF16), dn, preferred_element_type=F32)
    attend(s * scale + bias_ref[...], vc_ref[...].astype(BF16))

    @pl.when(c == last)
    def _():
        pad = jnp.zeros((biasn_ref.shape[1] - new, kn_ref.shape[1]), F32)
        kn = jnp.concatenate([kn_ref[...], pad], axis=0).astype(BF16)
        vn = jnp.concatenate([vn_ref[...], pad], axis=0).astype(BF16)
        s_n = lax.dot_general(q, kn, dn, preferred_element_type=F32)
        attend(s_n * scale + biasn_ref[...], vn)
        o_ref[...] = acc_s[...] / l_s[...]


def _c_sample(q2, cache_k, cache_v, k_new, v_new, batch_off, n_t):
    bsz, n_r, hd = q2.shape
    total = cache_k.shape[1]
    new = k_new.shape[1]
    buf_len = total // H_C
    rows = _pick(total, (8192, 4096, 2048, 1024))
    assert rows % new == 0 and total % rows == 0
    n_chunks = total // rows
    n_new_cols = max(LANES, new)
    bias, bias_n = _c_sample_bias(buf_len, n_t, n_new_cols)
    blocks_per_chunk = rows // new
    n_small = total // new

    cur = pl.BlockSpec((None, rows, hd), lambda b, c: (batch_off + b, c, 0))
    nxt = pl.BlockSpec((None, new, hd),
                       lambda b, c: (batch_off + b, jnp.minimum((c + 1) * blocks_per_chunk,
                                                                n_small - 1), 0))
    per_b = pl.BlockSpec((None, new, hd), lambda b, c: (b, 0, 0))
    out_buf = pl.BlockSpec((None, rows, hd), lambda b, c: (b, c, 0))
    return pl.pallas_call(
        functools.partial(_c_sample_kernel, rows=rows, new=new, scale=HD_C ** -0.5),
        grid=(bsz, n_chunks),
        in_specs=[pl.BlockSpec((None, n_r, hd), lambda b, c: (b, 0, 0)),
                  cur, nxt, per_b, cur, nxt, per_b,
                  pl.BlockSpec((n_r, rows), lambda b, c: (0, c)),
                  pl.BlockSpec((n_r, n_new_cols), lambda b, c: (0, 0))],
        out_specs=[out_buf, out_buf, pl.BlockSpec((None, n_r, hd), lambda b, c: (b, 0, 0))],
        out_shape=[jax.ShapeDtypeStruct((bsz, total, hd), F32),
                   jax.ShapeDtypeStruct((bsz, total, hd), F32),
                   jax.ShapeDtypeStruct((bsz, n_r, hd), F32)],
        scratch_shapes=[pltpu.VMEM((n_r, 1), F32), pltpu.VMEM((n_r, 1), F32),
                        pltpu.VMEM((n_r, hd), F32)],
        compiler_params=_params("parallel", "arbitrary"),
        name="window_sample_attention",
    )(q2, cache_k, cache_k, k_new, cache_v, cache_v, v_new, bias, bias_n)


def _cross_kernel(q_ref, k_ref, v_ref, o_ref, *, scale):
    dn = (((1,), (1,)), ((), ()))
    for h in range(H_X):
        cols = slice(h * HD_X, (h + 1) * HD_X)
        s = lax.dot_general(q_ref[:, cols], k_ref[:, cols].astype(BF16), dn,
                            preferred_element_type=F32) * scale
        m = jnp.max(s, axis=1, keepdims=True)
        e = jnp.exp(s - m)
        l = jnp.sum(e, axis=1, keepdims=True)
        o = jnp.dot(e.astype(BF16), v_ref[:, cols].astype(BF16), preferred_element_type=F32)
        o_ref[:, cols] = (o / l).astype(o_ref.dtype)


def _cross_attention(q, mem_k, mem_v):
    bsz, n_q, w = q.shape
    n_m = mem_k.shape[1]
    tq = _pick(n_q, (512, 256, 128))
    return pl.pallas_call(
        functools.partial(_cross_kernel, scale=HD_X ** -0.5),
        grid=(bsz, n_q // tq),
        in_specs=[pl.BlockSpec((None, tq, w), lambda b, i: (b, i, 0)),
                  pl.BlockSpec((None, n_m, w), lambda b, i: (b, 0, 0)),
                  pl.BlockSpec((None, n_m, w), lambda b, i: (b, 0, 0))],
        out_specs=pl.BlockSpec((None, tq, w), lambda b, i: (b, i, 0)),
        out_shape=jax.ShapeDtypeStruct((bsz, n_q, w), BF16),
        compiler_params=_params("parallel", "parallel"),
        name="cross_attention",
    )(q, mem_k, mem_v)


def _cross_rows_kernel(q_ref, k_ref, v_ref, mask_ref, o_ref, *, scale):
    dn = (((1,), (1,)), ((), ()))
    s = lax.dot_general(q_ref[...], k_ref[...].astype(BF16), dn,
                        preferred_element_type=F32) * scale + mask_ref[...]
    m = jnp.max(s, axis=1, keepdims=True)
    e = jnp.exp(s - m)
    l = jnp.sum(e, axis=1, keepdims=True)
    o = jnp.dot(e.astype(BF16), v_ref[...].astype(BF16), preferred_element_type=F32)
    o_ref[...] = o / l


def _cross_attention_rows(q2, mem_k, mem_v, batch_off):
    bsz, n_r, hd = q2.shape
    n_c = mem_k.shape[1]
    same = (np.arange(n_r)[:, None] % H_X) == (np.arange(n_c)[None, :] % H_X)
    mask = jnp.asarray(np.where(same, 0.0, -np.inf), F32)
    kv = pl.BlockSpec((None, n_c, hd), lambda b: (batch_off + b, 0, 0))
    return pl.pallas_call(
        functools.partial(_cross_rows_kernel, scale=HD_X ** -0.5),
        grid=(bsz,),
        in_specs=[pl.BlockSpec((None, n_r, hd), lambda b: (b, 0, 0)), kv, kv,
                  pl.BlockSpec((n_r, n_c), lambda b: (0, 0))],
        out_specs=pl.BlockSpec((None, n_r, hd), lambda b: (b, 0, 0)),
        out_shape=jax.ShapeDtypeStruct((bsz, n_r, hd), F32),
        compiler_params=_params("parallel"),
        name="cross_attention_rows",
    )(q2, mem_k, mem_v, mask)


NEW_KEY_ROWS = MXU_TILE


def _pad_rows(x, rows):
    return jnp.pad(x, ((0, 0), (0, rows - x.shape[1]), (0, 0)))


def kernel(x_prompt, x_sample, mem_prompt, cache_a_k, cache_a_v, cache_b_k, cache_b_v, cache_b_logf, cache_c_k, cache_c_v, cache_mem_k, cache_mem_v, page_table, norm_mix, norm_cross, norm_mem, norm_ffn, norm_final, w_in_even, b_forget, lambda_q1, lambda_k1, lambda_q2, lambda_k2, subln_a, w_out_even, w_in_odd, w_out_odd, w_xq, w_xkv, w_xo, w_gate_up, w_down):
    n_b, s_len, d_model = x_prompt.shape
    d_b, d_t, _ = x_sample.shape
    assert n_b == 1
    depth = norm_mix.shape[0]
    n_mem = mem_prompt.shape[1]
    xp = x_prompt.reshape(s_len, d_model)
    xs = x_sample.reshape(d_b * d_t, d_model)
    mem = mem_prompt.reshape(n_mem, d_model)
    wa = H_A * 2 * HD_A
    wb = H_B * HD_B
    wc = H_C * HD_C
    wx = H_X * HD_X

    n_even, n_pool = cache_a_k.shape[:2]
    pool_ak = cache_a_k.reshape(n_even * n_pool, PAGE_ROWS, HD_A)
    pool_av = (cache_a_v.reshape(n_even, n_pool, PAGE_SIZE, H_A, 2, HD_A)
               .transpose(0, 1, 2, 4, 3, 5).reshape(n_even * n_pool, PAGE_ROWS, HD_A))
    pool_bk = cache_b_k.reshape(n_even * n_pool, PAGE_ROWS, HD_B)
    pool_bv = cache_b_v.reshape(n_even * n_pool, PAGE_ROWS, HD_B)
    pool_lf = jnp.repeat(jnp.swapaxes(cache_b_logf, 2, 3), H_B, axis=3).reshape(
        n_even * n_pool, H_B, PAGE_ROWS)
    n_odd, _, buf_len = cache_c_k.shape[:3]
    win_k = cache_c_k.reshape(n_odd * d_b, buf_len * H_C, HD_C)
    win_v = cache_c_v.reshape(n_odd * d_b, buf_len * H_C, HD_C)
    memc_k = cache_mem_k.reshape(depth * d_b, n_mem * H_X, HD_X)
    memc_v = cache_mem_v.reshape(depth * d_b, n_mem * H_X, HD_X)

    ak_p, av_p, bk_p, bv_p, bl_p, ck_p, cv_p, mk_p, mv_p = [], [], [], [], [], [], [], [], []
    ak_s, av_s, bk_s, bv_s, bl_s, ck_s, cv_s = [], [], [], [], [], [], []

    w_in_even16, w_out_even16 = w_in_even.astype(BF16), w_out_even.astype(BF16)
    w_in_odd16, w_out_odd16 = w_in_odd.astype(BF16), w_out_odd.astype(BF16)
    w_xq16, w_xkv16, w_xo16 = w_xq.astype(BF16), w_xkv.astype(BF16), w_xo.astype(BF16)
    w_gate_up16, w_down16 = w_gate_up.astype(BF16), w_down.astype(BF16)
    gw = 1024

    for l in range(depth):
        j = l // 2
        if l % 2 == 0:
            lam_init = 0.8 - 0.6 * math.exp(-0.3 * l)
            w_fz = jnp.pad(w_in_even[j][:, 6 * wa:], ((0, 0), (0, LANES - H_B))).astype(BF16)
            lam_params = [p[j].reshape(1, HD_A).astype(F32)
                          for p in (lambda_q1, lambda_k1, lambda_q2, lambda_k2)]
            assert wa == gw and wb == gw and HD_A == HD_B
            ident = tuple(range(8))
            va_perm = tuple((c % 2) * H_A + c // 2 for c in range(8))
            proj = functools.partial(_norm_proj, gain=norm_mix[l], w3=w_in_even16, layer=j,
                                     n_groups=6, gw=gw, q_groups=(0, 3), w_gate=w_fz,
                                     row_perms={1: ident, 2: va_perm, 4: ident, 5: ident})
            unperm_v = lambda a, lead: (a.reshape(*lead, 2, H_A, HD_A)
                                        .swapaxes(-3, -2).reshape(*lead, H_A, 2 * HD_A))

            st16, st32, fz = proj(xp, q_scale=HD_A ** -0.5 * LOG2E)
            lf_t, c_t = _logf_cumsum(fz[:, :H_B].T, b_forget[j])
            za = _diff_attention(st16, 0, 1, 2, lam_params, subln_a[j], lam_init)
            zb = _fox_attention(st16, 3, 4, 5, c_t)
            xp = _matmul_res([(za, w_out_even16, j, 0), (zb, w_out_even16, j, 1)], xp)
            ak_p.append(st32[0].reshape(1, s_len, H_A, 2, HD_A))
            av_p.append(unperm_v(st32[1], (1, s_len)))
            bk_p.append(st32[2].reshape(1, s_len, H_B, HD_B))
            bv_p.append(st32[3].reshape(1, s_len, H_B, HD_B))
            bl_p.append(lf_t.T.reshape(1, s_len, H_B))

            ss16, ss32, fz_s = proj(xs)
            qa_s, ka_s16, va_s16, qb_s, kb_s16, vb_s16 = (ss16[g] for g in range(6))
            ka_s, va_s, kb_s, vb_s = (ss32[g] for g in range(4))
            lf_s_t, _ = _logf_cumsum(fz_s[:, :H_B].T, b_forget[j])
            lf_s = lf_s_t.T
            qa2 = (qa_s.reshape(d_b, d_t, H_A, 2, HD_A).transpose(0, 3, 1, 2, 4)
                   .reshape(d_b, 2 * d_t * H_A, HD_A))
            qb2 = qb_s.reshape(d_b, d_t * H_B, HD_B)
            va_rows = (va_s16.reshape(d_b, d_t, H_A, 2, HD_A).transpose(0, 1, 3, 2, 4)
                       .reshape(d_b, d_t * 8, HD_A))
            new_kv = [_pad_rows(a, NEW_KEY_ROWS)
                      for a in (ka_s16.reshape(d_b, d_t * 8, HD_A), va_rows,
                                kb_s16.reshape(d_b, d_t * H_B, HD_B),
                                vb_s16.reshape(d_b, d_t * H_B, HD_B))]
            lf_rows = lf_s.reshape(d_b, d_t * H_B, 1)
            lf_lanes = _pad_rows(lf_rows, LANES).reshape(d_b, 1, LANES)
            oa_s, ob_s = _even_sample_attention(
                qa2, qb2, lf_rows, lf_lanes, new_kv,
                (pool_ak, pool_av, pool_bk, pool_bv, pool_lf), page_table, j * n_pool,
                lam_params, subln_a[j], lam_init)
            za_s = (oa_s.reshape(d_b, 2, d_t, H_A, HD_A).transpose(0, 2, 3, 1, 4)
                    .reshape(d_b * d_t, wa).astype(BF16))
            zb_s = ob_s.reshape(d_b * d_t, wb).astype(BF16)
            xs = _matmul_res([(za_s, w_out_even16, j, 0), (zb_s, w_out_even16, j, 1)], xs)
            ak_s.append(ka_s.reshape(d_b, d_t, H_A, 2, HD_A))
            av_s.append(unperm_v(va_s, (d_b, d_t)))
            bk_s.append(kb_s.reshape(d_b, d_t, H_B, HD_B))
            bv_s.append(vb_s.reshape(d_b, d_t, H_B, HD_B))
            bl_s.append(lf_s.reshape(d_b, d_t, H_B))
        else:
            assert wc == 2 * gw
            proj = functools.partial(_norm_proj, gain=norm_mix[l], w3=w_in_odd16, layer=j,
                                     n_groups=6, gw=gw, q_groups=(0, 1))
            halves = lambda s32, a: jnp.concatenate([s32[a], s32[a + 1]], axis=1)

            st16, st32 = proj(xp, q_scale=HD_C ** -0.5 * LOG2E)
            o = _dilated_attention(st16)
            xp = _matmul_res([(o, w_out_odd16, j, 0)], xp)
            keep = min(C_WMAX, s_len)
            ck_p.append(halves(st32[:, s_len - keep:], 0).reshape(1, keep, H_C, HD_C))
            cv_p.append(halves(st32[:, s_len - keep:], 2).reshape(1, keep, H_C, HD_C))

            ss16, ss32 = proj(xs)
            q_s = jnp.concatenate([ss16[0], ss16[1]], axis=1)
            k_s, v_s = halves(ss32, 0), halves(ss32, 2)
            new_k, new_v, o_s = _c_sample(
                q_s.reshape(d_b, d_t * H_C, HD_C), win_k, win_v,
                k_s.reshape(d_b, d_t * H_C, HD_C), v_s.reshape(d_b, d_t * H_C, HD_C),
                j * d_b, d_t)
            xs = _matmul_res([(o_s.reshape(d_b * d_t, wc).astype(BF16), w_out_odd16, j, 0)], xs)
            ck_s.append(new_k.reshape(d_b, buf_len, H_C, HD_C))
            cv_s.append(new_v.reshape(d_b, buf_len, H_C, HD_C))

        mem_kv = _norm_proj(mem, norm_mem[l], w_xkv16, l, 2, wx, want_bf16=False)
        mk_p.append(mem_kv[0].reshape(1, n_mem, H_X, HD_X))
        mv_p.append(mem_kv[1].reshape(1, n_mem, H_X, HD_X))
        qx = _norm_proj(xp, norm_cross[l], w_xq16, l, 1, wx, want_f32=False)
        ox = _cross_attention(qx, mem_kv[0:1], mem_kv[1:2])
        xp = _matmul_res([(ox.reshape(s_len, wx), w_xo16, l, 0)], xp)
        qx_s = _norm_proj(xs, norm_cross[l], w_xq16, l, 1, wx, want_f32=False)
        ox_s = _cross_attention_rows(qx_s.reshape(d_b, d_t * H_X, HD_X), memc_k, memc_v, l * d_b)
        xs = _matmul_res([(ox_s.reshape(d_b * d_t, wx).astype(BF16), w_xo16, l, 0)], xs)

        final_gain = norm_final if l == depth - 1 else None
        xp = _ffn(xp, norm_ffn[l], w_gate_up16, w_down16, l, final_gain)
        xs = _ffn(xs, norm_ffn[l], w_gate_up16, w_down16, l, final_gain)

    y_prompt = xp.reshape(1, s_len, d_model)
    y_sample = xs.reshape(d_b, d_t, d_model)
    st = jnp.stack
    return (y_prompt, y_sample,
            st(ak_p), st(av_p), st(bk_p), st(bv_p), st(bl_p), st(ck_p), st(cv_p), st(mk_p), st(mv_p),
            st(ak_s), st(av_s), st(bk_s), st(bv_s), st(bl_s), st(ck_s), st(cv_s))
```
